```python
import math
import jax, jax.numpy as jnp
from jax import lax
import numpy as np


D_MODEL = 1024
BATCH = 1
SEQ = 16384
DEPTH = 2
DEC_BATCH = 128
DEC_SEQ = 4
PAST_LEN = 16384
PAGE_SIZE = 128

N_MIXERS = 2
N_CONV_LAYERS = (DEPTH + 1) // 2
N_ATTN_LAYERS = DEPTH // 2
CONV_WIDTH = 3
CONV_STATE = CONV_WIDTH - 1
HEAD_DIM = 64
N_HEADS = D_MODEL // HEAD_DIM
N_KV_HEADS = 2
GROUP = N_HEADS // N_KV_HEADS
WINDOW = 128
BLOCK = WINDOW
N_BUCKETS = 32
MAX_DISTANCE = 128
D_FF = -(-8 * D_MODEL // (3 * 256)) * 256
EPS = 1e-5
NEG_INF = -1e30
ATTN_SCALE = 1.0 / math.sqrt(HEAD_DIM)

kernel_name = 'hybrid_shortconv_swa_sink_decoder_step'


def rms_norm(x, g):
    xf = x.astype(jnp.float32)
    y = xf * lax.rsqrt(jnp.mean(xf * xf, axis=-1, keepdims=True) + EPS)
    return (y * g.astype(jnp.float32)).astype(x.dtype)


def swiglu(x, w_gate, w_up, w_down):
    return (jax.nn.silu(x @ w_gate) * (x @ w_up)) @ w_down


def t5_bucket(dist):
    n = jnp.maximum(dist, 0)
    max_exact = N_BUCKETS // 2
    nf = jnp.maximum(n, 1).astype(jnp.float32)
    large = max_exact + (jnp.log(nf / max_exact) / math.log(MAX_DISTANCE / max_exact)
                         * (N_BUCKETS - max_exact)).astype(jnp.int32)
    large = jnp.minimum(large, N_BUCKETS - 1)
    return jnp.where(n < max_exact, n, large)


def rel_bias(dist, rel_table):
    b = rel_table[t5_bucket(dist)].astype(jnp.float32)
    return jnp.transpose(b, (2, 0, 1)).reshape(N_KV_HEADS, GROUP, dist.shape[0], dist.shape[1])


def sink_softmax(logits, sinks):
    s = jnp.broadcast_to(sinks.astype(jnp.float32).reshape(N_KV_HEADS, GROUP, 1, 1),
                         logits.shape[:-1] + (1,))
    p = jax.nn.softmax(jnp.concatenate([logits, s], axis=-1), axis=-1)
    return p[..., :-1]


def conv_mixer(h, prev, w_conv_in, conv_w, w_conv_out):
    T = h.shape[1]
    b, c, xi = jnp.split(h @ w_conv_in, 3, axis=-1)
    u = c * xi
    up = jnp.concatenate([prev.astype(u.dtype), u], axis=1)
    v = conv_w[0] * up[:, 0:T] + conv_w[1] * up[:, 1:T + 1] + conv_w[2] * up[:, 2:T + 2]
    y = (b * v) @ w_conv_out
    return y, up[:, -CONV_STATE:]


def swa_prompt(h, w_q, w_k, w_v, w_o, sinks, rel_table):
    N, S, _ = h.shape
    nb = S // BLOCK
    q = (h @ w_q).reshape(N, nb, BLOCK, N_KV_HEADS, GROUP, HEAD_DIM)
    k = (h @ w_k).reshape(N, S, N_KV_HEADS, HEAD_DIM)
    v = (h @ w_v).reshape(N, S, N_KV_HEADS, HEAD_DIM)
    pad = jnp.zeros((N, BLOCK, N_KV_HEADS, HEAD_DIM), k.dtype)
    kp = jnp.concatenate([pad, k], axis=1).reshape(N, nb + 1, BLOCK, N_KV_HEADS, HEAD_DIM)
    vp = jnp.concatenate([pad, v], axis=1).reshape(N, nb + 1, BLOCK, N_KV_HEADS, HEAD_DIM)
    kb = jnp.concatenate([kp[:, :-1], kp[:, 1:]], axis=2)
    vb = jnp.concatenate([vp[:, :-1], vp[:, 1:]], axis=2)
    qi = jnp.arange(BLOCK)[:, None] + BLOCK
    kj = jnp.arange(2 * BLOCK)[None, :]
    dist = qi - kj
    band = (dist >= 0) & (dist <= WINDOW)
    blk = jnp.arange(nb)[:, None, None]
    valid = band[None] & ((blk > 0) | (kj >= BLOCK)[None])
    bias = rel_bias(dist, rel_table)
    logits = jnp.einsum('nbqkgd,nbskd->nbkgqs', q, kb,
                        preferred_element_type=jnp.float32) * ATTN_SCALE + bias
    logits = jnp.where(valid[None, :, None, None], logits, NEG_INF)
    p = sink_softmax(logits, sinks)
    o = jnp.einsum('nbkgqs,nbskd->nbqkgd', p.astype(vb.dtype), vb)
    y = o.reshape(N, S, D_MODEL) @ w_o
    return y, k[:, -WINDOW:], v[:, -WINDOW:]


def swa_sample(h, k_prev, v_prev, w_q, w_k, w_v, w_o, sinks, rel_table):
    N, T, _ = h.shape
    q = (h @ w_q).reshape(N, T, N_KV_HEADS, GROUP, HEAD_DIM)
    k_new = (h @ w_k).reshape(N, T, N_KV_HEADS, HEAD_DIM)
    v_new = (h @ w_v).reshape(N, T, N_KV_HEADS, HEAD_DIM)
    kc = jnp.concatenate([k_prev.astype(k_new.dtype), k_new], axis=1)
    vc = jnp.concatenate([v_prev.astype(v_new.dtype), v_new], axis=1)
    dist = jnp.arange(T)[:, None] - (jnp.arange(WINDOW + T)[None, :] - WINDOW)
    valid = (dist >= 0) & (dist <= WINDOW)
    bias = rel_bias(dist, rel_table)
    logits = jnp.einsum('nqkgd,nskd->nkgqs', q, kc,
                        preferred_element_type=jnp.float32) * ATTN_SCALE + bias
    logits = jnp.where(valid, logits, NEG_INF)
    p = sink_softmax(logits, sinks)
    o = jnp.einsum('nkgqs,nskd->nqkgd', p.astype(vc.dtype), vc)
    y = o.reshape(N, T, D_MODEL) @ w_o
    return y, kc[:, -WINDOW:], vc[:, -WINDOW:]


def trunk(x, conv_prev, k_prev, v_prev, g_mix, g_ffn, g_final, w_conv_in, conv_w, w_conv_out,
          w_q, w_k, w_v, w_o, sinks, rel_table, w_gate, w_up, w_down):
    conv_states, k_states, v_states = [], [], []
    for i in range(DEPTH):
        j = i // N_MIXERS
        h = rms_norm(x, g_mix[i])
        if i % N_MIXERS == 0:
            y, st = conv_mixer(h, conv_prev[j], w_conv_in[j], conv_w[j], w_conv_out[j])
            conv_states.append(st)
        else:
            if k_prev is None:
                y, ks, vs = swa_prompt(h, w_q[j], w_k[j], w_v[j], w_o[j], sinks[j], rel_table)
            else:
                y, ks, vs = swa_sample(h, k_prev[j], v_prev[j], w_q[j], w_k[j], w_v[j], w_o[j],
                                       sinks[j], rel_table)
            k_states.append(ks)
            v_states.append(vs)
        x = x + y
        x = x + swiglu(rms_norm(x, g_ffn[i]), w_gate[i], w_up[i], w_down[i])
    return rms_norm(x, g_final), jnp.stack(conv_states), jnp.stack(k_states), jnp.stack(v_states)


def setup_inputs(seed: int = 0) -> dict:
    key = jax.random.key(seed)
    ks = jax.random.split(key, 24)
    nrm = lambda k, shape, s: jax.random.normal(k, shape, jnp.float32) * s
    D = D_MODEL
    return {
        'x_prompt': nrm(ks[0], (BATCH, SEQ, D), 1.0),
        'x_sample': nrm(ks[1], (DEC_BATCH, DEC_SEQ, D), 1.0),
        'state_conv': nrm(ks[2], (N_CONV_LAYERS, DEC_BATCH, CONV_STATE, D), 1.0),
        'cache_k': nrm(ks[3], (N_ATTN_LAYERS, DEC_BATCH, WINDOW, N_KV_HEADS, HEAD_DIM), 1.0),
        'cache_v': nrm(ks[4], (N_ATTN_LAYERS, DEC_BATCH, WINDOW, N_KV_HEADS, HEAD_DIM), 1.0),
        'g_mix': 1.0 + nrm(ks[5], (DEPTH, D), 0.05),
        'g_ffn': 1.0 + nrm(ks[6], (DEPTH, D), 0.05),
        'g_final': 1.0 + nrm(ks[7], (D,), 0.05),
        'w_conv_in': nrm(ks[8], (N_CONV_LAYERS, D, 3 * D), D ** -0.5),
        'conv_w': nrm(ks[9], (N_CONV_LAYERS, CONV_WIDTH, D), CONV_WIDTH ** -0.5),
        'w_conv_out': nrm(ks[10], (N_CONV_LAYERS, D, D), D ** -0.5),
        'w_q': nrm(ks[11], (N_ATTN_LAYERS, D, N_HEADS * HEAD_DIM), D ** -0.5),
        'w_k': nrm(ks[12], (N_ATTN_LAYERS, D, N_KV_HEADS * HEAD_DIM), D ** -0.5),
        'w_v': nrm(ks[13], (N_ATTN_LAYERS, D, N_KV_HEADS * HEAD_DIM), D ** -0.5),
        'w_o': nrm(ks[14], (N_ATTN_LAYERS, N_HEADS * HEAD_DIM, D), (N_HEADS * HEAD_DIM) ** -0.5),
        'sinks': nrm(ks[15], (N_ATTN_LAYERS, N_HEADS), 0.5),
        'rel_table': nrm(ks[16], (N_BUCKETS, N_HEADS), 0.5),
        'w_gate': nrm(ks[17], (DEPTH, D, D_FF), D ** -0.5),
        'w_up': nrm(ks[18], (DEPTH, D, D_FF), D ** -0.5),
        'w_down': nrm(ks[19], (DEPTH, D_FF, D), D_FF ** -0.5),
    }


def reference(x_prompt, x_sample, state_conv, cache_k, cache_v, g_mix, g_ffn, g_final,
              w_conv_in, conv_w, w_conv_out, w_q, w_k, w_v, w_o, sinks, rel_table,
              w_gate, w_up, w_down):
    zero_prev = jnp.zeros((N_CONV_LAYERS, x_prompt.shape[0], CONV_STATE, D_MODEL), x_prompt.dtype)
    y_prompt, state_conv_prompt, cache_k_prompt, cache_v_prompt = trunk(
        x_prompt, zero_prev, None, None, g_mix, g_ffn, g_final, w_conv_in, conv_w, w_conv_out,
        w_q, w_k, w_v, w_o, sinks, rel_table, w_gate, w_up, w_down)
    y_sample, state_conv_sample, cache_k_sample, cache_v_sample = trunk(
        x_sample, state_conv, cache_k, cache_v, g_mix, g_ffn, g_final, w_conv_in, conv_w, w_conv_out,
        w_q, w_k, w_v, w_o, sinks, rel_table, w_gate, w_up, w_down)
    return (y_prompt, y_sample, state_conv_prompt, state_conv_sample,
            cache_k_prompt, cache_k_sample, cache_v_prompt, cache_v_sample)
```

```python
import functools
import math

import numpy as np
import jax
import jax.numpy as jnp
from jax import lax
from jax.experimental import pallas as pl
from jax.experimental.pallas import tpu as pltpu

D_MODEL = 1024
D_FF = 2816
HEAD_DIM = 64
N_HEADS = 16
N_KV_HEADS = 2
GROUP = N_HEADS // N_KV_HEADS
WINDOW = 128
N_BUCKETS = 32
MAX_DISTANCE = 128
CONV_STATE = 2
EPS = 1e-5
NEG_INF = -1e30
ATTN_SCALE = 1.0 / math.sqrt(HEAD_DIM)

KV_LANES = N_KV_HEADS * HEAD_DIM
QKV_COLS = D_MODEL + 2 * KV_LANES
SUBLANES = 8
KEY_TILE = 2 * WINDOW

TOKEN_TILE = 512
SAMPLE_BATCH_TILE = 16
VMEM_LIMIT_BYTES = 56 * 1024 * 1024

F32 = jnp.float32
BF16 = jnp.bfloat16


def _params(n_axes=1):
    return pltpu.CompilerParams(
        dimension_semantics=("arbitrary",) * n_axes,
        vmem_limit_bytes=VMEM_LIMIT_BYTES)


def _resident(shape):
    zeros = (0,) * len(shape)
    return pl.BlockSpec(shape, lambda *_: zeros, pipeline_mode=pl.Buffered(1))


def _smem():
    return pl.BlockSpec(memory_space=pltpu.SMEM)


def _rms(x, g):
    return x * lax.rsqrt(jnp.mean(x * x, axis=-1, keepdims=True) + EPS) * g


def _t5_bucket_np(dist):
    n = np.maximum(dist, 0)
    max_exact = N_BUCKETS // 2
    nf = np.maximum(n, 1).astype(np.float32)
    large = max_exact + (np.log(nf / np.float32(max_exact)) / np.float32(math.log(MAX_DISTANCE / max_exact))
                         * np.float32(N_BUCKETS - max_exact)).astype(np.int32)
    large = np.minimum(large, N_BUCKETS - 1)
    return np.where(n < max_exact, n, large).astype(np.int32)


def _ffn_kernel(x_ref, g_ref, wg_ref, wu_ref, wd_ref, gf_ref, o_ref, *, final_norm):
    x = x_ref[...]
    h = _rms(x, g_ref[...]).astype(BF16)
    gate = jnp.dot(h, wg_ref[...], preferred_element_type=F32)
    up = jnp.dot(h, wu_ref[...], preferred_element_type=F32)
    act = (gate * jax.nn.sigmoid(gate) * up).astype(BF16)
    y = x + jnp.dot(act, wd_ref[...], preferred_element_type=F32)
    if final_norm:
        y = _rms(y, gf_ref[...])
    o_ref[...] = y


def _ffn(x, g, wg, wu, wd, g_final, final_norm):
    n = x.shape[0]
    tile = pl.BlockSpec((TOKEN_TILE, D_MODEL), lambda i: (i, 0))
    return pl.pallas_call(
        functools.partial(_ffn_kernel, final_norm=final_norm),
        grid=(n // TOKEN_TILE,),
        in_specs=[tile, _resident((1, D_MODEL)), _resident((D_MODEL, D_FF)),
                  _resident((D_MODEL, D_FF)), _resident((D_FF, D_MODEL)),
                  _resident((1, D_MODEL))],
        out_specs=tile,
        out_shape=jax.ShapeDtypeStruct((n, D_MODEL), F32),
        compiler_params=_params(),
        name="ffn_final" if final_norm else "ffn",
    )(x, g, wg, wu, wd, g_final)


def _conv_gates(x, g, win):
    h = _rms(x, g).astype(BF16)
    bcx = jnp.dot(h, win, preferred_element_type=F32)
    b = bcx[:, :D_MODEL]
    u = bcx[:, D_MODEL:2 * D_MODEL] * bcx[:, 2 * D_MODEL:]
    return b, u


def _conv_prompt_kernel(x_ref, g_ref, win_ref, cw_ref, wout_ref, o_ref, tail_out_ref, tail_ref):
    tail = jnp.where(pl.program_id(0) > 0, tail_ref[...], 0.0)
    x = x_ref[...]
    b, u = _conv_gates(x, g_ref[...], win_ref[...])
    rows = u.shape[0]
    r8 = lax.broadcasted_iota(jnp.int32, (SUBLANES, D_MODEL), 0)
    u1 = pltpu.roll(u, 1, 0)
    u2 = pltpu.roll(u, 2, 0)
    u1 = jnp.concatenate([jnp.where(r8 < 1, pltpu.roll(tail, 1, 0), u1[:SUBLANES]), u1[SUBLANES:]], axis=0)
    u2 = jnp.concatenate([jnp.where(r8 < 2, pltpu.roll(tail, 2, 0), u2[:SUBLANES]), u2[SUBLANES:]], axis=0)
    cw = cw_ref[...]
    v = cw[0:1] * u2 + cw[1:2] * u1 + cw[2:3] * u
    o_ref[...] = x + jnp.dot((b * v).astype(BF16), wout_ref[...], preferred_element_type=F32)
    new_tail = u[rows - SUBLANES:]
    tail_ref[...] = new_tail
    tail_out_ref[...] = new_tail


def _conv_prompt(x, g, win, cw, wout):
    n = x.shape[0]
    tile = pl.BlockSpec((TOKEN_TILE, D_MODEL), lambda i: (i, 0))
    return pl.pallas_call(
        _conv_prompt_kernel,
        grid=(n // TOKEN_TILE,),
        in_specs=[tile, _resident((1, D_MODEL)), _resident((D_MODEL, 3 * D_MODEL)),
                  _resident((3, D_MODEL)), _resident((D_MODEL, D_MODEL))],
        out_specs=[tile, pl.BlockSpec((SUBLANES, D_MODEL), lambda i: (0, 0))],
        out_shape=[jax.ShapeDtypeStruct((n, D_MODEL), F32),
                   jax.ShapeDtypeStruct((SUBLANES, D_MODEL), F32)],
        scratch_shapes=[pltpu.VMEM((SUBLANES, D_MODEL), F32)],
        compiler_params=_params(),
        name="conv_prompt",
    )(x, g, win, cw, wout)


def _conv_sample_kernel(x_ref, prev_ref, g_ref, win_ref, cw_ref, wout_ref, o_ref, u_ref, *, steps):
    x = x_ref[...]
    b, u = _conv_gates(x, g_ref[...], win_ref[...])
    rows = u.shape[0]
    prev = prev_ref[...]
    t = lax.broadcasted_iota(jnp.int32, u.shape, 0) % steps
    u1 = jnp.where(t >= 1, pltpu.roll(u, 1, 0), pltpu.roll(prev, rows - 1, 0))
    u2 = jnp.where(t >= 2, pltpu.roll(u, 2, 0), prev)
    cw = cw_ref[...]
    v = cw[0:1] * u2 + cw[1:2] * u1 + cw[2:3] * u
    o_ref[...] = x + jnp.dot((b * v).astype(BF16), wout_ref[...], preferred_element_type=F32)
    u_ref[...] = u


def _conv_sample(x, prev, g, win, cw, wout, steps):
    n = x.shape[0]
    full = _resident((n, D_MODEL))
    return pl.pallas_call(
        functools.partial(_conv_sample_kernel, steps=steps),
        grid=(1,),
        in_specs=[full, full, _resident((1, D_MODEL)), _resident((D_MODEL, 3 * D_MODEL)),
                  _resident((3, D_MODEL)), _resident((D_MODEL, D_MODEL))],
        out_specs=[pl.BlockSpec((n, D_MODEL), lambda i: (0, 0))] * 2,
        out_shape=[jax.ShapeDtypeStruct((n, D_MODEL), F32)] * 2,
        compiler_params=_params(),
        name="conv_sample",
    )(x, prev, g, win, cw, wout)


def _lookup_bias(bucket, table_ref, head):
    acc = jnp.zeros(bucket.shape, F32)
    for b in range(N_BUCKETS):
        acc = jnp.where(bucket == b, table_ref[b, head], acc)
    return acc


def _tables_kernel(bucket_p_ref, bucket_s_ref, table_ref, sinks_ref, bias_p_ref, bias_s_ref, sink_s_ref, *, steps):
    bucket_p = bucket_p_ref[...]
    for head in range(N_HEADS):
        bias_p_ref[head] = _lookup_bias(bucket_p, table_ref, head)
    bucket_s = bucket_s_ref[...]
    grp = lax.broadcasted_iota(jnp.int32, bucket_s.shape, 0) // steps
    grp_l = lax.broadcasted_iota(jnp.int32, (bucket_s.shape[0], KV_LANES), 0) // steps
    for kv in range(N_KV_HEADS):
        bias = jnp.zeros(bucket_s.shape, F32)
        sink = jnp.zeros((bucket_s.shape[0], KV_LANES), F32)
        for gi in range(GROUP):
            head = kv * GROUP + gi
            bias = jnp.where(grp == gi, _lookup_bias(bucket_s, table_ref, head), bias)
            sink = jnp.where(grp_l == gi, sinks_ref[head], sink)
        bias_s_ref[kv] = bias
        sink_s_ref[kv] = sink


def _tables(bucket_p, bucket_s, table, sinks, steps):
    q_rows = bucket_s.shape[0]
    return pl.pallas_call(
        functools.partial(_tables_kernel, steps=steps),
        grid=(1,),
        in_specs=[_resident((WINDOW, KEY_TILE)), _resident((q_rows, KEY_TILE)), _smem(), _smem()],
        out_specs=[pl.BlockSpec((N_HEADS, WINDOW, KEY_TILE), lambda i: (0, 0, 0)),
                   pl.BlockSpec((N_KV_HEADS, q_rows, KEY_TILE), lambda i: (0, 0, 0)),
                   pl.BlockSpec((N_KV_HEADS, q_rows, KV_LANES), lambda i: (0, 0, 0))],
        out_shape=[jax.ShapeDtypeStruct((N_HEADS, WINDOW, KEY_TILE), F32),
                   jax.ShapeDtypeStruct((N_KV_HEADS, q_rows, KEY_TILE), F32),
                   jax.ShapeDtypeStruct((N_KV_HEADS, q_rows, KV_LANES), F32)],
        compiler_params=_params(),
        name="bias_tables",
    )(bucket_p, bucket_s, table, sinks)


def _attn_prompt_kernel(x_ref, g_ref, wqkv_ref, wo_ref, bias_ref, sinks_ref,
                        o_ref, klast_ref, vlast_ref, kbuf, vbuf, kcar, vcar, obuf):
    step = pl.program_id(0)
    rows = x_ref.shape[0]

    lane = lax.broadcasted_iota(jnp.int32, (rows, KV_LANES), 1)
    kv_lane = (lane < HEAD_DIM, lane >= HEAD_DIM)
    k_prev = [jnp.where(step > 0, kcar[kv], 0.0).astype(BF16) for kv in range(N_KV_HEADS)]
    v_prev = [jnp.where(step > 0, vcar[kv], 0.0).astype(BF16) for kv in range(N_KV_HEADS)]

    x = x_ref[...]
    h = _rms(x, g_ref[...]).astype(BF16)
    qkv = jnp.dot(h, wqkv_ref[...], preferred_element_type=F32)
    q = (qkv[:, :D_MODEL] * ATTN_SCALE).astype(BF16)
    k = qkv[:, D_MODEL:D_MODEL + KV_LANES]
    v = qkv[:, D_MODEL + KV_LANES:]
    klast_ref[...] = k[rows - WINDOW:]
    vlast_ref[...] = v[rows - WINDOW:]
    k_half = [jnp.where(kv_lane[kv], k, 0.0) for kv in range(N_KV_HEADS)]
    v_half = [jnp.where(kv_lane[kv], v, 0.0) for kv in range(N_KV_HEADS)]
    for kv in range(N_KV_HEADS):
        kbuf[kv] = k_half[kv].astype(BF16)
        vbuf[kv] = v_half[kv].astype(BF16)

    qi = lax.broadcasted_iota(jnp.int32, (WINDOW, KEY_TILE), 0)
    kj = lax.broadcasted_iota(jnp.int32, (WINDOW, KEY_TILE), 1)
    band = (kj >= qi) & (kj <= qi + WINDOW)
    lane_q = lax.broadcasted_iota(jnp.int32, (WINDOW, KV_LANES), 1)
    lane_k = lax.broadcasted_iota(jnp.int32, (KEY_TILE, KV_LANES), 1)
    ones_half = [(lane_k < HEAD_DIM).astype(F32).astype(BF16), (lane_k >= HEAD_DIM).astype(F32).astype(BF16)]

    for blk in range(rows // WINDOW):
        r0 = blk * WINDOW
        qs = jnp.concatenate(
            [q[r0:r0 + WINDOW, gi * KV_LANES:(gi + 1) * KV_LANES] for gi in range(GROUP)], axis=0)
        if blk == 0:
            mask = band & (kj >= jnp.where(step > 0, 0, WINDOW))
            keys = [jnp.concatenate([k_prev[kv], kbuf[kv, 0:WINDOW, :]], axis=0) for kv in range(N_KV_HEADS)]
            vals = [jnp.concatenate([v_prev[kv], vbuf[kv, 0:WINDOW, :]], axis=0) for kv in range(N_KV_HEADS)]
        else:
            mask = band
            keys = [kbuf[kv, r0 - WINDOW:r0 + WINDOW, :] for kv in range(N_KV_HEADS)]
            vals = [vbuf[kv, r0 - WINDOW:r0 + WINDOW, :] for kv in range(N_KV_HEADS)]
        probs, maxes = [], []
        for kv in range(N_KV_HEADS):
            s_all = lax.dot_general(qs, keys[kv], (((1,), (1,)), ((), ())), preferred_element_type=F32)
            p_kv, m_kv = [], []
            for gi in range(GROUP):
                head = kv * GROUP + gi
                s = s_all[gi * WINDOW:(gi + 1) * WINDOW]
                s = jnp.where(mask, s + bias_ref[head], NEG_INF)
                m = jnp.maximum(jnp.max(s, axis=-1, keepdims=True), sinks_ref[head])
                p_kv.append(jnp.exp(s - m).astype(BF16))
                m_kv.append(m)
            probs.append(jnp.concatenate(p_kv, axis=0))
            maxes.append(m_kv)
        acc = None
        for kv in range(N_KV_HEADS):
            rhs = jnp.concatenate([vals[kv], ones_half[kv]], axis=1)
            part = jnp.dot(probs[kv], rhs, preferred_element_type=F32)
            acc = part if acc is None else acc + part
        outs = []
        for gi in range(GROUP):
            a = acc[gi * WINDOW:(gi + 1) * WINDOW]
            sink_term = jnp.where(lane_q < HEAD_DIM,
                                  jnp.exp(sinks_ref[gi] - maxes[0][gi]),
                                  jnp.exp(sinks_ref[GROUP + gi] - maxes[1][gi]))
            outs.append(a[:, :KV_LANES] / (a[:, KV_LANES:] + sink_term))
        obuf[r0:r0 + WINDOW, :] = jnp.concatenate(outs, axis=1).astype(BF16)

    o_ref[...] = x + jnp.dot(obuf[...], wo_ref[...], preferred_element_type=F32)
    for kv in range(N_KV_HEADS):
        kcar[kv] = k_half[kv][rows - WINDOW:]
        vcar[kv] = v_half[kv][rows - WINDOW:]


def _attn_prompt(x, g, wqkv, wo, bias, sinks):
    n = x.shape[0]
    tile = pl.BlockSpec((TOKEN_TILE, D_MODEL), lambda i: (i, 0))
    last = pl.BlockSpec((WINDOW, KV_LANES), lambda i: (0, 0))
    return pl.pallas_call(
        _attn_prompt_kernel,
        grid=(n // TOKEN_TILE,),
        in_specs=[tile, _resident((1, D_MODEL)), _resident((D_MODEL, QKV_COLS)),
                  _resident((D_MODEL, D_MODEL)), _resident((N_HEADS, WINDOW, KEY_TILE)), _smem()],
        out_specs=[tile, last, last],
        out_shape=[jax.ShapeDtypeStruct((n, D_MODEL), F32),
                   jax.ShapeDtypeStruct((WINDOW, KV_LANES), F32),
                   jax.ShapeDtypeStruct((WINDOW, KV_LANES), F32)],
        scratch_shapes=[pltpu.VMEM((N_KV_HEADS, TOKEN_TILE, KV_LANES), BF16),
                        pltpu.VMEM((N_KV_HEADS, TOKEN_TILE, KV_LANES), BF16),
                        pltpu.VMEM((N_KV_HEADS, WINDOW, KV_LANES), F32),
                        pltpu.VMEM((N_KV_HEADS, WINDOW, KV_LANES), F32),
                        pltpu.VMEM((TOKEN_TILE, D_MODEL), BF16)],
        compiler_params=_params(),
        name="attn_prompt",
    )(x, g, wqkv, wo, bias, sinks)


def _proj_kernel(x_ref, g_ref, w_ref, o_ref):
    h = _rms(x_ref[...], g_ref[...]).astype(BF16)
    o_ref[...] = jnp.dot(h, w_ref[...], preferred_element_type=F32)


def _proj(x, g, w):
    n, cols = x.shape[0], w.shape[1]
    return pl.pallas_call(
        _proj_kernel,
        grid=(1,),
        in_specs=[_resident((n, D_MODEL)), _resident((1, D_MODEL)), _resident((D_MODEL, cols))],
        out_specs=pl.BlockSpec((n, cols), lambda i: (0, 0)),
        out_shape=jax.ShapeDtypeStruct((n, cols), F32),
        compiler_params=_params(),
        name="proj_sample",
    )(x, g, w)


def _oproj_kernel(x_ref, a_ref, w_ref, o_ref):
    o_ref[...] = x_ref[...] + jnp.dot(a_ref[...].astype(BF16), w_ref[...], preferred_element_type=F32)


def _oproj(x, a, w):
    n = x.shape[0]
    return pl.pallas_call(
        _oproj_kernel,
        grid=(1,),
        in_specs=[_resident((n, D_MODEL)), _resident((n, D_MODEL)), _resident((D_MODEL, D_MODEL))],
        out_specs=pl.BlockSpec((n, D_MODEL), lambda i: (0, 0)),
        out_shape=jax.ShapeDtypeStruct((n, D_MODEL), F32),
        compiler_params=_params(),
        name="oproj_sample",
    )(x, a, w)


def _attn_sample_kernel(q_ref, kc_ref, vc_ref, kn_ref, vn_ref, bias_ref, sink_ref,
                        o_ref, kout_ref, vout_ref, *, steps):
    q_rows = GROUP * steps

    qi = lax.broadcasted_iota(jnp.int32, (q_rows, KEY_TILE), 0) % steps
    kj = lax.broadcasted_iota(jnp.int32, (q_rows, KEY_TILE), 1)
    mask = (kj >= qi) & (kj <= qi + WINDOW)
    lane = lax.broadcasted_iota(jnp.int32, (KEY_TILE, KV_LANES), 1)
    lane_q = lax.broadcasted_iota(jnp.int32, (q_rows, KV_LANES), 1)
    kv_lane = (lane < HEAD_DIM, lane >= HEAD_DIM)
    pad = jnp.zeros((KEY_TILE - WINDOW - SUBLANES, KV_LANES), F32)
    r8 = lax.broadcasted_iota(jnp.int32, (SUBLANES, KV_LANES), 0)

    def shifted_cache(cache, new):
        body = pltpu.roll(cache, WINDOW - steps, 0)
        tail = jnp.where(r8 >= SUBLANES - steps, pltpu.roll(new, SUBLANES - steps, 0),
                         body[WINDOW - SUBLANES:])
        return jnp.concatenate([body[:WINDOW - SUBLANES], tail], axis=0)

    def one_batch(n, carry):
        kc, vc, kn, vn = kc_ref[n], vc_ref[n], kn_ref[n], vn_ref[n]
        kout_ref[n] = shifted_cache(kc, kn)
        vout_ref[n] = shifted_cache(vc, vn)
        keys = jnp.concatenate([kc, kn, pad], axis=0)
        vals = jnp.concatenate([vc, vn, pad], axis=0)
        q = (q_ref[n] * ATTN_SCALE).astype(BF16)
        acc = None
        sink_terms = []
        for kv in range(N_KV_HEADS):
            s = lax.dot_general(q, jnp.where(kv_lane[kv], keys, 0.0).astype(BF16),
                                (((1,), (1,)), ((), ())), preferred_element_type=F32)
            s = jnp.where(mask, s + bias_ref[kv], NEG_INF)
            sink = sink_ref[kv][:, 0:1]
            m = jnp.maximum(jnp.max(s, axis=-1, keepdims=True), sink)
            p = jnp.exp(s - m).astype(BF16)
            sink_terms.append(jnp.exp(sink - m))
            rhs = jnp.concatenate([jnp.where(kv_lane[kv], vals, 0.0), jnp.where(kv_lane[kv], 1.0, 0.0)], axis=1)
            part = jnp.dot(p, rhs.astype(BF16), preferred_element_type=F32)
            acc = part if acc is None else acc + part
        denom = acc[:, KV_LANES:] + jnp.where(lane_q < HEAD_DIM, sink_terms[0], sink_terms[1])
        o_ref[n] = acc[:, :KV_LANES] / denom
        return carry

    lax.fori_loop(0, q_ref.shape[0], one_batch, 0)


def _attn_sample(q, kc, vc, kn, vn, bias, sink, steps):
    nb = q.shape[0]
    q_rows = GROUP * steps
    bt = SAMPLE_BATCH_TILE

    def batch_spec(r):
        return pl.BlockSpec((bt, r, KV_LANES), lambda i: (i, 0, 0))

    return pl.pallas_call(
        functools.partial(_attn_sample_kernel, steps=steps),
        grid=(nb // bt,),
        in_specs=[batch_spec(q_rows), batch_spec(WINDOW), batch_spec(WINDOW),
                  batch_spec(SUBLANES), batch_spec(SUBLANES),
                  _resident((N_KV_HEADS, q_rows, KEY_TILE)), _resident((N_KV_HEADS, q_rows, KV_LANES))],
        out_specs=[batch_spec(q_rows), batch_spec(WINDOW), batch_spec(WINDOW)],
        out_shape=[jax.ShapeDtypeStruct((nb, q_rows, KV_LANES), F32),
                   jax.ShapeDtypeStruct((nb, WINDOW, KV_LANES), F32),
                   jax.ShapeDtypeStruct((nb, WINDOW, KV_LANES), F32)],
        compiler_params=_params(),
        name="attn_sample",
    )(q, kc, vc, kn, vn, bias, sink)


def kernel(x_prompt, x_sample, state_conv, cache_k, cache_v, g_mix, g_ffn, g_final, w_conv_in, conv_w,
           w_conv_out, w_q, w_k, w_v, w_o, sinks, rel_table, w_gate, w_up, w_down):
    batch, seq, _ = x_prompt.shape
    dec_batch, dec_seq, _ = x_sample.shape
    assert batch == 1 and seq % TOKEN_TILE == 0 and TOKEN_TILE % WINDOW == 0
    assert dec_batch % SAMPLE_BATCH_TILE == 0 and dec_seq <= SUBLANES
    assert (dec_batch * dec_seq) % TOKEN_TILE == 0
    assert g_mix.shape[0] == 2, "layer 0 is the conv mixer, layer 1 the attention mixer"

    win = w_conv_in[0].astype(BF16)
    wout = w_conv_out[0].astype(BF16)
    wq = w_q[0].reshape(D_MODEL, N_KV_HEADS, GROUP, HEAD_DIM).transpose(0, 2, 1, 3).reshape(D_MODEL, D_MODEL)
    wqkv = jnp.concatenate([wq, w_k[0], w_v[0]], axis=1).astype(BF16)
    wo = w_o[0].reshape(N_KV_HEADS, GROUP, HEAD_DIM, D_MODEL).transpose(1, 0, 2, 3).reshape(D_MODEL, D_MODEL)
    wo = wo.astype(BF16)
    wg, wu, wd = w_gate.astype(BF16), w_up.astype(BF16), w_down.astype(BF16)
    gm = g_mix.reshape(2, 1, D_MODEL)
    gf = g_ffn.reshape(2, 1, D_MODEL)
    gfin = g_final.reshape(1, D_MODEL)
    cw = conv_w[0]
    sink_vec = sinks[0]

    bucket_p = jnp.asarray(_t5_bucket_np(
        (np.arange(WINDOW)[:, None] + WINDOW) - np.arange(KEY_TILE)[None, :]))
    dist_s = np.arange(dec_seq)[:, None] + WINDOW - np.arange(KEY_TILE)[None, :]
    bucket_s = jnp.asarray(np.tile(_t5_bucket_np(dist_s), (GROUP, 1)))
    bias_p, bias_s, sink_s = _tables(bucket_p, bucket_s, rel_table, sink_vec, dec_seq)

    xp = x_prompt.reshape(seq, D_MODEL)
    xp, tail_p = _conv_prompt(xp, gm[0], win, cw, wout)
    xp = _ffn(xp, gf[0], wg[0], wu[0], wd[0], gfin, False)
    xp, k_last, v_last = _attn_prompt(xp, gm[1], wqkv, wo, bias_p, sink_vec)
    y_prompt = _ffn(xp, gf[1], wg[1], wu[1], wd[1], gfin, True).reshape(batch, seq, D_MODEL)
    state_conv_prompt = tail_p[SUBLANES - CONV_STATE:].reshape(1, batch, CONV_STATE, D_MODEL)
    cache_k_prompt = k_last.reshape(1, batch, WINDOW, N_KV_HEADS, HEAD_DIM)
    cache_v_prompt = v_last.reshape(1, batch, WINDOW, N_KV_HEADS, HEAD_DIM)

    n_s = dec_batch * dec_seq
    xs = x_sample.reshape(n_s, D_MODEL)
    prev = jnp.pad(state_conv[0], ((0, 0), (0, dec_seq - CONV_STATE), (0, 0))).reshape(n_s, D_MODEL)
    xs, u_s = _conv_sample(xs, prev, gm[0], win, cw, wout, dec_seq)
    state_conv_sample = u_s.reshape(dec_batch, dec_seq, D_MODEL)[:, dec_seq - CONV_STATE:][None]
    xs = _ffn(xs, gf[0], wg[0], wu[0], wd[0], gfin, False)
    qkv = _proj(xs, gm[1], wqkv)
    q_s = qkv[:, :D_MODEL].reshape(dec_batch, dec_seq, GROUP, KV_LANES).transpose(0, 2, 1, 3)
    q_s = q_s.reshape(dec_batch, GROUP * dec_seq, KV_LANES)
    pad_new = ((0, 0), (0, SUBLANES - dec_seq), (0, 0))
    k_new = jnp.pad(qkv[:, D_MODEL:D_MODEL + KV_LANES].reshape(dec_batch, dec_seq, KV_LANES), pad_new)
    v_new = jnp.pad(qkv[:, D_MODEL + KV_LANES:].reshape(dec_batch, dec_seq, KV_LANES), pad_new)
    kc = cache_k[0].reshape(dec_batch, WINDOW, KV_LANES)
    vc = cache_v[0].reshape(dec_batch, WINDOW, KV_LANES)
    o_s, k_out, v_out = _attn_sample(q_s, kc, vc, k_new, v_new, bias_s, sink_s, dec_seq)
    o_s = o_s.reshape(dec_batch, GROUP, dec_seq, KV_LANES).transpose(0, 2, 1, 3).reshape(n_s, D_MODEL)
    xs = _oproj(xs, o_s, wo)
    y_sample = _ffn(xs, gf[1], wg[1], wu[1], wd[1], gfin, True).reshape(dec_batch, dec_seq, D_MODEL)
    cache_k_sample = k_out.reshape(1, dec_batch, WINDOW, N_KV_HEADS, HEAD_DIM)
    cache_v_sample = v_out.reshape(1, dec_batch, WINDOW, N_KV_HEADS, HEAD_DIM)

    return (y_prompt, y_sample, state_conv_prompt, state_conv_sample,
            cache_k_prompt, cache_k_sample, cache_v_prompt, cache_v_sample)
```

```python
import functools
import math

import numpy as np
import jax
import jax.numpy as jnp
from jax import lax
from jax.experimental import pallas as pl
from jax.experimental.pallas import tpu as pltpu

D_MODEL = 1024
D_FF = 2816
HEAD_DIM = 64
N_HEADS = 16
N_KV_HEADS = 2
GROUP = N_HEADS // N_KV_HEADS
WINDOW = 128
N_BUCKETS = 32
MAX_DISTANCE = 128
CONV_STATE = 2
EPS = 1e-5
NEG_INF = -1e30
ATTN_SCALE = 1.0 / math.sqrt(HEAD_DIM)

KV_LANES = N_KV_HEADS * HEAD_DIM
QKV_COLS = D_MODEL + 2 * KV_LANES
SUBLANES = 8
KEY_TILE = 2 * WINDOW

TOKEN_TILE = 512
SAMPLE_BATCH_TILE = 16
VMEM_LIMIT_BYTES = 56 * 1024 * 1024

F32 = jnp.float32
BF16 = jnp.bfloat16


def _params(n_axes=1):
    return pltpu.CompilerParams(
        dimension_semantics=("arbitrary",) * n_axes,
        vmem_limit_bytes=VMEM_LIMIT_BYTES)


def _resident(shape):
    zeros = (0,) * len(shape)
    return pl.BlockSpec(shape, lambda *_: zeros, pipeline_mode=pl.Buffered(1))


def _layer(shape, layer):
    index = (layer,) + (0,) * len(shape)
    return pl.BlockSpec((None,) + tuple(shape), lambda *_: index, pipeline_mode=pl.Buffered(1))


def _smem():
    return pl.BlockSpec(memory_space=pltpu.SMEM)


def _rms(x, g):
    return x * lax.rsqrt(jnp.mean(x * x, axis=-1, keepdims=True) + EPS) * g


def _t5_bucket_np(dist):
    n = np.maximum(dist, 0)
    max_exact = N_BUCKETS // 2
    nf = np.maximum(n, 1).astype(np.float32)
    large = max_exact + (np.log(nf / np.float32(max_exact)) / np.float32(math.log(MAX_DISTANCE / max_exact))
                         * np.float32(N_BUCKETS - max_exact)).astype(np.int32)
    large = np.minimum(large, N_BUCKETS - 1)
    return np.where(n < max_exact, n, large).astype(np.int32)


def _ffn_kernel(x_ref, g_ref, wg_ref, wu_ref, wd_ref, gf_ref, o_ref, *, final_norm):
    x = x_ref[...]
    h = _rms(x, g_ref[...]).astype(BF16)
    gate = jnp.dot(h, wg_ref[...], preferred_element_type=F32)
    up = jnp.dot(h, wu_ref[...], preferred_element_type=F32)
    act = (gate * jax.nn.sigmoid(gate) * up).astype(BF16)
    y = x + jnp.dot(act, wd_ref[...], preferred_element_type=F32)
    if final_norm:
        y = _rms(y, gf_ref[...])
    o_ref[...] = y


def _ffn(x, g, wg, wu, wd, g_final, layer, final_norm):
    n = x.shape[0]
    tile = pl.BlockSpec((TOKEN_TILE, D_MODEL), lambda i: (i, 0))
    return pl.pallas_call(
        functools.partial(_ffn_kernel, final_norm=final_norm),
        grid=(n // TOKEN_TILE,),
        in_specs=[tile, _layer((1, D_MODEL), layer), _layer((D_MODEL, D_FF), layer),
                  _layer((D_MODEL, D_FF), layer), _layer((D_FF, D_MODEL), layer),
                  _resident((1, D_MODEL))],
        out_specs=tile,
        out_shape=jax.ShapeDtypeStruct((n, D_MODEL), F32),
        compiler_params=_params(),
        name="ffn_final" if final_norm else "ffn",
    )(x, g, wg, wu, wd, g_final)


def _conv_gates(x, g, win):
    h = _rms(x, g).astype(BF16)
    bcx = jnp.dot(h, win, preferred_element_type=F32)
    b = bcx[:, :D_MODEL]
    u = bcx[:, D_MODEL:2 * D_MODEL] * bcx[:, 2 * D_MODEL:]
    return b, u


def _conv_prompt_kernel(x_ref, g_ref, win_ref, cw_ref, wout_ref, o_ref, tail_out_ref, tail_ref):
    tail = jnp.where(pl.program_id(0) > 0, tail_ref[...], 0.0)
    x = x_ref[...]
    b, u = _conv_gates(x, g_ref[...], win_ref[...])
    rows = u.shape[0]
    r8 = lax.broadcasted_iota(jnp.int32, (SUBLANES, D_MODEL), 0)
    u1 = pltpu.roll(u, 1, 0)
    u2 = pltpu.roll(u, 2, 0)
    u1 = jnp.concatenate([jnp.where(r8 < 1, pltpu.roll(tail, 1, 0), u1[:SUBLANES]), u1[SUBLANES:]], axis=0)
    u2 = jnp.concatenate([jnp.where(r8 < 2, pltpu.roll(tail, 2, 0), u2[:SUBLANES]), u2[SUBLANES:]], axis=0)
    cw = cw_ref[...]
    v = cw[0:1] * u2 + cw[1:2] * u1 + cw[2:3] * u
    o_ref[...] = x + jnp.dot((b * v).astype(BF16), wout_ref[...], preferred_element_type=F32)
    new_tail = u[rows - SUBLANES:]
    tail_ref[...] = new_tail
    tail_out_ref[...] = new_tail


def _conv_prompt(x, g, win, cw, wout):
    n = x.shape[0]
    tile = pl.BlockSpec((TOKEN_TILE, D_MODEL), lambda i: (i, 0))
    return pl.pallas_call(
        _conv_prompt_kernel,
        grid=(n // TOKEN_TILE,),
        in_specs=[tile, _resident((1, D_MODEL)), _resident((D_MODEL, 3 * D_MODEL)),
                  _resident((3, D_MODEL)), _resident((D_MODEL, D_MODEL))],
        out_specs=[tile, pl.BlockSpec((SUBLANES, D_MODEL), lambda i: (0, 0))],
        out_shape=[jax.ShapeDtypeStruct((n, D_MODEL), F32),
                   jax.ShapeDtypeStruct((SUBLANES, D_MODEL), F32)],
        scratch_shapes=[pltpu.VMEM((SUBLANES, D_MODEL), F32)],
        compiler_params=_params(),
        name="conv_prompt",
    )(x, g, win, cw, wout)


def _conv_sample_kernel(x_ref, prev_ref, g_ref, win_ref, cw_ref, wout_ref, o_ref, u_ref, *, steps):
    x = x_ref[...]
    b, u = _conv_gates(x, g_ref[...], win_ref[...])
    rows = u.shape[0]
    prev = prev_ref[...]
    t = lax.broadcasted_iota(jnp.int32, u.shape, 0) % steps
    u1 = jnp.where(t >= 1, pltpu.roll(u, 1, 0), pltpu.roll(prev, rows - 1, 0))
    u2 = jnp.where(t >= 2, pltpu.roll(u, 2, 0), prev)
    cw = cw_ref[...]
    v = cw[0:1] * u2 + cw[1:2] * u1 + cw[2:3] * u
    o_ref[...] = x + jnp.dot((b * v).astype(BF16), wout_ref[...], preferred_element_type=F32)
    u_ref[...] = u


def _conv_sample(x, prev, g, win, cw, wout, steps):
    n = x.shape[0]
    full = _resident((n, D_MODEL))
    return pl.pallas_call(
        functools.partial(_conv_sample_kernel, steps=steps),
        grid=(1,),
        in_specs=[full, full, _resident((1, D_MODEL)), _resident((D_MODEL, 3 * D_MODEL)),
                  _resident((3, D_MODEL)), _resident((D_MODEL, D_MODEL))],
        out_specs=[pl.BlockSpec((n, D_MODEL), lambda i: (0, 0))] * 2,
        out_shape=[jax.ShapeDtypeStruct((n, D_MODEL), F32)] * 2,
        compiler_params=_params(),
        name="conv_sample",
    )(x, prev, g, win, cw, wout)


def _lookup_bias(bucket, table_ref, head):
    acc = jnp.zeros(bucket.shape, F32)
    for b in range(N_BUCKETS):
        acc = jnp.where(bucket == b, table_ref[b, head], acc)
    return acc


def _tables_kernel(bucket_p_ref, bucket_s_ref, table_ref, sinks_ref, bias_p_ref, bias_s_ref, sink_s_ref, *, steps):
    bucket_p = bucket_p_ref[...]
    for head in range(N_HEADS):
        bias_p_ref[head] = _lookup_bias(bucket_p, table_ref, head)
    bucket_s = bucket_s_ref[...]
    grp = lax.broadcasted_iota(jnp.int32, bucket_s.shape, 0) // steps
    grp_l = lax.broadcasted_iota(jnp.int32, (bucket_s.shape[0], KV_LANES), 0) // steps
    for kv in range(N_KV_HEADS):
        bias = jnp.zeros(bucket_s.shape, F32)
        sink = jnp.zeros((bucket_s.shape[0], KV_LANES), F32)
        for gi in range(GROUP):
            head = kv * GROUP + gi
            bias = jnp.where(grp == gi, _lookup_bias(bucket_s, table_ref, head), bias)
            sink = jnp.where(grp_l == gi, sinks_ref[head], sink)
        bias_s_ref[kv] = bias
        sink_s_ref[kv] = sink


def _tables(bucket_p, bucket_s, table, sinks, steps):
    q_rows = bucket_s.shape[0]
    return pl.pallas_call(
        functools.partial(_tables_kernel, steps=steps),
        grid=(1,),
        in_specs=[_resident((WINDOW, KEY_TILE)), _resident((q_rows, KEY_TILE)), _smem(), _smem()],
        out_specs=[pl.BlockSpec((N_HEADS, WINDOW, KEY_TILE), lambda i: (0, 0, 0)),
                   pl.BlockSpec((N_KV_HEADS, q_rows, KEY_TILE), lambda i: (0, 0, 0)),
                   pl.BlockSpec((N_KV_HEADS, q_rows, KV_LANES), lambda i: (0, 0, 0))],
        out_shape=[jax.ShapeDtypeStruct((N_HEADS, WINDOW, KEY_TILE), F32),
                   jax.ShapeDtypeStruct((N_KV_HEADS, q_rows, KEY_TILE), F32),
                   jax.ShapeDtypeStruct((N_KV_HEADS, q_rows, KV_LANES), F32)],
        compiler_params=_params(),
        name="bias_tables",
    )(bucket_p, bucket_s, table, sinks)


def _attn_prompt_kernel(x_ref, g_ref, wqkv_ref, wo_ref, bias_ref, sinks_ref,
                        o_ref, klast_ref, vlast_ref, kbuf, vbuf, kcar, vcar, obuf):
    step = pl.program_id(0)
    rows = x_ref.shape[0]

    lane = lax.broadcasted_iota(jnp.int32, (rows, KV_LANES), 1)
    kv_lane = (lane < HEAD_DIM, lane >= HEAD_DIM)
    k_prev = [jnp.where(step > 0, kcar[kv], 0.0).astype(BF16) for kv in range(N_KV_HEADS)]
    v_prev = [jnp.where(step > 0, vcar[kv], 0.0).astype(BF16) for kv in range(N_KV_HEADS)]

    x = x_ref[...]
    h = _rms(x, g_ref[...]).astype(BF16)
    qkv = jnp.dot(h, wqkv_ref[...], preferred_element_type=F32)
    q = (qkv[:, :D_MODEL] * ATTN_SCALE).astype(BF16)
    k = qkv[:, D_MODEL:D_MODEL + KV_LANES]
    v = qkv[:, D_MODEL + KV_LANES:]
    klast_ref[...] = k[rows - WINDOW:]
    vlast_ref[...] = v[rows - WINDOW:]
    k_half = [jnp.where(kv_lane[kv], k, 0.0) for kv in range(N_KV_HEADS)]
    v_half = [jnp.where(kv_lane[kv], v, 0.0) for kv in range(N_KV_HEADS)]
    for kv in range(N_KV_HEADS):
        kbuf[kv] = k_half[kv].astype(BF16)
        vbuf[kv] = v_half[kv].astype(BF16)

    qi = lax.broadcasted_iota(jnp.int32, (WINDOW, KEY_TILE), 0)
    kj = lax.broadcasted_iota(jnp.int32, (WINDOW, KEY_TILE), 1)
    band = (kj >= qi) & (kj <= qi + WINDOW)
    lane_q = lax.broadcasted_iota(jnp.int32, (WINDOW, KV_LANES), 1)
    lane_k = lax.broadcasted_iota(jnp.int32, (KEY_TILE, KV_LANES), 1)
    ones_half = [(lane_k < HEAD_DIM).astype(F32).astype(BF16), (lane_k >= HEAD_DIM).astype(F32).astype(BF16)]

    for blk in range(rows // WINDOW):
        r0 = blk * WINDOW
        qs = jnp.concatenate(
            [q[r0:r0 + WINDOW, gi * KV_LANES:(gi + 1) * KV_LANES] for gi in range(GROUP)], axis=0)
        if blk == 0:
            mask = band & (kj >= jnp.where(step > 0, 0, WINDOW))
            keys = [jnp.concatenate([k_prev[kv], kbuf[kv, 0:WINDOW, :]], axis=0) for kv in range(N_KV_HEADS)]
            vals = [jnp.concatenate([v_prev[kv], vbuf[kv, 0:WINDOW, :]], axis=0) for kv in range(N_KV_HEADS)]
        else:
            mask = band
            keys = [kbuf[kv, r0 - WINDOW:r0 + WINDOW, :] for kv in range(N_KV_HEADS)]
            vals = [vbuf[kv, r0 - WINDOW:r0 + WINDOW, :] for kv in range(N_KV_HEADS)]
        probs, maxes = [], []
        for kv in range(N_KV_HEADS):
            s_all = lax.dot_general(qs, keys[kv], (((1,), (1,)), ((), ())), preferred_element_type=F32)
            p_kv, m_kv = [], []
            for gi in range(GROUP):
                head = kv * GROUP + gi
                s = s_all[gi * WINDOW:(gi + 1) * WINDOW]
                s = jnp.where(mask, s + bias_ref[head], NEG_INF)
                m = jnp.maximum(jnp.max(s, axis=-1, keepdims=True), sinks_ref[head])
                p_kv.append(jnp.exp(s - m).astype(BF16))
                m_kv.append(m)
            probs.append(jnp.concatenate(p_kv, axis=0))
            maxes.append(m_kv)
        acc = None
        for kv in range(N_KV_HEADS):
            rhs = jnp.concatenate([vals[kv], ones_half[kv]], axis=1)
            part = jnp.dot(probs[kv], rhs, preferred_element_type=F32)
            acc = part if acc is None else acc + part
        outs = []
        for gi in range(GROUP):
            a = acc[gi * WINDOW:(gi + 1) * WINDOW]
            sink_term = jnp.where(lane_q < HEAD_DIM,
                                  jnp.exp(sinks_ref[gi] - maxes[0][gi]),
                                  jnp.exp(sinks_ref[GROUP + gi] - maxes[1][gi]))
            outs.append(a[:, :KV_LANES] / (a[:, KV_LANES:] + sink_term))
        obuf[r0:r0 + WINDOW, :] = jnp.concatenate(outs, axis=1).astype(BF16)

    o_ref[...] = x + jnp.dot(obuf[...], wo_ref[...], preferred_element_type=F32)
    for kv in range(N_KV_HEADS):
        kcar[kv] = k_half[kv][rows - WINDOW:]
        vcar[kv] = v_half[kv][rows - WINDOW:]


def _attn_prompt(x, g, wqkv, wo, bias, sinks):
    n = x.shape[0]
    tile = pl.BlockSpec((TOKEN_TILE, D_MODEL), lambda i: (i, 0))
    last = pl.BlockSpec((WINDOW, KV_LANES), lambda i: (0, 0))
    return pl.pallas_call(
        _attn_prompt_kernel,
        grid=(n // TOKEN_TILE,),
        in_specs=[tile, _resident((1, D_MODEL)), _resident((D_MODEL, QKV_COLS)),
                  _resident((D_MODEL, D_MODEL)), _resident((N_HEADS, WINDOW, KEY_TILE)), _smem()],
        out_specs=[tile, last, last],
        out_shape=[jax.ShapeDtypeStruct((n, D_MODEL), F32),
                   jax.ShapeDtypeStruct((WINDOW, KV_LANES), F32),
                   jax.ShapeDtypeStruct((WINDOW, KV_LANES), F32)],
        scratch_shapes=[pltpu.VMEM((N_KV_HEADS, TOKEN_TILE, KV_LANES), BF16),
                        pltpu.VMEM((N_KV_HEADS, TOKEN_TILE, KV_LANES), BF16),
                        pltpu.VMEM((N_KV_HEADS, WINDOW, KV_LANES), F32),
                        pltpu.VMEM((N_KV_HEADS, WINDOW, KV_LANES), F32),
                        pltpu.VMEM((TOKEN_TILE, D_MODEL), BF16)],
        compiler_params=_params(),
        name="attn_prompt",
    )(x, g, wqkv, wo, bias, sinks)


def _proj_kernel(x_ref, g_ref, w_ref, o_ref):
    h = _rms(x_ref[...], g_ref[...]).astype(BF16)
    o_ref[...] = jnp.dot(h, w_ref[...], preferred_element_type=F32)


def _proj(x, g, w):
    n, cols = x.shape[0], w.shape[1]
    return pl.pallas_call(
        _proj_kernel,
        grid=(1,),
        in_specs=[_resident((n, D_MODEL)), _resident((1, D_MODEL)), _resident((D_MODEL, cols))],
        out_specs=pl.BlockSpec((n, cols), lambda i: (0, 0)),
        out_shape=jax.ShapeDtypeStruct((n, cols), F32),
        compiler_params=_params(),
        name="proj_sample",
    )(x, g, w)


def _oproj_kernel(x_ref, a_ref, w_ref, o_ref):
    o_ref[...] = x_ref[...] + jnp.dot(a_ref[...].astype(BF16), w_ref[...], preferred_element_type=F32)


def _oproj(x, a, w):
    n = x.shape[0]
    return pl.pallas_call(
        _oproj_kernel,
        grid=(1,),
        in_specs=[_resident((n, D_MODEL)), _resident((n, D_MODEL)), _resident((D_MODEL, D_MODEL))],
        out_specs=pl.BlockSpec((n, D_MODEL), lambda i: (0, 0)),
        out_shape=jax.ShapeDtypeStruct((n, D_MODEL), F32),
        compiler_params=_params(),
        name="oproj_sample",
    )(x, a, w)


def _attn_sample_kernel(q_ref, kc_ref, vc_ref, kn_ref, vn_ref, bias_ref, sink_ref,
                        o_ref, kout_ref, vout_ref, *, steps):
    q_rows = GROUP * steps

    qi = lax.broadcasted_iota(jnp.int32, (q_rows, KEY_TILE), 0) % steps
    kj = lax.broadcasted_iota(jnp.int32, (q_rows, KEY_TILE), 1)
    mask = (kj >= qi) & (kj <= qi + WINDOW)
    lane = lax.broadcasted_iota(jnp.int32, (KEY_TILE, KV_LANES), 1)
    lane_q = lax.broadcasted_iota(jnp.int32, (q_rows, KV_LANES), 1)
    kv_lane = (lane < HEAD_DIM, lane >= HEAD_DIM)
    pad = jnp.zeros((KEY_TILE - WINDOW - SUBLANES, KV_LANES), F32)
    r8 = lax.broadcasted_iota(jnp.int32, (SUBLANES, KV_LANES), 0)

    def shifted_cache(cache, new):
        body = pltpu.roll(cache, WINDOW - steps, 0)
        tail = jnp.where(r8 >= SUBLANES - steps, pltpu.roll(new, SUBLANES - steps, 0),
                         body[WINDOW - SUBLANES:])
        return jnp.concatenate([body[:WINDOW - SUBLANES], tail], axis=0)

    def one_batch(n, carry):
        kc, vc, kn, vn = kc_ref[n], vc_ref[n], kn_ref[n], vn_ref[n]
        kout_ref[n] = shifted_cache(kc, kn)
        vout_ref[n] = shifted_cache(vc, vn)
        keys = jnp.concatenate([kc, kn, pad], axis=0)
        vals = jnp.concatenate([vc, vn, pad], axis=0)
        q = (q_ref[n] * ATTN_SCALE).astype(BF16)
        acc = None
        sink_terms = []
        for kv in range(N_KV_HEADS):
            s = lax.dot_general(q, jnp.where(kv_lane[kv], keys, 0.0).astype(BF16),
                                (((1,), (1,)), ((), ())), preferred_element_type=F32)
            s = jnp.where(mask, s + bias_ref[kv], NEG_INF)
            sink = sink_ref[kv][:, 0:1]
            m = jnp.maximum(jnp.max(s, axis=-1, keepdims=True), sink)
            p = jnp.exp(s - m).astype(BF16)
            sink_terms.append(jnp.exp(sink - m))
            rhs = jnp.concatenate([jnp.where(kv_lane[kv], vals, 0.0), jnp.where(kv_lane[kv], 1.0, 0.0)], axis=1)
            part = jnp.dot(p, rhs.astype(BF16), preferred_element_type=F32)
            acc = part if acc is None else acc + part
        denom = acc[:, KV_LANES:] + jnp.where(lane_q < HEAD_DIM, sink_terms[0], sink_terms[1])
        o_ref[n] = acc[:, :KV_LANES] / denom
        return carry

    for n in range(q_ref.shape[0]):
        one_batch(n, 0)


def _attn_sample(q, kc, vc, kn, vn, bias, sink, steps):
    nb = q.shape[0]
    q_rows = GROUP * steps
    bt = SAMPLE_BATCH_TILE

    def batch_spec(r):
        return pl.BlockSpec((bt, r, KV_LANES), lambda i: (i, 0, 0))

    return pl.pallas_call(
        functools.partial(_attn_sample_kernel, steps=steps),
        grid=(nb // bt,),
        in_specs=[batch_spec(q_rows), batch_spec(WINDOW), batch_spec(WINDOW),
                  batch_spec(SUBLANES), batch_spec(SUBLANES),
                  _resident((N_KV_HEADS, q_rows, KEY_TILE)), _resident((N_KV_HEADS, q_rows, KV_LANES))],
        out_specs=[batch_spec(q_rows), batch_spec(WINDOW), batch_spec(WINDOW)],
        out_shape=[jax.ShapeDtypeStruct((nb, q_rows, KV_LANES), F32),
                   jax.ShapeDtypeStruct((nb, WINDOW, KV_LANES), F32),
                   jax.ShapeDtypeStruct((nb, WINDOW, KV_LANES), F32)],
        compiler_params=_params(),
        name="attn_sample",
    )(q, kc, vc, kn, vn, bias, sink)


def kernel(x_prompt, x_sample, state_conv, cache_k, cache_v, g_mix, g_ffn, g_final, w_conv_in, conv_w,
           w_conv_out, w_q, w_k, w_v, w_o, sinks, rel_table, w_gate, w_up, w_down):
    batch, seq, _ = x_prompt.shape
    dec_batch, dec_seq, _ = x_sample.shape
    assert batch == 1 and seq % TOKEN_TILE == 0 and TOKEN_TILE % WINDOW == 0
    assert dec_batch % SAMPLE_BATCH_TILE == 0 and dec_seq <= SUBLANES
    assert (dec_batch * dec_seq) % TOKEN_TILE == 0
    assert g_mix.shape[0] == 2, "layer 0 is the conv mixer, layer 1 the attention mixer"

    win = w_conv_in[0].astype(BF16)
    wout = w_conv_out[0].astype(BF16)
    wq = w_q[0].reshape(D_MODEL, N_KV_HEADS, GROUP, HEAD_DIM).transpose(0, 2, 1, 3).reshape(D_MODEL, D_MODEL)
    wqkv = jnp.concatenate([wq, w_k[0], w_v[0]], axis=1).astype(BF16)
    wo = w_o[0].reshape(N_KV_HEADS, GROUP, HEAD_DIM, D_MODEL).transpose(1, 0, 2, 3).reshape(D_MODEL, D_MODEL)
    wo = wo.astype(BF16)
    wg, wu, wd = w_gate.astype(BF16), w_up.astype(BF16), w_down.astype(BF16)
    gm = g_mix.reshape(2, 1, D_MODEL)
    gf = g_ffn.reshape(2, 1, D_MODEL)
    gfin = g_final.reshape(1, D_MODEL)
    cw = conv_w[0]
    sink_vec = sinks[0]

    bucket_p = jnp.asarray(_t5_bucket_np(
        (np.arange(WINDOW)[:, None] + WINDOW) - np.arange(KEY_TILE)[None, :]))
    dist_s = np.arange(dec_seq)[:, None] + WINDOW - np.arange(KEY_TILE)[None, :]
    bucket_s = jnp.asarray(np.tile(_t5_bucket_np(dist_s), (GROUP, 1)))
    bias_p, bias_s, sink_s = _tables(bucket_p, bucket_s, rel_table, sink_vec, dec_seq)

    xp = x_prompt.reshape(seq, D_MODEL)
    xp, tail_p = _conv_prompt(xp, gm[0], win, cw, wout)
    xp = _ffn(xp, gf, wg, wu, wd, gfin, 0, False)
    xp, k_last, v_last = _attn_prompt(xp, gm[1], wqkv, wo, bias_p, sink_vec)
    y_prompt = _ffn(xp, gf, wg, wu, wd, gfin, 1, True).reshape(batch, seq, D_MODEL)
    state_conv_prompt = tail_p[SUBLANES - CONV_STATE:].reshape(1, batch, CONV_STATE, D_MODEL)
    cache_k_prompt = k_last.reshape(1, batch, WINDOW, N_KV_HEADS, HEAD_DIM)
    cache_v_prompt = v_last.reshape(1, batch, WINDOW, N_KV_HEADS, HEAD_DIM)

    n_s = dec_batch * dec_seq
    xs = x_sample.reshape(n_s, D_MODEL)
    prev = jnp.pad(state_conv[0], ((0, 0), (0, dec_seq - CONV_STATE), (0, 0))).reshape(n_s, D_MODEL)
    xs, u_s = _conv_sample(xs, prev, gm[0], win, cw, wout, dec_seq)
    state_conv_sample = u_s.reshape(dec_batch, dec_seq, D_MODEL)[:, dec_seq - CONV_STATE:][None]
    xs = _ffn(xs, gf, wg, wu, wd, gfin, 0, False)
    qkv = _proj(xs, gm[1], wqkv)
    q_s = qkv[:, :D_MODEL].reshape(dec_batch, dec_seq, GROUP, KV_LANES).transpose(0, 2, 1, 3)
    q_s = q_s.reshape(dec_batch, GROUP * dec_seq, KV_LANES)
    pad_new = ((0, 0), (0, SUBLANES - dec_seq), (0, 0))
    k_new = jnp.pad(qkv[:, D_MODEL:D_MODEL + KV_LANES].reshape(dec_batch, dec_seq, KV_LANES), pad_new)
    v_new = jnp.pad(qkv[:, D_MODEL + KV_LANES:].reshape(dec_batch, dec_seq, KV_LANES), pad_new)
    kc = cache_k[0].reshape(dec_batch, WINDOW, KV_LANES)
    vc = cache_v[0].reshape(dec_batch, WINDOW, KV_LANES)
    o_s, k_out, v_out = _attn_sample(q_s, kc, vc, k_new, v_new, bias_s, sink_s, dec_seq)
    o_s = o_s.reshape(dec_batch, GROUP, dec_seq, KV_LANES).transpose(0, 2, 1, 3).reshape(n_s, D_MODEL)
    xs = _oproj(xs, o_s, wo)
    y_sample = _ffn(xs, gf, wg, wu, wd, gfin, 1, True).reshape(dec_batch, dec_seq, D_MODEL)
    cache_k_sample = k_out.reshape(1, dec_batch, WINDOW, N_KV_HEADS, HEAD_DIM)
    cache_v_sample = v_out.reshape(1, dec_batch, WINDOW, N_KV_HEADS, HEAD_DIM)

    return (y_prompt, y_sample, state_conv_prompt, state_conv_sample,
            cache_k_prompt, cache_k_sample, cache_v_prompt, cache_v_sample)
```

```python
import functools
import math

import numpy as np
import jax
import jax.numpy as jnp
from jax import lax
from jax.experimental import pallas as pl
from jax.experimental.pallas import tpu as pltpu

D_MODEL = 1024
D_FF = 2816
HEAD_DIM = 64
N_HEADS = 16
N_KV_HEADS = 2
GROUP = N_HEADS // N_KV_HEADS
WINDOW = 128
N_BUCKETS = 32
MAX_DISTANCE = 128
CONV_STATE = 2
EPS = 1e-5
NEG_INF = -1e30
ATTN_SCALE = 1.0 / math.sqrt(HEAD_DIM)

KV_LANES = N_KV_HEADS * HEAD_DIM
QKV_COLS = D_MODEL + 2 * KV_LANES
SUBLANES = 8
KEY_TILE = 2 * WINDOW

TOKEN_TILE = 512
SAMPLE_BATCH_TILE = 16
VMEM_LIMIT_BYTES = 56 * 1024 * 1024

F32 = jnp.float32
BF16 = jnp.bfloat16


def _params(n_axes=1):
    return pltpu.CompilerParams(
        dimension_semantics=("arbitrary",) * n_axes,
        vmem_limit_bytes=VMEM_LIMIT_BYTES)


def _resident(shape):
    zeros = (0,) * len(shape)
    return pl.BlockSpec(shape, lambda *_: zeros, pipeline_mode=pl.Buffered(1))


def _layer(shape, layer):
    index = (layer,) + (0,) * len(shape)
    return pl.BlockSpec((None,) + tuple(shape), lambda *_: index, pipeline_mode=pl.Buffered(1))


def _smem():
    return pl.BlockSpec(memory_space=pltpu.SMEM)


def _rms(x, g):
    return x * lax.rsqrt(jnp.mean(x * x, axis=-1, keepdims=True) + EPS) * g


def _t5_bucket_np(dist):
    n = np.maximum(dist, 0)
    max_exact = N_BUCKETS // 2
    nf = np.maximum(n, 1).astype(np.float32)
    large = max_exact + (np.log(nf / np.float32(max_exact)) / np.float32(math.log(MAX_DISTANCE / max_exact))
                         * np.float32(N_BUCKETS - max_exact)).astype(np.int32)
    large = np.minimum(large, N_BUCKETS - 1)
    return np.where(n < max_exact, n, large).astype(np.int32)


def _ffn_kernel(x_ref, g_ref, wg_ref, wu_ref, wd_ref, gf_ref, o_ref, *, final_norm):
    x = x_ref[...]
    h = _rms(x, g_ref[...]).astype(BF16)
    gate = jnp.dot(h, wg_ref[...], preferred_element_type=F32)
    up = jnp.dot(h, wu_ref[...], preferred_element_type=F32)
    act = (gate * jax.nn.sigmoid(gate) * up).astype(BF16)
    y = x + jnp.dot(act, wd_ref[...], preferred_element_type=F32)
    if final_norm:
        y = _rms(y, gf_ref[...])
    o_ref[...] = y


def _ffn(x, g, wg, wu, wd, g_final, layer, final_norm):
    n = x.shape[0]
    tile = pl.BlockSpec((TOKEN_TILE, D_MODEL), lambda i: (i, 0))
    return pl.pallas_call(
        functools.partial(_ffn_kernel, final_norm=final_norm),
        grid=(n // TOKEN_TILE,),
        in_specs=[tile, _layer((1, D_MODEL), layer), _layer((D_MODEL, D_FF), layer),
                  _layer((D_MODEL, D_FF), layer), _layer((D_FF, D_MODEL), layer),
                  _resident((1, D_MODEL))],
        out_specs=tile,
        out_shape=jax.ShapeDtypeStruct((n, D_MODEL), F32),
        compiler_params=_params(),
        name="ffn_final" if final_norm else "ffn",
    )(x, g, wg, wu, wd, g_final)


def _conv_gates(x, g, win):
    h = _rms(x, g).astype(BF16)
    bcx = jnp.dot(h, win, preferred_element_type=F32)
    b = bcx[:, :D_MODEL]
    u = bcx[:, D_MODEL:2 * D_MODEL] * bcx[:, 2 * D_MODEL:]
    return b, u


def _conv_prompt_kernel(x_ref, g_ref, win_ref, cw_ref, wout_ref, o_ref, tail_out_ref, tail_ref):
    tail = jnp.where(pl.program_id(0) > 0, tail_ref[...], 0.0)
    x = x_ref[...]
    b, u = _conv_gates(x, g_ref[...], win_ref[...])
    rows = u.shape[0]
    r8 = lax.broadcasted_iota(jnp.int32, (SUBLANES, D_MODEL), 0)
    u1 = pltpu.roll(u, 1, 0)
    u2 = pltpu.roll(u, 2, 0)
    u1 = jnp.concatenate([jnp.where(r8 < 1, pltpu.roll(tail, 1, 0), u1[:SUBLANES]), u1[SUBLANES:]], axis=0)
    u2 = jnp.concatenate([jnp.where(r8 < 2, pltpu.roll(tail, 2, 0), u2[:SUBLANES]), u2[SUBLANES:]], axis=0)
    cw = cw_ref[...]
    v = cw[0:1] * u2 + cw[1:2] * u1 + cw[2:3] * u
    o_ref[...] = x + jnp.dot((b * v).astype(BF16), wout_ref[...], preferred_element_type=F32)
    new_tail = u[rows - SUBLANES:]
    tail_ref[...] = new_tail
    tail_out_ref[...] = new_tail


def _conv_prompt(x, g, win, cw, wout):
    n = x.shape[0]
    tile = pl.BlockSpec((TOKEN_TILE, D_MODEL), lambda i: (i, 0))
    return pl.pallas_call(
        _conv_prompt_kernel,
        grid=(n // TOKEN_TILE,),
        in_specs=[tile, _resident((1, D_MODEL)), _resident((D_MODEL, 3 * D_MODEL)),
                  _resident((3, D_MODEL)), _resident((D_MODEL, D_MODEL))],
        out_specs=[tile, pl.BlockSpec((SUBLANES, D_MODEL), lambda i: (0, 0))],
        out_shape=[jax.ShapeDtypeStruct((n, D_MODEL), F32),
                   jax.ShapeDtypeStruct((SUBLANES, D_MODEL), F32)],
        scratch_shapes=[pltpu.VMEM((SUBLANES, D_MODEL), F32)],
        compiler_params=_params(),
        name="conv_prompt",
    )(x, g, win, cw, wout)


def _conv_sample_kernel(x_ref, prev_ref, g_ref, win_ref, cw_ref, wout_ref, o_ref, u_ref, *, steps):
    x = x_ref[...]
    b, u = _conv_gates(x, g_ref[...], win_ref[...])
    rows = u.shape[0]
    prev = prev_ref[...]
    t = lax.broadcasted_iota(jnp.int32, u.shape, 0) % steps
    u1 = jnp.where(t >= 1, pltpu.roll(u, 1, 0), pltpu.roll(prev, rows - 1, 0))
    u2 = jnp.where(t >= 2, pltpu.roll(u, 2, 0), prev)
    cw = cw_ref[...]
    v = cw[0:1] * u2 + cw[1:2] * u1 + cw[2:3] * u
    o_ref[...] = x + jnp.dot((b * v).astype(BF16), wout_ref[...], preferred_element_type=F32)
    u_ref[...] = u


def _conv_sample(x, prev, g, win, cw, wout, steps):
    n = x.shape[0]
    full = _resident((n, D_MODEL))
    return pl.pallas_call(
        functools.partial(_conv_sample_kernel, steps=steps),
        grid=(1,),
        in_specs=[full, full, _resident((1, D_MODEL)), _resident((D_MODEL, 3 * D_MODEL)),
                  _resident((3, D_MODEL)), _resident((D_MODEL, D_MODEL))],
        out_specs=[pl.BlockSpec((n, D_MODEL), lambda i: (0, 0))] * 2,
        out_shape=[jax.ShapeDtypeStruct((n, D_MODEL), F32)] * 2,
        compiler_params=_params(),
        name="conv_sample",
    )(x, prev, g, win, cw, wout)


def _lookup_bias(bucket, table_ref, head):
    acc = jnp.zeros(bucket.shape, F32)
    for b in range(N_BUCKETS):
        acc = jnp.where(bucket == b, table_ref[b, head], acc)
    return acc


def _tables_kernel(bucket_p_ref, bucket_s_ref, table_ref, sinks_ref, bias_p_ref, bias_s_ref, sink_s_ref, *, steps):
    bucket_p = bucket_p_ref[...]
    for head in range(N_HEADS):
        bias_p_ref[head] = _lookup_bias(bucket_p, table_ref, head)
    bucket_s = bucket_s_ref[...]
    grp = lax.broadcasted_iota(jnp.int32, bucket_s.shape, 0) // steps
    grp_l = lax.broadcasted_iota(jnp.int32, (bucket_s.shape[0], KV_LANES), 0) // steps
    for kv in range(N_KV_HEADS):
        bias = jnp.zeros(bucket_s.shape, F32)
        sink = jnp.zeros((bucket_s.shape[0], KV_LANES), F32)
        for gi in range(GROUP):
            head = kv * GROUP + gi
            bias = jnp.where(grp == gi, _lookup_bias(bucket_s, table_ref, head), bias)
            sink = jnp.where(grp_l == gi, sinks_ref[head], sink)
        bias_s_ref[kv] = bias
        sink_s_ref[kv] = sink


def _tables(bucket_p, bucket_s, table, sinks, steps):
    q_rows = bucket_s.shape[0]
    return pl.pallas_call(
        functools.partial(_tables_kernel, steps=steps),
        grid=(1,),
        in_specs=[_resident((WINDOW, KEY_TILE)), _resident((q_rows, KEY_TILE)), _smem(), _smem()],
        out_specs=[pl.BlockSpec((N_HEADS, WINDOW, KEY_TILE), lambda i: (0, 0, 0)),
                   pl.BlockSpec((N_KV_HEADS, q_rows, KEY_TILE), lambda i: (0, 0, 0)),
                   pl.BlockSpec((N_KV_HEADS, q_rows, KV_LANES), lambda i: (0, 0, 0))],
        out_shape=[jax.ShapeDtypeStruct((N_HEADS, WINDOW, KEY_TILE), F32),
                   jax.ShapeDtypeStruct((N_KV_HEADS, q_rows, KEY_TILE), F32),
                   jax.ShapeDtypeStruct((N_KV_HEADS, q_rows, KV_LANES), F32)],
        compiler_params=_params(),
        name="bias_tables",
    )(bucket_p, bucket_s, table, sinks)


def _attn_prompt_kernel(xn_ref, xc_ref, g_ref, wqkv_ref, wo_ref, bias_ref, sinks_ref,
                        o_ref, klast_ref, vlast_ref, qbuf, kbuf, vbuf, kcar, vcar, obuf):
    step = pl.program_id(0)
    rows = xn_ref.shape[0]

    @pl.when(step == 0)
    def _():
        qbuf[1] = jnp.zeros(qbuf.shape[1:], BF16)
        kbuf[1] = jnp.zeros(kbuf.shape[1:], BF16)
        vbuf[1] = jnp.zeros(vbuf.shape[1:], BF16)
        kcar[...] = jnp.zeros(kcar.shape, F32)
        vcar[...] = jnp.zeros(vcar.shape, F32)

    def body(cur, nxt):
        lane = lax.broadcasted_iota(jnp.int32, (rows, KV_LANES), 1)
        kv_lane = (lane < HEAD_DIM, lane >= HEAD_DIM)
        k_prev = [jnp.where(step > 1, kcar[nxt, kv], 0.0).astype(BF16) for kv in range(N_KV_HEADS)]
        v_prev = [jnp.where(step > 1, vcar[nxt, kv], 0.0).astype(BF16) for kv in range(N_KV_HEADS)]

        h = _rms(xn_ref[...], g_ref[...]).astype(BF16)
        n_blk = rows // WINDOW
        q_cols = D_MODEL // n_blk

        def project_chunk(c):
            c0 = c * q_cols
            part = jnp.dot(h, wqkv_ref[:, c0:c0 + q_cols], preferred_element_type=F32)
            qbuf[nxt, :, c0:c0 + q_cols] = (part * ATTN_SCALE).astype(BF16)
            if c < n_blk - 1:
                return
            kv_part = jnp.dot(h, wqkv_ref[:, D_MODEL:], preferred_element_type=F32)
            k = kv_part[:, :KV_LANES]
            v = kv_part[:, KV_LANES:]
            klast_ref[...] = k[rows - WINDOW:]
            vlast_ref[...] = v[rows - WINDOW:]
            for kv in range(N_KV_HEADS):
                k_half = jnp.where(kv_lane[kv], k, 0.0)
                v_half = jnp.where(kv_lane[kv], v, 0.0)
                kbuf[nxt, kv] = k_half.astype(BF16)
                vbuf[nxt, kv] = v_half.astype(BF16)
                kcar[nxt, kv] = k_half[rows - WINDOW:]
                vcar[nxt, kv] = v_half[rows - WINDOW:]

        qi = lax.broadcasted_iota(jnp.int32, (WINDOW, KEY_TILE), 0)
        kj = lax.broadcasted_iota(jnp.int32, (WINDOW, KEY_TILE), 1)
        band = (kj >= qi) & (kj <= qi + WINDOW)
        lane_q = lax.broadcasted_iota(jnp.int32, (WINDOW, KV_LANES), 1)
        lane_k = lax.broadcasted_iota(jnp.int32, (KEY_TILE, KV_LANES), 1)
        ones_half = [(lane_k < HEAD_DIM).astype(F32).astype(BF16), (lane_k >= HEAD_DIM).astype(F32).astype(BF16)]

        for blk in range(rows // WINDOW):
            r0 = blk * WINDOW
            qs = jnp.concatenate(
                [qbuf[cur, r0:r0 + WINDOW, gi * KV_LANES:(gi + 1) * KV_LANES] for gi in range(GROUP)], axis=0)
            if blk == 0:
                mask = band & (kj >= jnp.where(step > 1, 0, WINDOW))
                keys = [jnp.concatenate([k_prev[kv], kbuf[cur, kv, 0:WINDOW, :]], axis=0) for kv in range(N_KV_HEADS)]
                vals = [jnp.concatenate([v_prev[kv], vbuf[cur, kv, 0:WINDOW, :]], axis=0) for kv in range(N_KV_HEADS)]
            else:
                mask = band
                keys = [kbuf[cur, kv, r0 - WINDOW:r0 + WINDOW, :] for kv in range(N_KV_HEADS)]
                vals = [vbuf[cur, kv, r0 - WINDOW:r0 + WINDOW, :] for kv in range(N_KV_HEADS)]
            probs, maxes = [], []
            for kv in range(N_KV_HEADS):
                s_all = lax.dot_general(qs, keys[kv], (((1,), (1,)), ((), ())), preferred_element_type=F32)
                p_kv, m_kv = [], []
                for gi in range(GROUP):
                    head = kv * GROUP + gi
                    s = s_all[gi * WINDOW:(gi + 1) * WINDOW]
                    s = jnp.where(mask, s + bias_ref[head], NEG_INF)
                    m = jnp.maximum(jnp.max(s, axis=-1, keepdims=True), sinks_ref[head])
                    p_kv.append(jnp.exp(s - m).astype(BF16))
                    m_kv.append(m)
                probs.append(jnp.concatenate(p_kv, axis=0))
                maxes.append(m_kv)
            project_chunk(blk)
            acc = None
            for kv in range(N_KV_HEADS):
                rhs = jnp.concatenate([vals[kv], ones_half[kv]], axis=1)
                part = jnp.dot(probs[kv], rhs, preferred_element_type=F32)
                acc = part if acc is None else acc + part
            outs = []
            for gi in range(GROUP):
                a = acc[gi * WINDOW:(gi + 1) * WINDOW]
                sink_term = jnp.where(lane_q < HEAD_DIM,
                                      jnp.exp(sinks_ref[gi] - maxes[0][gi]),
                                      jnp.exp(sinks_ref[GROUP + gi] - maxes[1][gi]))
                outs.append(a[:, :KV_LANES] / (a[:, KV_LANES:] + sink_term))
            obuf[r0:r0 + WINDOW, :] = jnp.concatenate(outs, axis=1).astype(BF16)

        o_ref[...] = xc_ref[...] + jnp.dot(obuf[...], wo_ref[...], preferred_element_type=F32)

    @pl.when(step % 2 == 0)
    def _():
        body(1, 0)

    @pl.when(step % 2 == 1)
    def _():
        body(0, 1)


def _attn_prompt(x, g, wqkv, wo, bias, sinks):
    n = x.shape[0]
    tiles = n // TOKEN_TILE
    nxt = pl.BlockSpec((TOKEN_TILE, D_MODEL), lambda i: (jnp.minimum(i, tiles - 1), 0))
    cur = pl.BlockSpec((TOKEN_TILE, D_MODEL), lambda i: (jnp.maximum(i - 1, 0), 0))
    last = pl.BlockSpec((WINDOW, KV_LANES), lambda i: (0, 0))
    return pl.pallas_call(
        _attn_prompt_kernel,
        grid=(tiles + 1,),
        in_specs=[nxt, cur, _resident((1, D_MODEL)), _resident((D_MODEL, QKV_COLS)),
                  _resident((D_MODEL, D_MODEL)), _resident((N_HEADS, WINDOW, KEY_TILE)), _smem()],
        out_specs=[cur, last, last],
        out_shape=[jax.ShapeDtypeStruct((n, D_MODEL), F32),
                   jax.ShapeDtypeStruct((WINDOW, KV_LANES), F32),
                   jax.ShapeDtypeStruct((WINDOW, KV_LANES), F32)],
        scratch_shapes=[pltpu.VMEM((2, TOKEN_TILE, D_MODEL), BF16),
                        pltpu.VMEM((2, N_KV_HEADS, TOKEN_TILE, KV_LANES), BF16),
                        pltpu.VMEM((2, N_KV_HEADS, TOKEN_TILE, KV_LANES), BF16),
                        pltpu.VMEM((2, N_KV_HEADS, WINDOW, KV_LANES), F32),
                        pltpu.VMEM((2, N_KV_HEADS, WINDOW, KV_LANES), F32),
                        pltpu.VMEM((TOKEN_TILE, D_MODEL), BF16)],
        compiler_params=_params(),
        name="attn_prompt",
    )(x, x, g, wqkv, wo, bias, sinks)


def _proj_kernel(x_ref, g_ref, w_ref, o_ref):
    h = _rms(x_ref[...], g_ref[...]).astype(BF16)
    o_ref[...] = jnp.dot(h, w_ref[...], preferred_element_type=F32)


def _proj(x, g, w):
    n, cols = x.shape[0], w.shape[1]
    return pl.pallas_call(
        _proj_kernel,
        grid=(1,),
        in_specs=[_resident((n, D_MODEL)), _resident((1, D_MODEL)), _resident((D_MODEL, cols))],
        out_specs=pl.BlockSpec((n, cols), lambda i: (0, 0)),
        out_shape=jax.ShapeDtypeStruct((n, cols), F32),
        compiler_params=_params(),
        name="proj_sample",
    )(x, g, w)


def _oproj_kernel(x_ref, a_ref, w_ref, o_ref):
    o_ref[...] = x_ref[...] + jnp.dot(a_ref[...].astype(BF16), w_ref[...], preferred_element_type=F32)


def _oproj(x, a, w):
    n = x.shape[0]
    return pl.pallas_call(
        _oproj_kernel,
        grid=(1,),
        in_specs=[_resident((n, D_MODEL)), _resident((n, D_MODEL)), _resident((D_MODEL, D_MODEL))],
        out_specs=pl.BlockSpec((n, D_MODEL), lambda i: (0, 0)),
        out_shape=jax.ShapeDtypeStruct((n, D_MODEL), F32),
        compiler_params=_params(),
        name="oproj_sample",
    )(x, a, w)


def _attn_sample_kernel(q_ref, kc_ref, vc_ref, kn_ref, vn_ref, bias_ref, sink_ref,
                        o_ref, kout_ref, vout_ref, *, steps):
    q_rows = GROUP * steps

    qi = lax.broadcasted_iota(jnp.int32, (q_rows, KEY_TILE), 0) % steps
    kj = lax.broadcasted_iota(jnp.int32, (q_rows, KEY_TILE), 1)
    mask = (kj >= qi) & (kj <= qi + WINDOW)
    lane = lax.broadcasted_iota(jnp.int32, (KEY_TILE, KV_LANES), 1)
    lane_q = lax.broadcasted_iota(jnp.int32, (q_rows, KV_LANES), 1)
    kv_lane = (lane < HEAD_DIM, lane >= HEAD_DIM)
    pad = jnp.zeros((KEY_TILE - WINDOW - SUBLANES, KV_LANES), F32)
    r8 = lax.broadcasted_iota(jnp.int32, (SUBLANES, KV_LANES), 0)

    def shifted_cache(cache, new):
        body = pltpu.roll(cache, WINDOW - steps, 0)
        tail = jnp.where(r8 >= SUBLANES - steps, pltpu.roll(new, SUBLANES - steps, 0),
                         body[WINDOW - SUBLANES:])
        return jnp.concatenate([body[:WINDOW - SUBLANES], tail], axis=0)

    def one_batch(n, carry):
        kc, vc, kn, vn = kc_ref[n], vc_ref[n], kn_ref[n], vn_ref[n]
        kout_ref[n] = shifted_cache(kc, kn)
        vout_ref[n] = shifted_cache(vc, vn)
        keys = jnp.concatenate([kc, kn, pad], axis=0)
        vals = jnp.concatenate([vc, vn, pad], axis=0)
        q = (q_ref[n] * ATTN_SCALE).astype(BF16)
        acc = None
        sink_terms = []
        for kv in range(N_KV_HEADS):
            s = lax.dot_general(q, jnp.where(kv_lane[kv], keys, 0.0).astype(BF16),
                                (((1,), (1,)), ((), ())), preferred_element_type=F32)
            s = jnp.where(mask, s + bias_ref[kv], NEG_INF)
            sink = sink_ref[kv][:, 0:1]
            m = jnp.maximum(jnp.max(s, axis=-1, keepdims=True), sink)
            p = jnp.exp(s - m).astype(BF16)
            sink_terms.append(jnp.exp(sink - m))
            rhs = jnp.concatenate([jnp.where(kv_lane[kv], vals, 0.0), jnp.where(kv_lane[kv], 1.0, 0.0)], axis=1)
            part = jnp.dot(p, rhs.astype(BF16), preferred_element_type=F32)
            acc = part if acc is None else acc + part
        denom = acc[:, KV_LANES:] + jnp.where(lane_q < HEAD_DIM, sink_terms[0], sink_terms[1])
        o_ref[n] = acc[:, :KV_LANES] / denom
        return carry

    for n in range(q_ref.shape[0]):
        one_batch(n, 0)


def _attn_sample(q, kc, vc, kn, vn, bias, sink, steps):
    nb = q.shape[0]
    q_rows = GROUP * steps
    bt = SAMPLE_BATCH_TILE

    def batch_spec(r):
        return pl.BlockSpec((bt, r, KV_LANES), lambda i: (i, 0, 0))

    return pl.pallas_call(
        functools.partial(_attn_sample_kernel, steps=steps),
        grid=(nb // bt,),
        in_specs=[batch_spec(q_rows), batch_spec(WINDOW), batch_spec(WINDOW),
                  batch_spec(SUBLANES), batch_spec(SUBLANES),
                  _resident((N_KV_HEADS, q_rows, KEY_TILE)), _resident((N_KV_HEADS, q_rows, KV_LANES))],
        out_specs=[batch_spec(q_rows), batch_spec(WINDOW), batch_spec(WINDOW)],
        out_shape=[jax.ShapeDtypeStruct((nb, q_rows, KV_LANES), F32),
                   jax.ShapeDtypeStruct((nb, WINDOW, KV_LANES), F32),
                   jax.ShapeDtypeStruct((nb, WINDOW, KV_LANES), F32)],
        compiler_params=_params(),
        name="attn_sample",
    )(q, kc, vc, kn, vn, bias, sink)


def kernel(x_prompt, x_sample, state_conv, cache_k, cache_v, g_mix, g_ffn, g_final, w_conv_in, conv_w,
           w_conv_out, w_q, w_k, w_v, w_o, sinks, rel_table, w_gate, w_up, w_down):
    batch, seq, _ = x_prompt.shape
    dec_batch, dec_seq, _ = x_sample.shape
    assert batch == 1 and seq % TOKEN_TILE == 0 and TOKEN_TILE % WINDOW == 0
    assert dec_batch % SAMPLE_BATCH_TILE == 0 and dec_seq <= SUBLANES
    assert (dec_batch * dec_seq) % TOKEN_TILE == 0
    assert g_mix.shape[0] == 2, "layer 0 is the conv mixer, layer 1 the attention mixer"

    win = w_conv_in[0].astype(BF16)
    wout = w_conv_out[0].astype(BF16)
    wq = w_q[0].reshape(D_MODEL, N_KV_HEADS, GROUP, HEAD_DIM).transpose(0, 2, 1, 3).reshape(D_MODEL, D_MODEL)
    wqkv = jnp.concatenate([wq, w_k[0], w_v[0]], axis=1).astype(BF16)
    wo = w_o[0].reshape(N_KV_HEADS, GROUP, HEAD_DIM, D_MODEL).transpose(1, 0, 2, 3).reshape(D_MODEL, D_MODEL)
    wo = wo.astype(BF16)
    wg, wu, wd = w_gate.astype(BF16), w_up.astype(BF16), w_down.astype(BF16)
    gm = g_mix.reshape(2, 1, D_MODEL)
    gf = g_ffn.reshape(2, 1, D_MODEL)
    gfin = g_final.reshape(1, D_MODEL)
    cw = conv_w[0]
    sink_vec = sinks[0]

    bucket_p = jnp.asarray(_t5_bucket_np(
        (np.arange(WINDOW)[:, None] + WINDOW) - np.arange(KEY_TILE)[None, :]))
    dist_s = np.arange(dec_seq)[:, None] + WINDOW - np.arange(KEY_TILE)[None, :]
    bucket_s = jnp.asarray(np.tile(_t5_bucket_np(dist_s), (GROUP, 1)))
    bias_p, bias_s, sink_s = _tables(bucket_p, bucket_s, rel_table, sink_vec, dec_seq)

    xp = x_prompt.reshape(seq, D_MODEL)
    xp, tail_p = _conv_prompt(xp, gm[0], win, cw, wout)
    xp = _ffn(xp, gf, wg, wu, wd, gfin, 0, False)
    xp, k_last, v_last = _attn_prompt(xp, gm[1], wqkv, wo, bias_p, sink_vec)
    y_prompt = _ffn(xp, gf, wg, wu, wd, gfin, 1, True).reshape(batch, seq, D_MODEL)
    state_conv_prompt = tail_p[SUBLANES - CONV_STATE:].reshape(1, batch, CONV_STATE, D_MODEL)
    cache_k_prompt = k_last.reshape(1, batch, WINDOW, N_KV_HEADS, HEAD_DIM)
    cache_v_prompt = v_last.reshape(1, batch, WINDOW, N_KV_HEADS, HEAD_DIM)

    n_s = dec_batch * dec_seq
    xs = x_sample.reshape(n_s, D_MODEL)
    prev = jnp.pad(state_conv[0], ((0, 0), (0, dec_seq - CONV_STATE), (0, 0))).reshape(n_s, D_MODEL)
    xs, u_s = _conv_sample(xs, prev, gm[0], win, cw, wout, dec_seq)
    state_conv_sample = u_s.reshape(dec_batch, dec_seq, D_MODEL)[:, dec_seq - CONV_STATE:][None]
    xs = _ffn(xs, gf, wg, wu, wd, gfin, 0, False)
    qkv = _proj(xs, gm[1], wqkv)
    q_s = qkv[:, :D_MODEL].reshape(dec_batch, dec_seq, GROUP, KV_LANES).transpose(0, 2, 1, 3)
    q_s = q_s.reshape(dec_batch, GROUP * dec_seq, KV_LANES)
    pad_new = ((0, 0), (0, SUBLANES - dec_seq), (0, 0))
    k_new = jnp.pad(qkv[:, D_MODEL:D_MODEL + KV_LANES].reshape(dec_batch, dec_seq, KV_LANES), pad_new)
    v_new = jnp.pad(qkv[:, D_MODEL + KV_LANES:].reshape(dec_batch, dec_seq, KV_LANES), pad_new)
    kc = cache_k[0].reshape(dec_batch, WINDOW, KV_LANES)
    vc = cache_v[0].reshape(dec_batch, WINDOW, KV_LANES)
    o_s, k_out, v_out = _attn_sample(q_s, kc, vc, k_new, v_new, bias_s, sink_s, dec_seq)
    o_s = o_s.reshape(dec_batch, GROUP, dec_seq, KV_LANES).transpose(0, 2, 1, 3).reshape(n_s, D_MODEL)
    xs = _oproj(xs, o_s, wo)
    y_sample = _ffn(xs, gf, wg, wu, wd, gfin, 1, True).reshape(dec_batch, dec_seq, D_MODEL)
    cache_k_sample = k_out.reshape(1, dec_batch, WINDOW, N_KV_HEADS, HEAD_DIM)
    cache_v_sample = v_out.reshape(1, dec_batch, WINDOW, N_KV_HEADS, HEAD_DIM)

    return (y_prompt, y_sample, state_conv_prompt, state_conv_sample,
            cache_k_prompt, cache_k_sample, cache_v_prompt, cache_v_sample)
```

```python
import functools
import math

import numpy as np
import jax
import jax.numpy as jnp
from jax import lax
from jax.experimental import pallas as pl
from jax.experimental.pallas import tpu as pltpu

D_MODEL = 1024
D_FF = 2816
HEAD_DIM = 64
N_HEADS = 16
N_KV_HEADS = 2
GROUP = N_HEADS // N_KV_HEADS
WINDOW = 128
N_BUCKETS = 32
MAX_DISTANCE = 128
CONV_STATE = 2
EPS = 1e-5
NEG_INF = -1e30
ATTN_SCALE = 1.0 / math.sqrt(HEAD_DIM)

KV_LANES = N_KV_HEADS * HEAD_DIM
QKV_COLS = D_MODEL + 2 * KV_LANES
SUBLANES = 8
KEY_TILE = 2 * WINDOW

TOKEN_TILE = 512
SAMPLE_BATCH_TILE = 16
VMEM_LIMIT_BYTES = 56 * 1024 * 1024

F32 = jnp.float32
BF16 = jnp.bfloat16


def _params(n_axes=1):
    return pltpu.CompilerParams(
        dimension_semantics=("arbitrary",) * n_axes,
        vmem_limit_bytes=VMEM_LIMIT_BYTES)


def _resident(shape):
    zeros = (0,) * len(shape)
    return pl.BlockSpec(shape, lambda *_: zeros, pipeline_mode=pl.Buffered(1))


def _layer(shape, layer):
    index = (layer,) + (0,) * len(shape)
    return pl.BlockSpec((None,) + tuple(shape), lambda *_: index, pipeline_mode=pl.Buffered(1))


def _smem():
    return pl.BlockSpec(memory_space=pltpu.SMEM)


def _rms(x, g):
    return x * lax.rsqrt(jnp.mean(x * x, axis=-1, keepdims=True) + EPS) * g


def _pipeline_specs(n):
    tiles = n // TOKEN_TILE
    nxt = pl.BlockSpec((TOKEN_TILE, D_MODEL), lambda i: (jnp.minimum(i, tiles - 1), 0))
    cur = pl.BlockSpec((TOKEN_TILE, D_MODEL), lambda i: (jnp.maximum(i - 1, 0), 0))
    return tiles, nxt, cur


def _two_stage(first, body):
    step = pl.program_id(0)

    @pl.when(step == 0)
    def _():
        first(0)

    @pl.when(step % 2 == 1)
    def _():
        body(0, 1)

    @pl.when((step % 2 == 0) & (step > 0))
    def _():
        body(1, 0)


def _t5_bucket_np(dist):
    n = np.maximum(dist, 0)
    max_exact = N_BUCKETS // 2
    nf = np.maximum(n, 1).astype(np.float32)
    large = max_exact + (np.log(nf / np.float32(max_exact)) / np.float32(math.log(MAX_DISTANCE / max_exact))
                         * np.float32(N_BUCKETS - max_exact)).astype(np.int32)
    large = np.minimum(large, N_BUCKETS - 1)
    return np.where(n < max_exact, n, large).astype(np.int32)


def _ffn_kernel(xn_ref, xc_ref, g_ref, wg_ref, wu_ref, wd_ref, gf_ref, o_ref, hbuf, *, final_norm):
    def norm_next(slot):
        hbuf[slot] = _rms(xn_ref[...], g_ref[...]).astype(BF16)

    def body(cur, nxt):
        norm_next(nxt)
        h = hbuf[cur]
        gate = jnp.dot(h, wg_ref[...], preferred_element_type=F32)
        up = jnp.dot(h, wu_ref[...], preferred_element_type=F32)
        act = (gate * jax.nn.sigmoid(gate) * up).astype(BF16)
        y = xc_ref[...] + jnp.dot(act, wd_ref[...], preferred_element_type=F32)
        if final_norm:
            y = _rms(y, gf_ref[...])
        o_ref[...] = y

    _two_stage(norm_next, body)


def _ffn(x, g, wg, wu, wd, g_final, layer, final_norm):
    n = x.shape[0]
    tiles, nxt, cur = _pipeline_specs(n)
    return pl.pallas_call(
        functools.partial(_ffn_kernel, final_norm=final_norm),
        grid=(tiles + 1,),
        in_specs=[nxt, cur, _layer((1, D_MODEL), layer), _layer((D_MODEL, D_FF), layer),
                  _layer((D_MODEL, D_FF), layer), _layer((D_FF, D_MODEL), layer),
                  _resident((1, D_MODEL))],
        out_specs=cur,
        out_shape=jax.ShapeDtypeStruct((n, D_MODEL), F32),
        scratch_shapes=[pltpu.VMEM((2, TOKEN_TILE, D_MODEL), BF16)],
        compiler_params=_params(),
        name="ffn_final" if final_norm else "ffn",
    )(x, x, g, wg, wu, wd, g_final)


def _conv_gates(h, win):
    bcx = jnp.dot(h, win, preferred_element_type=F32)
    b = bcx[:, :D_MODEL]
    u = bcx[:, D_MODEL:2 * D_MODEL] * bcx[:, 2 * D_MODEL:]
    return b, u


def _conv_prompt_kernel(xn_ref, xc_ref, g_ref, win_ref, cw_ref, wout_ref, o_ref, tail_out_ref, tail_ref, hbuf):
    def norm_next(slot):
        hbuf[slot] = _rms(xn_ref[...], g_ref[...]).astype(BF16)

    def first(slot):
        tail_ref[...] = jnp.zeros_like(tail_ref)
        norm_next(slot)

    def body(cur, nxt):
        norm_next(nxt)
        tail = tail_ref[...]
        b, u = _conv_gates(hbuf[cur], win_ref[...])
        rows = u.shape[0]
        r8 = lax.broadcasted_iota(jnp.int32, (SUBLANES, D_MODEL), 0)
        u1 = pltpu.roll(u, 1, 0)
        u2 = pltpu.roll(u, 2, 0)
        u1 = jnp.concatenate([jnp.where(r8 < 1, pltpu.roll(tail, 1, 0), u1[:SUBLANES]), u1[SUBLANES:]], axis=0)
        u2 = jnp.concatenate([jnp.where(r8 < 2, pltpu.roll(tail, 2, 0), u2[:SUBLANES]), u2[SUBLANES:]], axis=0)
        cw = cw_ref[...]
        v = cw[0:1] * u2 + cw[1:2] * u1 + cw[2:3] * u
        o_ref[...] = xc_ref[...] + jnp.dot((b * v).astype(BF16), wout_ref[...], preferred_element_type=F32)
        new_tail = u[rows - SUBLANES:]
        tail_ref[...] = new_tail
        tail_out_ref[...] = new_tail

    _two_stage(first, body)


def _conv_prompt(x, g, win, cw, wout):
    n = x.shape[0]
    tiles, nxt, cur = _pipeline_specs(n)
    return pl.pallas_call(
        _conv_prompt_kernel,
        grid=(tiles + 1,),
        in_specs=[nxt, cur, _resident((1, D_MODEL)), _resident((D_MODEL, 3 * D_MODEL)),
                  _resident((3, D_MODEL)), _resident((D_MODEL, D_MODEL))],
        out_specs=[cur, pl.BlockSpec((SUBLANES, D_MODEL), lambda i: (0, 0))],
        out_shape=[jax.ShapeDtypeStruct((n, D_MODEL), F32),
                   jax.ShapeDtypeStruct((SUBLANES, D_MODEL), F32)],
        scratch_shapes=[pltpu.VMEM((SUBLANES, D_MODEL), F32),
                        pltpu.VMEM((2, TOKEN_TILE, D_MODEL), BF16)],
        compiler_params=_params(),
        name="conv_prompt",
    )(x, x, g, win, cw, wout)


def _conv_sample_kernel(x_ref, prev_ref, g_ref, win_ref, cw_ref, wout_ref, o_ref, u_ref, *, steps):
    x = x_ref[...]
    b, u = _conv_gates(_rms(x, g_ref[...]).astype(BF16), win_ref[...])
    rows = u.shape[0]
    prev = prev_ref[...]
    t = lax.broadcasted_iota(jnp.int32, u.shape, 0) % steps
    u1 = jnp.where(t >= 1, pltpu.roll(u, 1, 0), pltpu.roll(prev, rows - 1, 0))
    u2 = jnp.where(t >= 2, pltpu.roll(u, 2, 0), prev)
    cw = cw_ref[...]
    v = cw[0:1] * u2 + cw[1:2] * u1 + cw[2:3] * u
    o_ref[...] = x + jnp.dot((b * v).astype(BF16), wout_ref[...], preferred_element_type=F32)
    u_ref[...] = u


def _conv_sample(x, prev, g, win, cw, wout, steps):
    n = x.shape[0]
    full = _resident((n, D_MODEL))
    return pl.pallas_call(
        functools.partial(_conv_sample_kernel, steps=steps),
        grid=(1,),
        in_specs=[full, full, _resident((1, D_MODEL)), _resident((D_MODEL, 3 * D_MODEL)),
                  _resident((3, D_MODEL)), _resident((D_MODEL, D_MODEL))],
        out_specs=[pl.BlockSpec((n, D_MODEL), lambda i: (0, 0))] * 2,
        out_shape=[jax.ShapeDtypeStruct((n, D_MODEL), F32)] * 2,
        compiler_params=_params(),
        name="conv_sample",
    )(x, prev, g, win, cw, wout)


def _lookup_bias(bucket, table_ref, head):
    acc = jnp.zeros(bucket.shape, F32)
    for b in range(N_BUCKETS):
        acc = jnp.where(bucket == b, table_ref[b, head], acc)
    return acc


def _tables_kernel(bucket_p_ref, bucket_s_ref, table_ref, sinks_ref, bias_p_ref, bias_s_ref, sink_s_ref, *, steps):
    bucket_p = bucket_p_ref[...]
    for head in range(N_HEADS):
        bias_p_ref[head] = _lookup_bias(bucket_p, table_ref, head)
    bucket_s = bucket_s_ref[...]
    grp = lax.broadcasted_iota(jnp.int32, bucket_s.shape, 0) // steps
    grp_l = lax.broadcasted_iota(jnp.int32, (bucket_s.shape[0], KV_LANES), 0) // steps
    for kv in range(N_KV_HEADS):
        bias = jnp.zeros(bucket_s.shape, F32)
        sink = jnp.zeros((bucket_s.shape[0], KV_LANES), F32)
        for gi in range(GROUP):
            head = kv * GROUP + gi
            bias = jnp.where(grp == gi, _lookup_bias(bucket_s, table_ref, head), bias)
            sink = jnp.where(grp_l == gi, sinks_ref[head], sink)
        bias_s_ref[kv] = bias
        sink_s_ref[kv] = sink


def _tables(bucket_p, bucket_s, table, sinks, steps):
    q_rows = bucket_s.shape[0]
    return pl.pallas_call(
        functools.partial(_tables_kernel, steps=steps),
        grid=(1,),
        in_specs=[_resident((WINDOW, KEY_TILE)), _resident((q_rows, KEY_TILE)), _smem(), _smem()],
        out_specs=[pl.BlockSpec((N_HEADS, WINDOW, KEY_TILE), lambda i: (0, 0, 0)),
                   pl.BlockSpec((N_KV_HEADS, q_rows, KEY_TILE), lambda i: (0, 0, 0)),
                   pl.BlockSpec((N_KV_HEADS, q_rows, KV_LANES), lambda i: (0, 0, 0))],
        out_shape=[jax.ShapeDtypeStruct((N_HEADS, WINDOW, KEY_TILE), F32),
                   jax.ShapeDtypeStruct((N_KV_HEADS, q_rows, KEY_TILE), F32),
                   jax.ShapeDtypeStruct((N_KV_HEADS, q_rows, KV_LANES), F32)],
        compiler_params=_params(),
        name="bias_tables",
    )(bucket_p, bucket_s, table, sinks)


def _attn_prompt_kernel(xn_ref, xc_ref, g_ref, wqkv_ref, wo_ref, bias_ref, sinks_ref,
                        o_ref, klast_ref, vlast_ref, qbuf, kbuf, vbuf, kcar, vcar, obuf):
    step = pl.program_id(0)
    rows = xn_ref.shape[0]
    n_blk = rows // WINDOW
    q_cols = D_MODEL // n_blk

    def norm_next():
        return _rms(xn_ref[...], g_ref[...]).astype(BF16)

    def project_chunk(h, slot, c):
        c0 = c * q_cols
        part = jnp.dot(h, wqkv_ref[:, c0:c0 + q_cols], preferred_element_type=F32)
        qbuf[slot, :, c0:c0 + q_cols] = (part * ATTN_SCALE).astype(BF16)
        if c < n_blk - 1:
            return
        kv_part = jnp.dot(h, wqkv_ref[:, D_MODEL:], preferred_element_type=F32)
        k = kv_part[:, :KV_LANES]
        v = kv_part[:, KV_LANES:]
        klast_ref[...] = k[rows - WINDOW:]
        vlast_ref[...] = v[rows - WINDOW:]
        lane = lax.broadcasted_iota(jnp.int32, (rows, KV_LANES), 1)
        for kv, half in enumerate((lane < HEAD_DIM, lane >= HEAD_DIM)):
            k_half = jnp.where(half, k, 0.0)
            v_half = jnp.where(half, v, 0.0)
            kbuf[slot, kv] = k_half.astype(BF16)
            vbuf[slot, kv] = v_half.astype(BF16)
            kcar[slot, kv] = k_half[rows - WINDOW:]
            vcar[slot, kv] = v_half[rows - WINDOW:]

    def first(slot):
        kcar[...] = jnp.zeros(kcar.shape, F32)
        vcar[...] = jnp.zeros(vcar.shape, F32)
        h = norm_next()
        for c in range(n_blk):
            project_chunk(h, slot, c)

    def body(cur, nxt):
        k_prev = [jnp.where(step > 1, kcar[nxt, kv], 0.0).astype(BF16) for kv in range(N_KV_HEADS)]
        v_prev = [jnp.where(step > 1, vcar[nxt, kv], 0.0).astype(BF16) for kv in range(N_KV_HEADS)]
        h = norm_next()

        qi = lax.broadcasted_iota(jnp.int32, (WINDOW, KEY_TILE), 0)
        kj = lax.broadcasted_iota(jnp.int32, (WINDOW, KEY_TILE), 1)
        band = (kj >= qi) & (kj <= qi + WINDOW)
        lane_q = lax.broadcasted_iota(jnp.int32, (WINDOW, KV_LANES), 1)
        lane_k = lax.broadcasted_iota(jnp.int32, (KEY_TILE, KV_LANES), 1)
        ones_half = [(lane_k < HEAD_DIM).astype(F32).astype(BF16), (lane_k >= HEAD_DIM).astype(F32).astype(BF16)]

        for blk in range(rows // WINDOW):
            r0 = blk * WINDOW
            qs = jnp.concatenate(
                [qbuf[cur, r0:r0 + WINDOW, gi * KV_LANES:(gi + 1) * KV_LANES] for gi in range(GROUP)], axis=0)
            if blk == 0:
                mask = band & (kj >= jnp.where(step > 1, 0, WINDOW))
                keys = [jnp.concatenate([k_prev[kv], kbuf[cur, kv, 0:WINDOW, :]], axis=0) for kv in range(N_KV_HEADS)]
                vals = [jnp.concatenate([v_prev[kv], vbuf[cur, kv, 0:WINDOW, :]], axis=0) for kv in range(N_KV_HEADS)]
            else:
                mask = band
                keys = [kbuf[cur, kv, r0 - WINDOW:r0 + WINDOW, :] for kv in range(N_KV_HEADS)]
                vals = [vbuf[cur, kv, r0 - WINDOW:r0 + WINDOW, :] for kv in range(N_KV_HEADS)]
            probs, maxes = [], []
            for kv in range(N_KV_HEADS):
                s_all = lax.dot_general(qs, keys[kv], (((1,), (1,)), ((), ())), preferred_element_type=F32)
                p_kv, m_kv = [], []
                for gi in range(GROUP):
                    head = kv * GROUP + gi
                    s = s_all[gi * WINDOW:(gi + 1) * WINDOW]
                    s = jnp.where(mask, s + bias_ref[head], NEG_INF)
                    m = jnp.maximum(jnp.max(s, axis=-1, keepdims=True), sinks_ref[head])
                    p_kv.append(jnp.exp(s - m).astype(BF16))
                    m_kv.append(m)
                probs.append(jnp.concatenate(p_kv, axis=0))
                maxes.append(m_kv)
            project_chunk(h, nxt, blk)
            acc = None
            for kv in range(N_KV_HEADS):
                rhs = jnp.concatenate([vals[kv], ones_half[kv]], axis=1)
                part = jnp.dot(probs[kv], rhs, preferred_element_type=F32)
                acc = part if acc is None else acc + part
            outs = []
            for gi in range(GROUP):
                a = acc[gi * WINDOW:(gi + 1) * WINDOW]
                sink_term = jnp.where(lane_q < HEAD_DIM,
                                      jnp.exp(sinks_ref[gi] - maxes[0][gi]),
                                      jnp.exp(sinks_ref[GROUP + gi] - maxes[1][gi]))
                outs.append(a[:, :KV_LANES] / (a[:, KV_LANES:] + sink_term))
            obuf[r0:r0 + WINDOW, :] = jnp.concatenate(outs, axis=1).astype(BF16)

        o_ref[...] = xc_ref[...] + jnp.dot(obuf[...], wo_ref[...], preferred_element_type=F32)

    _two_stage(first, body)


def _attn_prompt(x, g, wqkv, wo, bias, sinks):
    n = x.shape[0]
    tiles, nxt, cur = _pipeline_specs(n)
    last = pl.BlockSpec((WINDOW, KV_LANES), lambda i: (0, 0))
    return pl.pallas_call(
        _attn_prompt_kernel,
        grid=(tiles + 1,),
        in_specs=[nxt, cur, _resident((1, D_MODEL)), _resident((D_MODEL, QKV_COLS)),
                  _resident((D_MODEL, D_MODEL)), _resident((N_HEADS, WINDOW, KEY_TILE)), _smem()],
        out_specs=[cur, last, last],
        out_shape=[jax.ShapeDtypeStruct((n, D_MODEL), F32),
                   jax.ShapeDtypeStruct((WINDOW, KV_LANES), F32),
                   jax.ShapeDtypeStruct((WINDOW, KV_LANES), F32)],
        scratch_shapes=[pltpu.VMEM((2, TOKEN_TILE, D_MODEL), BF16),
                        pltpu.VMEM((2, N_KV_HEADS, TOKEN_TILE, KV_LANES), BF16),
                        pltpu.VMEM((2, N_KV_HEADS, TOKEN_TILE, KV_LANES), BF16),
                        pltpu.VMEM((2, N_KV_HEADS, WINDOW, KV_LANES), F32),
                        pltpu.VMEM((2, N_KV_HEADS, WINDOW, KV_LANES), F32),
                        pltpu.VMEM((TOKEN_TILE, D_MODEL), BF16)],
        compiler_params=_params(),
        name="attn_prompt",
    )(x, x, g, wqkv, wo, bias, sinks)


def _proj_kernel(x_ref, g_ref, w_ref, o_ref):
    h = _rms(x_ref[...], g_ref[...]).astype(BF16)
    o_ref[...] = jnp.dot(h, w_ref[...], preferred_element_type=F32)


def _proj(x, g, w):
    n, cols = x.shape[0], w.shape[1]
    return pl.pallas_call(
        _proj_kernel,
        grid=(1,),
        in_specs=[_resident((n, D_MODEL)), _resident((1, D_MODEL)), _resident((D_MODEL, cols))],
        out_specs=pl.BlockSpec((n, cols), lambda i: (0, 0)),
        out_shape=jax.ShapeDtypeStruct((n, cols), F32),
        compiler_params=_params(),
        name="proj_sample",
    )(x, g, w)


def _oproj_kernel(x_ref, a_ref, w_ref, o_ref):
    o_ref[...] = x_ref[...] + jnp.dot(a_ref[...].astype(BF16), w_ref[...], preferred_element_type=F32)


def _oproj(x, a, w):
    n = x.shape[0]
    return pl.pallas_call(
        _oproj_kernel,
        grid=(1,),
        in_specs=[_resident((n, D_MODEL)), _resident((n, D_MODEL)), _resident((D_MODEL, D_MODEL))],
        out_specs=pl.BlockSpec((n, D_MODEL), lambda i: (0, 0)),
        out_shape=jax.ShapeDtypeStruct((n, D_MODEL), F32),
        compiler_params=_params(),
        name="oproj_sample",
    )(x, a, w)


def _attn_sample_kernel(q_ref, kc_ref, vc_ref, kn_ref, vn_ref, bias_ref, sink_ref,
                        o_ref, kout_ref, vout_ref, *, steps):
    q_rows = GROUP * steps

    qi = lax.broadcasted_iota(jnp.int32, (q_rows, KEY_TILE), 0) % steps
    kj = lax.broadcasted_iota(jnp.int32, (q_rows, KEY_TILE), 1)
    mask = (kj >= qi) & (kj <= qi + WINDOW)
    lane = lax.broadcasted_iota(jnp.int32, (KEY_TILE, KV_LANES), 1)
    lane_q = lax.broadcasted_iota(jnp.int32, (q_rows, KV_LANES), 1)
    kv_lane = (lane < HEAD_DIM, lane >= HEAD_DIM)
    pad = jnp.zeros((KEY_TILE - WINDOW - SUBLANES, KV_LANES), F32)
    r8 = lax.broadcasted_iota(jnp.int32, (SUBLANES, KV_LANES), 0)

    def shifted_cache(cache, new):
        body = pltpu.roll(cache, WINDOW - steps, 0)
        tail = jnp.where(r8 >= SUBLANES - steps, pltpu.roll(new, SUBLANES - steps, 0),
                         body[WINDOW - SUBLANES:])
        return jnp.concatenate([body[:WINDOW - SUBLANES], tail], axis=0)

    def one_batch(n, carry):
        kc, vc, kn, vn = kc_ref[n], vc_ref[n], kn_ref[n], vn_ref[n]
        kout_ref[n] = shifted_cache(kc, kn)
        vout_ref[n] = shifted_cache(vc, vn)
        keys = jnp.concatenate([kc, kn, pad], axis=0)
        vals = jnp.concatenate([vc, vn, pad], axis=0)
        q = (q_ref[n] * ATTN_SCALE).astype(BF16)
        acc = None
        sink_terms = []
        for kv in range(N_KV_HEADS):
            s = lax.dot_general(q, jnp.where(kv_lane[kv], keys, 0.0).astype(BF16),
                                (((1,), (1,)), ((), ())), preferred_element_type=F32)
            s = jnp.where(mask, s + bias_ref[kv], NEG_INF)
            sink = sink_ref[kv][:, 0:1]
            m = jnp.maximum(jnp.max(s, axis=-1, keepdims=True), sink)
            p = jnp.exp(s - m).astype(BF16)
            sink_terms.append(jnp.exp(sink - m))
            rhs = jnp.concatenate([jnp.where(kv_lane[kv], vals, 0.0), jnp.where(kv_lane[kv], 1.0, 0.0)], axis=1)
            part = jnp.dot(p, rhs.astype(BF16), preferred_element_type=F32)
            acc = part if acc is None else acc + part
        denom = acc[:, KV_LANES:] + jnp.where(lane_q < HEAD_DIM, sink_terms[0], sink_terms[1])
        o_ref[n] = acc[:, :KV_LANES] / denom
        return carry

    for n in range(q_ref.shape[0]):
        one_batch(n, 0)


def _attn_sample(q, kc, vc, kn, vn, bias, sink, steps):
    nb = q.shape[0]
    q_rows = GROUP * steps
    bt = SAMPLE_BATCH_TILE

    def batch_spec(r):
        return pl.BlockSpec((bt, r, KV_LANES), lambda i: (i, 0, 0))

    return pl.pallas_call(
        functools.partial(_attn_sample_kernel, steps=steps),
        grid=(nb // bt,),
        in_specs=[batch_spec(q_rows), batch_spec(WINDOW), batch_spec(WINDOW),
                  batch_spec(SUBLANES), batch_spec(SUBLANES),
                  _resident((N_KV_HEADS, q_rows, KEY_TILE)), _resident((N_KV_HEADS, q_rows, KV_LANES))],
        out_specs=[batch_spec(q_rows), batch_spec(WINDOW), batch_spec(WINDOW)],
        out_shape=[jax.ShapeDtypeStruct((nb, q_rows, KV_LANES), F32),
                   jax.ShapeDtypeStruct((nb, WINDOW, KV_LANES), F32),
                   jax.ShapeDtypeStruct((nb, WINDOW, KV_LANES), F32)],
        compiler_params=_params(),
        name="attn_sample",
    )(q, kc, vc, kn, vn, bias, sink)


def kernel(x_prompt, x_sample, state_conv, cache_k, cache_v, g_mix, g_ffn, g_final, w_conv_in, conv_w,
           w_conv_out, w_q, w_k, w_v, w_o, sinks, rel_table, w_gate, w_up, w_down):
    batch, seq, _ = x_prompt.shape
    dec_batch, dec_seq, _ = x_sample.shape
    assert batch == 1 and seq % TOKEN_TILE == 0 and TOKEN_TILE % WINDOW == 0
    assert dec_batch % SAMPLE_BATCH_TILE == 0 and dec_seq <= SUBLANES
    assert (dec_batch * dec_seq) % TOKEN_TILE == 0
    assert g_mix.shape[0] == 2, "layer 0 is the conv mixer, layer 1 the attention mixer"

    win = w_conv_in[0].astype(BF16)
    wout = w_conv_out[0].astype(BF16)
    wq = w_q[0].reshape(D_MODEL, N_KV_HEADS, GROUP, HEAD_DIM).transpose(0, 2, 1, 3).reshape(D_MODEL, D_MODEL)
    wqkv = jnp.concatenate([wq, w_k[0], w_v[0]], axis=1).astype(BF16)
    wo = w_o[0].reshape(N_KV_HEADS, GROUP, HEAD_DIM, D_MODEL).transpose(1, 0, 2, 3).reshape(D_MODEL, D_MODEL)
    wo = wo.astype(BF16)
    wg, wu, wd = w_gate.astype(BF16), w_up.astype(BF16), w_down.astype(BF16)
    gm = g_mix.reshape(2, 1, D_MODEL)
    gf = g_ffn.reshape(2, 1, D_MODEL)
    gfin = g_final.reshape(1, D_MODEL)
    cw = conv_w[0]
    sink_vec = sinks[0]

    bucket_p = jnp.asarray(_t5_bucket_np(
        (np.arange(WINDOW)[:, None] + WINDOW) - np.arange(KEY_TILE)[None, :]))
    dist_s = np.arange(dec_seq)[:, None] + WINDOW - np.arange(KEY_TILE)[None, :]
    bucket_s = jnp.asarray(np.tile(_t5_bucket_np(dist_s), (GROUP, 1)))
    bias_p, bias_s, sink_s = _tables(bucket_p, bucket_s, rel_table, sink_vec, dec_seq)

    xp = x_prompt.reshape(seq, D_MODEL)
    xp, tail_p = _conv_prompt(xp, gm[0], win, cw, wout)
    xp = _ffn(xp, gf, wg, wu, wd, gfin, 0, False)
    xp, k_last, v_last = _attn_prompt(xp, gm[1], wqkv, wo, bias_p, sink_vec)
    y_prompt = _ffn(xp, gf, wg, wu, wd, gfin, 1, True).reshape(batch, seq, D_MODEL)
    state_conv_prompt = tail_p[SUBLANES - CONV_STATE:].reshape(1, batch, CONV_STATE, D_MODEL)
    cache_k_prompt = k_last.reshape(1, batch, WINDOW, N_KV_HEADS, HEAD_DIM)
    cache_v_prompt = v_last.reshape(1, batch, WINDOW, N_KV_HEADS, HEAD_DIM)

    n_s = dec_batch * dec_seq
    xs = x_sample.reshape(n_s, D_MODEL)
    prev = jnp.pad(state_conv[0], ((0, 0), (0, dec_seq - CONV_STATE), (0, 0))).reshape(n_s, D_MODEL)
    xs, u_s = _conv_sample(xs, prev, gm[0], win, cw, wout, dec_seq)
    state_conv_sample = u_s.reshape(dec_batch, dec_seq, D_MODEL)[:, dec_seq - CONV_STATE:][None]
    xs = _ffn(xs, gf, wg, wu, wd, gfin, 0, False)
    qkv = _proj(xs, gm[1], wqkv)
    q_s = qkv[:, :D_MODEL].reshape(dec_batch, dec_seq, GROUP, KV_LANES).transpose(0, 2, 1, 3)
    q_s = q_s.reshape(dec_batch, GROUP * dec_seq, KV_LANES)
    pad_new = ((0, 0), (0, SUBLANES - dec_seq), (0, 0))
    k_new = jnp.pad(qkv[:, D_MODEL:D_MODEL + KV_LANES].reshape(dec_batch, dec_seq, KV_LANES), pad_new)
    v_new = jnp.pad(qkv[:, D_MODEL + KV_LANES:].reshape(dec_batch, dec_seq, KV_LANES), pad_new)
    kc = cache_k[0].reshape(dec_batch, WINDOW, KV_LANES)
    vc = cache_v[0].reshape(dec_batch, WINDOW, KV_LANES)
    o_s, k_out, v_out = _attn_sample(q_s, kc, vc, k_new, v_new, bias_s, sink_s, dec_seq)
    o_s = o_s.reshape(dec_batch, GROUP, dec_seq, KV_LANES).transpose(0, 2, 1, 3).reshape(n_s, D_MODEL)
    xs = _oproj(xs, o_s, wo)
    y_sample = _ffn(xs, gf, wg, wu, wd, gfin, 1, True).reshape(dec_batch, dec_seq, D_MODEL)
    cache_k_sample = k_out.reshape(1, dec_batch, WINDOW, N_KV_HEADS, HEAD_DIM)
    cache_v_sample = v_out.reshape(1, dec_batch, WINDOW, N_KV_HEADS, HEAD_DIM)

    return (y_prompt, y_sample, state_conv_prompt, state_conv_sample,
            cache_k_prompt, cache_k_sample, cache_v_prompt, cache_v_sample)
```

```python
import functools
import math

import numpy as np
import jax
import jax.numpy as jnp
from jax import lax
from jax.experimental import pallas as pl
from jax.experimental.pallas import tpu as pltpu

D_MODEL = 1024
D_FF = 2816
HEAD_DIM = 64
N_HEADS = 16
N_KV_HEADS = 2
GROUP = N_HEADS // N_KV_HEADS
WINDOW = 128
N_BUCKETS = 32
MAX_DISTANCE = 128
CONV_STATE = 2
EPS = 1e-5
NEG_INF = -1e30
ATTN_SCALE = 1.0 / math.sqrt(HEAD_DIM)

KV_LANES = N_KV_HEADS * HEAD_DIM
QKV_COLS = D_MODEL + 2 * KV_LANES
SUBLANES = 8
KEY_TILE = 2 * WINDOW

TOKEN_TILE = 512
SAMPLE_BATCH_TILE = 16
VMEM_LIMIT_BYTES = 56 * 1024 * 1024

F32 = jnp.float32
BF16 = jnp.bfloat16


def _params(n_axes=1):
    return pltpu.CompilerParams(
        dimension_semantics=("arbitrary",) * n_axes,
        vmem_limit_bytes=VMEM_LIMIT_BYTES)


def _resident(shape):
    zeros = (0,) * len(shape)
    return pl.BlockSpec(shape, lambda *_: zeros, pipeline_mode=pl.Buffered(1))


def _layer(shape, layer):
    index = (layer,) + (0,) * len(shape)
    return pl.BlockSpec((None,) + tuple(shape), lambda *_: index, pipeline_mode=pl.Buffered(1))


def _smem():
    return pl.BlockSpec(memory_space=pltpu.SMEM)


def _rms(x, g):
    return x * lax.rsqrt(jnp.mean(x * x, axis=-1, keepdims=True) + EPS) * g


def _pipeline_specs(n):
    tiles = n // TOKEN_TILE
    nxt = pl.BlockSpec((TOKEN_TILE, D_MODEL), lambda i: (jnp.minimum(i, tiles - 1), 0))
    cur = pl.BlockSpec((TOKEN_TILE, D_MODEL), lambda i: (jnp.maximum(i - 1, 0), 0))
    return tiles, nxt, cur


def _two_stage(first, body):
    step = pl.program_id(0)

    @pl.when(step == 0)
    def _():
        first(0)

    @pl.when(step % 2 == 1)
    def _():
        body(0, 1)

    @pl.when((step % 2 == 0) & (step > 0))
    def _():
        body(1, 0)


def _t5_bucket_np(dist):
    n = np.maximum(dist, 0)
    max_exact = N_BUCKETS // 2
    nf = np.maximum(n, 1).astype(np.float32)
    large = max_exact + (np.log(nf / np.float32(max_exact)) / np.float32(math.log(MAX_DISTANCE / max_exact))
                         * np.float32(N_BUCKETS - max_exact)).astype(np.int32)
    large = np.minimum(large, N_BUCKETS - 1)
    return np.where(n < max_exact, n, large).astype(np.int32)


def _ffn_kernel(x_ref, g_ref, wg_ref, wu_ref, wd_ref, gf_ref, o_ref, *, final_norm):
    x = x_ref[...]
    h = _rms(x, g_ref[...]).astype(BF16)
    gate = jnp.dot(h, wg_ref[...], preferred_element_type=F32)
    up = jnp.dot(h, wu_ref[...], preferred_element_type=F32)
    act = (gate * jax.nn.sigmoid(gate) * up).astype(BF16)
    y = x + jnp.dot(act, wd_ref[...], preferred_element_type=F32)
    if final_norm:
        y = _rms(y, gf_ref[...])
    o_ref[...] = y


def _ffn(x, g, wg, wu, wd, g_final, layer, final_norm):
    n = x.shape[0]
    tile = pl.BlockSpec((TOKEN_TILE, D_MODEL), lambda i: (i, 0))
    return pl.pallas_call(
        functools.partial(_ffn_kernel, final_norm=final_norm),
        grid=(n // TOKEN_TILE,),
        in_specs=[tile, _layer((1, D_MODEL), layer), _layer((D_MODEL, D_FF), layer),
                  _layer((D_MODEL, D_FF), layer), _layer((D_FF, D_MODEL), layer),
                  _resident((1, D_MODEL))],
        out_specs=tile,
        out_shape=jax.ShapeDtypeStruct((n, D_MODEL), F32),
        compiler_params=_params(),
        name="ffn_final" if final_norm else "ffn",
    )(x, g, wg, wu, wd, g_final)


def _conv_gates(h, win):
    bcx = jnp.dot(h, win, preferred_element_type=F32)
    b = bcx[:, :D_MODEL]
    u = bcx[:, D_MODEL:2 * D_MODEL] * bcx[:, 2 * D_MODEL:]
    return b, u


def _conv_prompt_kernel(x_ref, g_ref, win_ref, cw_ref, wout_ref, o_ref, tail_out_ref, tail_ref):
    tail = jnp.where(pl.program_id(0) > 0, tail_ref[...], 0.0)
    x = x_ref[...]
    b, u = _conv_gates(_rms(x, g_ref[...]).astype(BF16), win_ref[...])
    rows = u.shape[0]
    r8 = lax.broadcasted_iota(jnp.int32, (SUBLANES, D_MODEL), 0)
    u1 = pltpu.roll(u, 1, 0)
    u2 = pltpu.roll(u, 2, 0)
    u1 = jnp.concatenate([jnp.where(r8 < 1, pltpu.roll(tail, 1, 0), u1[:SUBLANES]), u1[SUBLANES:]], axis=0)
    u2 = jnp.concatenate([jnp.where(r8 < 2, pltpu.roll(tail, 2, 0), u2[:SUBLANES]), u2[SUBLANES:]], axis=0)
    cw = cw_ref[...]
    v = cw[0:1] * u2 + cw[1:2] * u1 + cw[2:3] * u
    o_ref[...] = x + jnp.dot((b * v).astype(BF16), wout_ref[...], preferred_element_type=F32)
    new_tail = u[rows - SUBLANES:]
    tail_ref[...] = new_tail
    tail_out_ref[...] = new_tail


def _conv_prompt(x, g, win, cw, wout):
    n = x.shape[0]
    tile = pl.BlockSpec((TOKEN_TILE, D_MODEL), lambda i: (i, 0))
    return pl.pallas_call(
        _conv_prompt_kernel,
        grid=(n // TOKEN_TILE,),
        in_specs=[tile, _resident((1, D_MODEL)), _resident((D_MODEL, 3 * D_MODEL)),
                  _resident((3, D_MODEL)), _resident((D_MODEL, D_MODEL))],
        out_specs=[tile, pl.BlockSpec((SUBLANES, D_MODEL), lambda i: (0, 0))],
        out_shape=[jax.ShapeDtypeStruct((n, D_MODEL), F32),
                   jax.ShapeDtypeStruct((SUBLANES, D_MODEL), F32)],
        scratch_shapes=[pltpu.VMEM((SUBLANES, D_MODEL), F32)],
        compiler_params=_params(),
        name="conv_prompt",
    )(x, g, win, cw, wout)


def _conv_sample_kernel(x_ref, prev_ref, g_ref, win_ref, cw_ref, wout_ref, o_ref, u_ref, *, steps):
    x = x_ref[...]
    b, u = _conv_gates(_rms(x, g_ref[...]).astype(BF16), win_ref[...])
    rows = u.shape[0]
    prev = prev_ref[...]
    t = lax.broadcasted_iota(jnp.int32, u.shape, 0) % steps
    u1 = jnp.where(t >= 1, pltpu.roll(u, 1, 0), pltpu.roll(prev, rows - 1, 0))
    u2 = jnp.where(t >= 2, pltpu.roll(u, 2, 0), prev)
    cw = cw_ref[...]
    v = cw[0:1] * u2 + cw[1:2] * u1 + cw[2:3] * u
    o_ref[...] = x + jnp.dot((b * v).astype(BF16), wout_ref[...], preferred_element_type=F32)
    u_ref[...] = u


def _conv_sample(x, prev, g, win, cw, wout, steps):
    n = x.shape[0]
    full = _resident((n, D_MODEL))
    return pl.pallas_call(
        functools.partial(_conv_sample_kernel, steps=steps),
        grid=(1,),
        in_specs=[full, full, _resident((1, D_MODEL)), _resident((D_MODEL, 3 * D_MODEL)),
                  _resident((3, D_MODEL)), _resident((D_MODEL, D_MODEL))],
        out_specs=[pl.BlockSpec((n, D_MODEL), lambda i: (0, 0))] * 2,
        out_shape=[jax.ShapeDtypeStruct((n, D_MODEL), F32)] * 2,
        compiler_params=_params(),
        name="conv_sample",
    )(x, prev, g, win, cw, wout)


def _lookup_bias(bucket, table_ref, head):
    acc = jnp.zeros(bucket.shape, F32)
    for b in range(N_BUCKETS):
        acc = jnp.where(bucket == b, table_ref[b, head], acc)
    return acc


def _tables_kernel(bucket_p_ref, bucket_s_ref, table_ref, sinks_ref, bias_p_ref, bias_s_ref, sink_s_ref, *, steps):
    bucket_p = bucket_p_ref[...]
    for head in range(N_HEADS):
        bias_p_ref[head] = _lookup_bias(bucket_p, table_ref, head)
    bucket_s = bucket_s_ref[...]
    grp = lax.broadcasted_iota(jnp.int32, bucket_s.shape, 0) // steps
    grp_l = lax.broadcasted_iota(jnp.int32, (bucket_s.shape[0], KV_LANES), 0) // steps
    for kv in range(N_KV_HEADS):
        bias = jnp.zeros(bucket_s.shape, F32)
        sink = jnp.zeros((bucket_s.shape[0], KV_LANES), F32)
        for gi in range(GROUP):
            head = kv * GROUP + gi
            bias = jnp.where(grp == gi, _lookup_bias(bucket_s, table_ref, head), bias)
            sink = jnp.where(grp_l == gi, sinks_ref[head], sink)
        bias_s_ref[kv] = bias
        sink_s_ref[kv] = sink


def _tables(bucket_p, bucket_s, table, sinks, steps):
    q_rows = bucket_s.shape[0]
    return pl.pallas_call(
        functools.partial(_tables_kernel, steps=steps),
        grid=(1,),
        in_specs=[_resident((WINDOW, KEY_TILE)), _resident((q_rows, KEY_TILE)), _smem(), _smem()],
        out_specs=[pl.BlockSpec((N_HEADS, WINDOW, KEY_TILE), lambda i: (0, 0, 0)),
                   pl.BlockSpec((N_KV_HEADS, q_rows, KEY_TILE), lambda i: (0, 0, 0)),
                   pl.BlockSpec((N_KV_HEADS, q_rows, KV_LANES), lambda i: (0, 0, 0))],
        out_shape=[jax.ShapeDtypeStruct((N_HEADS, WINDOW, KEY_TILE), F32),
                   jax.ShapeDtypeStruct((N_KV_HEADS, q_rows, KEY_TILE), F32),
                   jax.ShapeDtypeStruct((N_KV_HEADS, q_rows, KV_LANES), F32)],
        compiler_params=_params(),
        name="bias_tables",
    )(bucket_p, bucket_s, table, sinks)


def _attn_prompt_kernel(xn_ref, xc_ref, g_ref, wqkv_ref, wo_ref, bias_ref, sinks_ref,
                        o_ref, klast_ref, vlast_ref, qbuf, kbuf, vbuf, kcar, vcar, obuf):
    step = pl.program_id(0)
    rows = xn_ref.shape[0]
    n_blk = rows // WINDOW
    q_cols = D_MODEL // n_blk

    def norm_next():
        return _rms(xn_ref[...], g_ref[...]).astype(BF16)

    def project_chunk(h, slot, c):
        c0 = c * q_cols
        part = jnp.dot(h, wqkv_ref[:, c0:c0 + q_cols], preferred_element_type=F32)
        qbuf[slot, :, c0:c0 + q_cols] = (part * ATTN_SCALE).astype(BF16)
        if c < n_blk - 1:
            return
        kv_part = jnp.dot(h, wqkv_ref[:, D_MODEL:], preferred_element_type=F32)
        k = kv_part[:, :KV_LANES]
        v = kv_part[:, KV_LANES:]
        klast_ref[...] = k[rows - WINDOW:]
        vlast_ref[...] = v[rows - WINDOW:]
        lane = lax.broadcasted_iota(jnp.int32, (rows, KV_LANES), 1)
        for kv, half in enumerate((lane < HEAD_DIM, lane >= HEAD_DIM)):
            k_half = jnp.where(half, k, 0.0)
            v_half = jnp.where(half, v, 0.0)
            kbuf[slot, kv] = k_half.astype(BF16)
            vbuf[slot, kv] = v_half.astype(BF16)
            kcar[slot, kv] = k_half[rows - WINDOW:]
            vcar[slot, kv] = v_half[rows - WINDOW:]

    def first(slot):
        kcar[...] = jnp.zeros(kcar.shape, F32)
        vcar[...] = jnp.zeros(vcar.shape, F32)
        h = norm_next()
        for c in range(n_blk):
            project_chunk(h, slot, c)

    def body(cur, nxt):
        k_prev = [jnp.where(step > 1, kcar[nxt, kv], 0.0).astype(BF16) for kv in range(N_KV_HEADS)]
        v_prev = [jnp.where(step > 1, vcar[nxt, kv], 0.0).astype(BF16) for kv in range(N_KV_HEADS)]
        h = norm_next()

        qi = lax.broadcasted_iota(jnp.int32, (WINDOW, KEY_TILE), 0)
        kj = lax.broadcasted_iota(jnp.int32, (WINDOW, KEY_TILE), 1)
        band = (kj >= qi) & (kj <= qi + WINDOW)
        lane_q = lax.broadcasted_iota(jnp.int32, (WINDOW, KV_LANES), 1)
        lane_k = lax.broadcasted_iota(jnp.int32, (KEY_TILE, KV_LANES), 1)
        ones_half = [(lane_k < HEAD_DIM).astype(F32).astype(BF16), (lane_k >= HEAD_DIM).astype(F32).astype(BF16)]

        def block_scores(blk):
            r0 = blk * WINDOW
            qs = jnp.concatenate(
                [qbuf[cur, r0:r0 + WINDOW, gi * KV_LANES:(gi + 1) * KV_LANES] for gi in range(GROUP)], axis=0)
            if blk == 0:
                keys = [jnp.concatenate([k_prev[kv], kbuf[cur, kv, 0:WINDOW, :]], axis=0) for kv in range(N_KV_HEADS)]
            else:
                keys = [kbuf[cur, kv, r0 - WINDOW:r0 + WINDOW, :] for kv in range(N_KV_HEADS)]
            return [lax.dot_general(qs, keys[kv], (((1,), (1,)), ((), ())), preferred_element_type=F32)
                    for kv in range(N_KV_HEADS)]

        scores = block_scores(0)
        for blk in range(n_blk):
            r0 = blk * WINDOW
            if blk == 0:
                mask = band & (kj >= jnp.where(step > 1, 0, WINDOW))
                vals = [jnp.concatenate([v_prev[kv], vbuf[cur, kv, 0:WINDOW, :]], axis=0) for kv in range(N_KV_HEADS)]
            else:
                mask = band
                vals = [vbuf[cur, kv, r0 - WINDOW:r0 + WINDOW, :] for kv in range(N_KV_HEADS)]
            next_scores = block_scores(blk + 1) if blk + 1 < n_blk else None
            project_chunk(h, nxt, blk)
            probs, maxes = [], []
            for kv in range(N_KV_HEADS):
                s_all = scores[kv]
                p_kv, m_kv = [], []
                for gi in range(GROUP):
                    head = kv * GROUP + gi
                    s = s_all[gi * WINDOW:(gi + 1) * WINDOW]
                    s = jnp.where(mask, s + bias_ref[head], NEG_INF)
                    m = jnp.maximum(jnp.max(s, axis=-1, keepdims=True), sinks_ref[head])
                    p_kv.append(jnp.exp(s - m).astype(BF16))
                    m_kv.append(m)
                probs.append(jnp.concatenate(p_kv, axis=0))
                maxes.append(m_kv)
            scores = next_scores
            acc = None
            for kv in range(N_KV_HEADS):
                rhs = jnp.concatenate([vals[kv], ones_half[kv]], axis=1)
                part = jnp.dot(probs[kv], rhs, preferred_element_type=F32)
                acc = part if acc is None else acc + part
            outs = []
            for gi in range(GROUP):
                a = acc[gi * WINDOW:(gi + 1) * WINDOW]
                sink_term = jnp.where(lane_q < HEAD_DIM,
                                      jnp.exp(sinks_ref[gi] - maxes[0][gi]),
                                      jnp.exp(sinks_ref[GROUP + gi] - maxes[1][gi]))
                outs.append(a[:, :KV_LANES] / (a[:, KV_LANES:] + sink_term))
            obuf[r0:r0 + WINDOW, :] = jnp.concatenate(outs, axis=1).astype(BF16)

        o_ref[...] = xc_ref[...] + jnp.dot(obuf[...], wo_ref[...], preferred_element_type=F32)

    _two_stage(first, body)


def _attn_prompt(x, g, wqkv, wo, bias, sinks):
    n = x.shape[0]
    tiles, nxt, cur = _pipeline_specs(n)
    last = pl.BlockSpec((WINDOW, KV_LANES), lambda i: (0, 0))
    return pl.pallas_call(
        _attn_prompt_kernel,
        grid=(tiles + 1,),
        in_specs=[nxt, cur, _resident((1, D_MODEL)), _resident((D_MODEL, QKV_COLS)),
                  _resident((D_MODEL, D_MODEL)), _resident((N_HEADS, WINDOW, KEY_TILE)), _smem()],
        out_specs=[cur, last, last],
        out_shape=[jax.ShapeDtypeStruct((n, D_MODEL), F32),
                   jax.ShapeDtypeStruct((WINDOW, KV_LANES), F32),
                   jax.ShapeDtypeStruct((WINDOW, KV_LANES), F32)],
        scratch_shapes=[pltpu.VMEM((2, TOKEN_TILE, D_MODEL), BF16),
                        pltpu.VMEM((2, N_KV_HEADS, TOKEN_TILE, KV_LANES), BF16),
                        pltpu.VMEM((2, N_KV_HEADS, TOKEN_TILE, KV_LANES), BF16),
                        pltpu.VMEM((2, N_KV_HEADS, WINDOW, KV_LANES), F32),
                        pltpu.VMEM((2, N_KV_HEADS, WINDOW, KV_LANES), F32),
                        pltpu.VMEM((TOKEN_TILE, D_MODEL), BF16)],
        compiler_params=_params(),
        name="attn_prompt",
    )(x, x, g, wqkv, wo, bias, sinks)


def _proj_kernel(x_ref, g_ref, w_ref, o_ref):
    h = _rms(x_ref[...], g_ref[...]).astype(BF16)
    o_ref[...] = jnp.dot(h, w_ref[...], preferred_element_type=F32)


def _proj(x, g, w):
    n, cols = x.shape[0], w.shape[1]
    return pl.pallas_call(
        _proj_kernel,
        grid=(1,),
        in_specs=[_resident((n, D_MODEL)), _resident((1, D_MODEL)), _resident((D_MODEL, cols))],
        out_specs=pl.BlockSpec((n, cols), lambda i: (0, 0)),
        out_shape=jax.ShapeDtypeStruct((n, cols), F32),
        compiler_params=_params(),
        name="proj_sample",
    )(x, g, w)


def _oproj_kernel(x_ref, a_ref, w_ref, o_ref):
    o_ref[...] = x_ref[...] + jnp.dot(a_ref[...].astype(BF16), w_ref[...], preferred_element_type=F32)


def _oproj(x, a, w):
    n = x.shape[0]
    return pl.pallas_call(
        _oproj_kernel,
        grid=(1,),
        in_specs=[_resident((n, D_MODEL)), _resident((n, D_MODEL)), _resident((D_MODEL, D_MODEL))],
        out_specs=pl.BlockSpec((n, D_MODEL), lambda i: (0, 0)),
        out_shape=jax.ShapeDtypeStruct((n, D_MODEL), F32),
        compiler_params=_params(),
        name="oproj_sample",
    )(x, a, w)


def _attn_sample_kernel(q_ref, kc_ref, vc_ref, kn_ref, vn_ref, bias_ref, sink_ref,
                        o_ref, kout_ref, vout_ref, *, steps):
    q_rows = GROUP * steps

    qi = lax.broadcasted_iota(jnp.int32, (q_rows, KEY_TILE), 0) % steps
    kj = lax.broadcasted_iota(jnp.int32, (q_rows, KEY_TILE), 1)
    mask = (kj >= qi) & (kj <= qi + WINDOW)
    lane = lax.broadcasted_iota(jnp.int32, (KEY_TILE, KV_LANES), 1)
    lane_q = lax.broadcasted_iota(jnp.int32, (q_rows, KV_LANES), 1)
    kv_lane = (lane < HEAD_DIM, lane >= HEAD_DIM)
    pad = jnp.zeros((KEY_TILE - WINDOW - SUBLANES, KV_LANES), F32)
    r8 = lax.broadcasted_iota(jnp.int32, (SUBLANES, KV_LANES), 0)

    def shifted_cache(cache, new):
        body = pltpu.roll(cache, WINDOW - steps, 0)
        tail = jnp.where(r8 >= SUBLANES - steps, pltpu.roll(new, SUBLANES - steps, 0),
                         body[WINDOW - SUBLANES:])
        return jnp.concatenate([body[:WINDOW - SUBLANES], tail], axis=0)

    batches = range(q_ref.shape[0])
    scores = []
    for n in batches:
        kc, kn = kc_ref[n], kn_ref[n]
        kout_ref[n] = shifted_cache(kc, kn)
        keys = jnp.concatenate([kc, kn, pad], axis=0)
        q = (q_ref[n] * ATTN_SCALE).astype(BF16)
        scores.append([lax.dot_general(q, jnp.where(kv_lane[kv], keys, 0.0).astype(BF16),
                                       (((1,), (1,)), ((), ())), preferred_element_type=F32)
                       for kv in range(N_KV_HEADS)])
    probs, sink_terms = [], []
    for n in batches:
        p_n, t_n = [], []
        for kv in range(N_KV_HEADS):
            s = jnp.where(mask, scores[n][kv] + bias_ref[kv], NEG_INF)
            sink = sink_ref[kv][:, 0:1]
            m = jnp.maximum(jnp.max(s, axis=-1, keepdims=True), sink)
            p_n.append(jnp.exp(s - m).astype(BF16))
            t_n.append(jnp.exp(sink - m))
        probs.append(p_n)
        sink_terms.append(t_n)
    for n in batches:
        vc, vn = vc_ref[n], vn_ref[n]
        vout_ref[n] = shifted_cache(vc, vn)
        vals = jnp.concatenate([vc, vn, pad], axis=0)
        acc = None
        for kv in range(N_KV_HEADS):
            rhs = jnp.concatenate([jnp.where(kv_lane[kv], vals, 0.0), jnp.where(kv_lane[kv], 1.0, 0.0)], axis=1)
            part = jnp.dot(probs[n][kv], rhs.astype(BF16), preferred_element_type=F32)
            acc = part if acc is None else acc + part
        denom = acc[:, KV_LANES:] + jnp.where(lane_q < HEAD_DIM, sink_terms[n][0], sink_terms[n][1])
        o_ref[n] = acc[:, :KV_LANES] / denom


def _attn_sample(q, kc, vc, kn, vn, bias, sink, steps):
    nb = q.shape[0]
    q_rows = GROUP * steps
    bt = SAMPLE_BATCH_TILE

    def batch_spec(r):
        return pl.BlockSpec((bt, r, KV_LANES), lambda i: (i, 0, 0))

    return pl.pallas_call(
        functools.partial(_attn_sample_kernel, steps=steps),
        grid=(nb // bt,),
        in_specs=[batch_spec(q_rows), batch_spec(WINDOW), batch_spec(WINDOW),
                  batch_spec(SUBLANES), batch_spec(SUBLANES),
                  _resident((N_KV_HEADS, q_rows, KEY_TILE)), _resident((N_KV_HEADS, q_rows, KV_LANES))],
        out_specs=[batch_spec(q_rows), batch_spec(WINDOW), batch_spec(WINDOW)],
        out_shape=[jax.ShapeDtypeStruct((nb, q_rows, KV_LANES), F32),
                   jax.ShapeDtypeStruct((nb, WINDOW, KV_LANES), F32),
                   jax.ShapeDtypeStruct((nb, WINDOW, KV_LANES), F32)],
        compiler_params=_params(),
        name="attn_sample",
    )(q, kc, vc, kn, vn, bias, sink)


def kernel(x_prompt, x_sample, state_conv, cache_k, cache_v, g_mix, g_ffn, g_final, w_conv_in, conv_w,
           w_conv_out, w_q, w_k, w_v, w_o, sinks, rel_table, w_gate, w_up, w_down):
    batch, seq, _ = x_prompt.shape
    dec_batch, dec_seq, _ = x_sample.shape
    assert batch == 1 and seq % TOKEN_TILE == 0 and TOKEN_TILE % WINDOW == 0
    assert dec_batch % SAMPLE_BATCH_TILE == 0 and dec_seq <= SUBLANES
    assert (dec_batch * dec_seq) % TOKEN_TILE == 0
    assert g_mix.shape[0] == 2, "layer 0 is the conv mixer, layer 1 the attention mixer"

    win = w_conv_in[0].astype(BF16)
    wout = w_conv_out[0].astype(BF16)
    wq = w_q[0].reshape(D_MODEL, N_KV_HEADS, GROUP, HEAD_DIM).transpose(0, 2, 1, 3).reshape(D_MODEL, D_MODEL)
    wqkv = jnp.concatenate([wq, w_k[0], w_v[0]], axis=1).astype(BF16)
    wo = w_o[0].reshape(N_KV_HEADS, GROUP, HEAD_DIM, D_MODEL).transpose(1, 0, 2, 3).reshape(D_MODEL, D_MODEL)
    wo = wo.astype(BF16)
    wg, wu, wd = w_gate.astype(BF16), w_up.astype(BF16), w_down.astype(BF16)
    gm = g_mix.reshape(2, 1, D_MODEL)
    gf = g_ffn.reshape(2, 1, D_MODEL)
    gfin = g_final.reshape(1, D_MODEL)
    cw = conv_w[0]
    sink_vec = sinks[0]

    bucket_p = jnp.asarray(_t5_bucket_np(
        (np.arange(WINDOW)[:, None] + WINDOW) - np.arange(KEY_TILE)[None, :]))
    dist_s = np.arange(dec_seq)[:, None] + WINDOW - np.arange(KEY_TILE)[None, :]
    bucket_s = jnp.asarray(np.tile(_t5_bucket_np(dist_s), (GROUP, 1)))
    bias_p, bias_s, sink_s = _tables(bucket_p, bucket_s, rel_table, sink_vec, dec_seq)

    xp = x_prompt.reshape(seq, D_MODEL)
    xp, tail_p = _conv_prompt(xp, gm[0], win, cw, wout)
    xp = _ffn(xp, gf, wg, wu, wd, gfin, 0, False)
    xp, k_last, v_last = _attn_prompt(xp, gm[1], wqkv, wo, bias_p, sink_vec)
    y_prompt = _ffn(xp, gf, wg, wu, wd, gfin, 1, True).reshape(batch, seq, D_MODEL)
    state_conv_prompt = tail_p[SUBLANES - CONV_STATE:].reshape(1, batch, CONV_STATE, D_MODEL)
    cache_k_prompt = k_last.reshape(1, batch, WINDOW, N_KV_HEADS, HEAD_DIM)
    cache_v_prompt = v_last.reshape(1, batch, WINDOW, N_KV_HEADS, HEAD_DIM)

    n_s = dec_batch * dec_seq
    xs = x_sample.reshape(n_s, D_MODEL)
    prev = jnp.pad(state_conv[0], ((0, 0), (0, dec_seq - CONV_STATE), (0, 0))).reshape(n_s, D_MODEL)
    xs, u_s = _conv_sample(xs, prev, gm[0], win, cw, wout, dec_seq)
    state_conv_sample = u_s.reshape(dec_batch, dec_seq, D_MODEL)[:, dec_seq - CONV_STATE:][None]
    xs = _ffn(xs, gf, wg, wu, wd, gfin, 0, False)
    qkv = _proj(xs, gm[1], wqkv)
    q_s = qkv[:, :D_MODEL].reshape(dec_batch, dec_seq, GROUP, KV_LANES).transpose(0, 2, 1, 3)
    q_s = q_s.reshape(dec_batch, GROUP * dec_seq, KV_LANES)
    pad_new = ((0, 0), (0, SUBLANES - dec_seq), (0, 0))
    k_new = jnp.pad(qkv[:, D_MODEL:D_MODEL + KV_LANES].reshape(dec_batch, dec_seq, KV_LANES), pad_new)
    v_new = jnp.pad(qkv[:, D_MODEL + KV_LANES:].reshape(dec_batch, dec_seq, KV_LANES), pad_new)
    kc = cache_k[0].reshape(dec_batch, WINDOW, KV_LANES)
    vc = cache_v[0].reshape(dec_batch, WINDOW, KV_LANES)
    o_s, k_out, v_out = _attn_sample(q_s, kc, vc, k_new, v_new, bias_s, sink_s, dec_seq)
    o_s = o_s.reshape(dec_batch, GROUP, dec_seq, KV_LANES).transpose(0, 2, 1, 3).reshape(n_s, D_MODEL)
    xs = _oproj(xs, o_s, wo)
    y_sample = _ffn(xs, gf, wg, wu, wd, gfin, 1, True).reshape(dec_batch, dec_seq, D_MODEL)
    cache_k_sample = k_out.reshape(1, dec_batch, WINDOW, N_KV_HEADS, HEAD_DIM)
    cache_v_sample = v_out.reshape(1, dec_batch, WINDOW, N_KV_HEADS, HEAD_DIM)

    return (y_prompt, y_sample, state_conv_prompt, state_conv_sample,
            cache_k_prompt, cache_k_sample, cache_v_prompt, cache_v_sample)
```

```python
import functools
import math

import numpy as np
import jax
import jax.numpy as jnp
from jax import lax
from jax.experimental import pallas as pl
from jax.experimental.pallas import tpu as pltpu

D_MODEL = 1024
D_FF = 2816
HEAD_DIM = 64
N_HEADS = 16
N_KV_HEADS = 2
GROUP = N_HEADS // N_KV_HEADS
WINDOW = 128
N_BUCKETS = 32
MAX_DISTANCE = 128
CONV_STATE = 2
EPS = 1e-5
NEG_INF = -1e30
ATTN_SCALE = 1.0 / math.sqrt(HEAD_DIM)

KV_LANES = N_KV_HEADS * HEAD_DIM
QKV_COLS = D_MODEL + 2 * KV_LANES
SUBLANES = 8
BF16_SUBLANES = 16
KEY_TILE = 2 * WINDOW

TOKEN_TILE = 512
SAMPLE_BATCH_TILE = 16
VMEM_LIMIT_BYTES = 56 * 1024 * 1024

F32 = jnp.float32
BF16 = jnp.bfloat16


def _params(n_axes=1):
    return pltpu.CompilerParams(
        dimension_semantics=("arbitrary",) * n_axes,
        vmem_limit_bytes=VMEM_LIMIT_BYTES)


def _resident(shape):
    zeros = (0,) * len(shape)
    return pl.BlockSpec(shape, lambda *_: zeros, pipeline_mode=pl.Buffered(1))


def _layer(shape, layer):
    index = (layer,) + (0,) * len(shape)
    return pl.BlockSpec((None,) + tuple(shape), lambda *_: index, pipeline_mode=pl.Buffered(1))


def _smem():
    return pl.BlockSpec(memory_space=pltpu.SMEM)


def _rms(x, g):
    return x * lax.rsqrt(jnp.mean(x * x, axis=-1, keepdims=True) + EPS) * g


def _pipeline_specs(n):
    tiles = n // TOKEN_TILE
    nxt = pl.BlockSpec((TOKEN_TILE, D_MODEL), lambda i: (jnp.minimum(i, tiles - 1), 0))
    cur = pl.BlockSpec((TOKEN_TILE, D_MODEL), lambda i: (jnp.maximum(i - 1, 0), 0))
    return tiles, nxt, cur


def _two_stage(first, body):
    step = pl.program_id(0)

    @pl.when(step == 0)
    def _():
        first(0)

    @pl.when(step % 2 == 1)
    def _():
        body(0, 1)

    @pl.when((step % 2 == 0) & (step > 0))
    def _():
        body(1, 0)


def _t5_bucket_np(dist):
    n = np.maximum(dist, 0)
    max_exact = N_BUCKETS // 2
    nf = np.maximum(n, 1).astype(np.float32)
    large = max_exact + (np.log(nf / np.float32(max_exact)) / np.float32(math.log(MAX_DISTANCE / max_exact))
                         * np.float32(N_BUCKETS - max_exact)).astype(np.int32)
    large = np.minimum(large, N_BUCKETS - 1)
    return np.where(n < max_exact, n, large).astype(np.int32)


def _ffn_kernel(x_ref, g_ref, wg_ref, wu_ref, wd_ref, gf_ref, o_ref, *, final_norm):
    x = x_ref[...]
    h = _rms(x, g_ref[...]).astype(BF16)
    gate = jnp.dot(h, wg_ref[...], preferred_element_type=F32)
    up = jnp.dot(h, wu_ref[...], preferred_element_type=F32)
    act = (gate * jax.nn.sigmoid(gate) * up).astype(BF16)
    y = x + jnp.dot(act, wd_ref[...], preferred_element_type=F32)
    if final_norm:
        y = _rms(y, gf_ref[...])
    o_ref[...] = y


def _ffn(x, g, wg, wu, wd, g_final, layer, final_norm):
    n = x.shape[0]
    tile = pl.BlockSpec((TOKEN_TILE, D_MODEL), lambda i: (i, 0))
    return pl.pallas_call(
        functools.partial(_ffn_kernel, final_norm=final_norm),
        grid=(n // TOKEN_TILE,),
        in_specs=[tile, _layer((1, D_MODEL), layer), _layer((D_MODEL, D_FF), layer),
                  _layer((D_MODEL, D_FF), layer), _layer((D_FF, D_MODEL), layer),
                  _resident((1, D_MODEL))],
        out_specs=tile,
        out_shape=jax.ShapeDtypeStruct((n, D_MODEL), F32),
        compiler_params=_params(),
        name="ffn_final" if final_norm else "ffn",
    )(x, g, wg, wu, wd, g_final)


def _conv_gates(h, win_ref):
    cx = jnp.dot(h, win_ref[:, D_MODEL:], preferred_element_type=F32)
    u = cx[:, :D_MODEL] * cx[:, D_MODEL:]
    b = jnp.dot(h, win_ref[:, :D_MODEL], preferred_element_type=F32)
    return b, u


def _conv_prompt_kernel(x_ref, g_ref, win_ref, cw_ref, wout_ref, wg_ref, wu_ref, wd_ref,
                        o_ref, tail_out_ref, wg_out_ref, wu_out_ref, wd_out_ref, tail_ref):
    wg_out_ref[...] = wg_ref[...].astype(BF16)
    wu_out_ref[...] = wu_ref[...].astype(BF16)
    wd_out_ref[...] = wd_ref[...].astype(BF16)
    tail = jnp.where(pl.program_id(0) > 0, tail_ref[...], 0.0)
    x = x_ref[...]
    b, u = _conv_gates(_rms(x, g_ref[...]), win_ref)
    rows = u.shape[0]
    r8 = lax.broadcasted_iota(jnp.int32, (SUBLANES, D_MODEL), 0)
    u1 = pltpu.roll(u, 1, 0)
    u2 = pltpu.roll(u, 2, 0)
    u1 = jnp.concatenate([jnp.where(r8 < 1, pltpu.roll(tail, 1, 0), u1[:SUBLANES]), u1[SUBLANES:]], axis=0)
    u2 = jnp.concatenate([jnp.where(r8 < 2, pltpu.roll(tail, 2, 0), u2[:SUBLANES]), u2[SUBLANES:]], axis=0)
    cw = cw_ref[...]
    v = cw[0:1] * u2 + cw[1:2] * u1 + cw[2:3] * u
    o_ref[...] = x + jnp.dot(b * v, wout_ref[...], preferred_element_type=F32)
    new_tail = u[rows - SUBLANES:]
    tail_ref[...] = new_tail
    tail_out_ref[...] = new_tail


def _conv_prompt(x, g, win, cw, wout, w_gate, w_up, w_down):
    n = x.shape[0]
    tiles = n // TOKEN_TILE
    tile = pl.BlockSpec((TOKEN_TILE, D_MODEL), lambda i: (i, 0))

    def slabs(w):
        flat = w.reshape(-1, w.shape[-1])
        rows = flat.shape[0] // tiles
        assert rows * tiles == flat.shape[0] and rows % BF16_SUBLANES == 0
        return flat, pl.BlockSpec((rows, flat.shape[1]), lambda i: (i, 0))

    (wg, wg_spec), (wu, wu_spec), (wd, wd_spec) = slabs(w_gate), slabs(w_up), slabs(w_down)
    out = pl.pallas_call(
        _conv_prompt_kernel,
        grid=(tiles,),
        in_specs=[tile, _resident((1, D_MODEL)), _resident((D_MODEL, 3 * D_MODEL)),
                  _resident((3, D_MODEL)), _resident((D_MODEL, D_MODEL)), wg_spec, wu_spec, wd_spec],
        out_specs=[tile, pl.BlockSpec((SUBLANES, D_MODEL), lambda i: (0, 0)), wg_spec, wu_spec, wd_spec],
        out_shape=[jax.ShapeDtypeStruct((n, D_MODEL), F32),
                   jax.ShapeDtypeStruct((SUBLANES, D_MODEL), F32),
                   jax.ShapeDtypeStruct(wg.shape, BF16),
                   jax.ShapeDtypeStruct(wu.shape, BF16),
                   jax.ShapeDtypeStruct(wd.shape, BF16)],
        scratch_shapes=[pltpu.VMEM((SUBLANES, D_MODEL), F32)],
        compiler_params=_params(),
        name="conv_prompt",
    )(x, g, win, cw, wout, wg, wu, wd)
    y, tail, wg_b, wu_b, wd_b = out
    return y, tail, wg_b.reshape(w_gate.shape), wu_b.reshape(w_up.shape), wd_b.reshape(w_down.shape)


def _conv_sample_kernel(x_ref, prev_ref, g_ref, win_ref, cw_ref, wout_ref, o_ref, u_ref, *, steps):
    x = x_ref[...]
    b, u = _conv_gates(_rms(x, g_ref[...]), win_ref)
    rows = u.shape[0]
    prev = prev_ref[...]
    t = lax.broadcasted_iota(jnp.int32, u.shape, 0) % steps
    u1 = jnp.where(t >= 1, pltpu.roll(u, 1, 0), pltpu.roll(prev, rows - 1, 0))
    u2 = jnp.where(t >= 2, pltpu.roll(u, 2, 0), prev)
    cw = cw_ref[...]
    v = cw[0:1] * u2 + cw[1:2] * u1 + cw[2:3] * u
    o_ref[...] = x + jnp.dot(b * v, wout_ref[...], preferred_element_type=F32)
    u_ref[...] = u


def _conv_sample(x, prev, g, win, cw, wout, steps):
    n = x.shape[0]
    full = _resident((n, D_MODEL))
    return pl.pallas_call(
        functools.partial(_conv_sample_kernel, steps=steps),
        grid=(1,),
        in_specs=[full, full, _resident((1, D_MODEL)), _resident((D_MODEL, 3 * D_MODEL)),
                  _resident((3, D_MODEL)), _resident((D_MODEL, D_MODEL))],
        out_specs=[pl.BlockSpec((n, D_MODEL), lambda i: (0, 0))] * 2,
        out_shape=[jax.ShapeDtypeStruct((n, D_MODEL), F32)] * 2,
        compiler_params=_params(),
        name="conv_sample",
    )(x, prev, g, win, cw, wout)


def _lookup_bias(bucket, table_ref, head):
    acc = jnp.zeros(bucket.shape, F32)
    for b in range(N_BUCKETS):
        acc = jnp.where(bucket == b, table_ref[b, head], acc)
    return acc


def _tables_kernel(bucket_p_ref, bucket_s_ref, table_ref, sinks_ref, bias_p_ref, bias_s_ref, sink_s_ref, *, steps):
    bucket_p = bucket_p_ref[...]
    for head in range(N_HEADS):
        bias_p_ref[head] = _lookup_bias(bucket_p, table_ref, head)
    bucket_s = bucket_s_ref[...]
    grp = lax.broadcasted_iota(jnp.int32, bucket_s.shape, 0) // steps
    grp_l = lax.broadcasted_iota(jnp.int32, (bucket_s.shape[0], KV_LANES), 0) // steps
    for kv in range(N_KV_HEADS):
        bias = jnp.zeros(bucket_s.shape, F32)
        sink = jnp.zeros((bucket_s.shape[0], KV_LANES), F32)
        for gi in range(GROUP):
            head = kv * GROUP + gi
            bias = jnp.where(grp == gi, _lookup_bias(bucket_s, table_ref, head), bias)
            sink = jnp.where(grp_l == gi, sinks_ref[head], sink)
        bias_s_ref[kv] = bias
        sink_s_ref[kv] = sink


def _tables(bucket_p, bucket_s, table, sinks, steps):
    q_rows = bucket_s.shape[0]
    return pl.pallas_call(
        functools.partial(_tables_kernel, steps=steps),
        grid=(1,),
        in_specs=[_resident((WINDOW, KEY_TILE)), _resident((q_rows, KEY_TILE)), _smem(), _smem()],
        out_specs=[pl.BlockSpec((N_HEADS, WINDOW, KEY_TILE), lambda i: (0, 0, 0)),
                   pl.BlockSpec((N_KV_HEADS, q_rows, KEY_TILE), lambda i: (0, 0, 0)),
                   pl.BlockSpec((N_KV_HEADS, q_rows, KV_LANES), lambda i: (0, 0, 0))],
        out_shape=[jax.ShapeDtypeStruct((N_HEADS, WINDOW, KEY_TILE), F32),
                   jax.ShapeDtypeStruct((N_KV_HEADS, q_rows, KEY_TILE), F32),
                   jax.ShapeDtypeStruct((N_KV_HEADS, q_rows, KV_LANES), F32)],
        compiler_params=_params(),
        name="bias_tables",
    )(bucket_p, bucket_s, table, sinks)


def _attn_prompt_kernel(xn_ref, xc_ref, g_ref, wqkv_ref, wo_ref, bias_ref, sinks_ref,
                        o_ref, klast_ref, vlast_ref, qbuf, kbuf, vbuf, kcar, vcar, obuf):
    step = pl.program_id(0)
    rows = xn_ref.shape[0]
    n_blk = rows // WINDOW
    q_cols = D_MODEL // n_blk

    def norm_next():
        return _rms(xn_ref[...], g_ref[...]).astype(BF16)

    def project_chunk(h, slot, c):
        c0 = c * q_cols
        part = jnp.dot(h, wqkv_ref[:, c0:c0 + q_cols], preferred_element_type=F32)
        qbuf[slot, :, c0:c0 + q_cols] = (part * ATTN_SCALE).astype(BF16)
        if c < n_blk - 1:
            return
        kv_part = jnp.dot(h, wqkv_ref[:, D_MODEL:], preferred_element_type=F32)
        k = kv_part[:, :KV_LANES]
        v = kv_part[:, KV_LANES:]
        klast_ref[...] = k[rows - WINDOW:]
        vlast_ref[...] = v[rows - WINDOW:]
        lane = lax.broadcasted_iota(jnp.int32, (rows, KV_LANES), 1)
        for kv, half in enumerate((lane < HEAD_DIM, lane >= HEAD_DIM)):
            k_half = jnp.where(half, k, 0.0)
            v_half = jnp.where(half, v, 0.0)
            kbuf[slot, kv] = k_half.astype(BF16)
            vbuf[slot, kv] = v_half.astype(BF16)
            kcar[slot, kv] = k_half[rows - WINDOW:]
            vcar[slot, kv] = v_half[rows - WINDOW:]

    def first(slot):
        kcar[...] = jnp.zeros(kcar.shape, F32)
        vcar[...] = jnp.zeros(vcar.shape, F32)
        h = norm_next()
        for c in range(n_blk):
            project_chunk(h, slot, c)

    def body(cur, nxt):
        k_prev = [jnp.where(step > 1, kcar[nxt, kv], 0.0).astype(BF16) for kv in range(N_KV_HEADS)]
        v_prev = [jnp.where(step > 1, vcar[nxt, kv], 0.0).astype(BF16) for kv in range(N_KV_HEADS)]
        h = norm_next()

        qi = lax.broadcasted_iota(jnp.int32, (WINDOW, KEY_TILE), 0)
        kj = lax.broadcasted_iota(jnp.int32, (WINDOW, KEY_TILE), 1)
        band = (kj >= qi) & (kj <= qi + WINDOW)
        lane_q = lax.broadcasted_iota(jnp.int32, (WINDOW, KV_LANES), 1)
        lane_k = lax.broadcasted_iota(jnp.int32, (KEY_TILE, KV_LANES), 1)
        ones_half = [(lane_k < HEAD_DIM).astype(F32).astype(BF16), (lane_k >= HEAD_DIM).astype(F32).astype(BF16)]

        def block_scores(blk):
            r0 = blk * WINDOW
            qs = jnp.concatenate(
                [qbuf[cur, r0:r0 + WINDOW, gi * KV_LANES:(gi + 1) * KV_LANES] for gi in range(GROUP)], axis=0)
            if blk == 0:
                keys = [jnp.concatenate([k_prev[kv], kbuf[cur, kv, 0:WINDOW, :]], axis=0) for kv in range(N_KV_HEADS)]
            else:
                keys = [kbuf[cur, kv, r0 - WINDOW:r0 + WINDOW, :] for kv in range(N_KV_HEADS)]
            return [lax.dot_general(qs, keys[kv], (((1,), (1,)), ((), ())), preferred_element_type=F32)
                    for kv in range(N_KV_HEADS)]

        scores = block_scores(0)
        for blk in range(n_blk):
            r0 = blk * WINDOW
            if blk == 0:
                mask = band & (kj >= jnp.where(step > 1, 0, WINDOW))
                vals = [jnp.concatenate([v_prev[kv], vbuf[cur, kv, 0:WINDOW, :]], axis=0) for kv in range(N_KV_HEADS)]
            else:
                mask = band
                vals = [vbuf[cur, kv, r0 - WINDOW:r0 + WINDOW, :] for kv in range(N_KV_HEADS)]
            next_scores = block_scores(blk + 1) if blk + 1 < n_blk else None
            project_chunk(h, nxt, blk)
            probs, maxes = [], []
            for kv in range(N_KV_HEADS):
                s_all = scores[kv]
                p_kv, m_kv = [], []
                for gi in range(GROUP):
                    head = kv * GROUP + gi
                    s = s_all[gi * WINDOW:(gi + 1) * WINDOW]
                    s = jnp.where(mask, s + bias_ref[head], NEG_INF)
                    m = jnp.maximum(jnp.max(s, axis=-1, keepdims=True), sinks_ref[head])
                    p_kv.append(jnp.exp(s - m).astype(BF16))
                    m_kv.append(m)
                probs.append(jnp.concatenate(p_kv, axis=0))
                maxes.append(m_kv)
            scores = next_scores
            acc = None
            for kv in range(N_KV_HEADS):
                rhs = jnp.concatenate([vals[kv], ones_half[kv]], axis=1)
                part = jnp.dot(probs[kv], rhs, preferred_element_type=F32)
                acc = part if acc is None else acc + part
            outs = []
            for gi in range(GROUP):
                a = acc[gi * WINDOW:(gi + 1) * WINDOW]
                sink_term = jnp.where(lane_q < HEAD_DIM,
                                      jnp.exp(sinks_ref[gi] - maxes[0][gi]),
                                      jnp.exp(sinks_ref[GROUP + gi] - maxes[1][gi]))
                outs.append(a[:, :KV_LANES] / (a[:, KV_LANES:] + sink_term))
            obuf[r0:r0 + WINDOW, :] = jnp.concatenate(outs, axis=1).astype(BF16)

        o_ref[...] = xc_ref[...] + jnp.dot(obuf[...], wo_ref[...], preferred_element_type=F32)

    _two_stage(first, body)


def _attn_prompt(x, g, wqkv, wo, bias, sinks):
    n = x.shape[0]
    tiles, nxt, cur = _pipeline_specs(n)
    last = pl.BlockSpec((WINDOW, KV_LANES), lambda i: (0, 0))
    return pl.pallas_call(
        _attn_prompt_kernel,
        grid=(tiles + 1,),
        in_specs=[nxt, cur, _resident((1, D_MODEL)), _resident((D_MODEL, QKV_COLS)),
                  _resident((D_MODEL, D_MODEL)), _resident((N_HEADS, WINDOW, KEY_TILE)), _smem()],
        out_specs=[cur, last, last],
        out_shape=[jax.ShapeDtypeStruct((n, D_MODEL), F32),
                   jax.ShapeDtypeStruct((WINDOW, KV_LANES), F32),
                   jax.ShapeDtypeStruct((WINDOW, KV_LANES), F32)],
        scratch_shapes=[pltpu.VMEM((2, TOKEN_TILE, D_MODEL), BF16),
                        pltpu.VMEM((2, N_KV_HEADS, TOKEN_TILE, KV_LANES), BF16),
                        pltpu.VMEM((2, N_KV_HEADS, TOKEN_TILE, KV_LANES), BF16),
                        pltpu.VMEM((2, N_KV_HEADS, WINDOW, KV_LANES), F32),
                        pltpu.VMEM((2, N_KV_HEADS, WINDOW, KV_LANES), F32),
                        pltpu.VMEM((TOKEN_TILE, D_MODEL), BF16)],
        compiler_params=_params(),
        name="attn_prompt",
    )(x, x, g, wqkv, wo, bias, sinks)


def _proj_kernel(x_ref, g_ref, w_ref, o_ref):
    h = _rms(x_ref[...], g_ref[...]).astype(BF16)
    o_ref[...] = jnp.dot(h, w_ref[...], preferred_element_type=F32)


def _proj(x, g, w):
    n, cols = x.shape[0], w.shape[1]
    return pl.pallas_call(
        _proj_kernel,
        grid=(1,),
        in_specs=[_resident((n, D_MODEL)), _resident((1, D_MODEL)), _resident((D_MODEL, cols))],
        out_specs=pl.BlockSpec((n, cols), lambda i: (0, 0)),
        out_shape=jax.ShapeDtypeStruct((n, cols), F32),
        compiler_params=_params(),
        name="proj_sample",
    )(x, g, w)


def _oproj_kernel(x_ref, a_ref, w_ref, o_ref):
    o_ref[...] = x_ref[...] + jnp.dot(a_ref[...].astype(BF16), w_ref[...], preferred_element_type=F32)


def _oproj(x, a, w):
    n = x.shape[0]
    return pl.pallas_call(
        _oproj_kernel,
        grid=(1,),
        in_specs=[_resident((n, D_MODEL)), _resident((n, D_MODEL)), _resident((D_MODEL, D_MODEL))],
        out_specs=pl.BlockSpec((n, D_MODEL), lambda i: (0, 0)),
        out_shape=jax.ShapeDtypeStruct((n, D_MODEL), F32),
        compiler_params=_params(),
        name="oproj_sample",
    )(x, a, w)


def _attn_sample_kernel(q_ref, kc_ref, vc_ref, kn_ref, vn_ref, bias_ref, sink_ref,
                        o_ref, kout_ref, vout_ref, *, steps):
    q_rows = GROUP * steps

    qi = lax.broadcasted_iota(jnp.int32, (q_rows, KEY_TILE), 0) % steps
    kj = lax.broadcasted_iota(jnp.int32, (q_rows, KEY_TILE), 1)
    mask = (kj >= qi) & (kj <= qi + WINDOW)
    lane = lax.broadcasted_iota(jnp.int32, (KEY_TILE, KV_LANES), 1)
    lane_q = lax.broadcasted_iota(jnp.int32, (q_rows, KV_LANES), 1)
    kv_lane = (lane < HEAD_DIM, lane >= HEAD_DIM)
    pad = jnp.zeros((KEY_TILE - WINDOW - SUBLANES, KV_LANES), F32)
    r8 = lax.broadcasted_iota(jnp.int32, (SUBLANES, KV_LANES), 0)

    def shifted_cache(cache, new):
        body = pltpu.roll(cache, WINDOW - steps, 0)
        tail = jnp.where(r8 >= SUBLANES - steps, pltpu.roll(new, SUBLANES - steps, 0),
                         body[WINDOW - SUBLANES:])
        return jnp.concatenate([body[:WINDOW - SUBLANES], tail], axis=0)

    batches = range(q_ref.shape[0])
    scores = []
    for n in batches:
        kc, kn = kc_ref[n], kn_ref[n]
        kout_ref[n] = shifted_cache(kc, kn)
        keys = jnp.concatenate([kc, kn, pad], axis=0)
        q = (q_ref[n] * ATTN_SCALE).astype(BF16)
        scores.append([lax.dot_general(q, jnp.where(kv_lane[kv], keys, 0.0).astype(BF16),
                                       (((1,), (1,)), ((), ())), preferred_element_type=F32)
                       for kv in range(N_KV_HEADS)])
    probs, sink_terms = [], []
    for n in batches:
        p_n, t_n = [], []
        for kv in range(N_KV_HEADS):
            s = jnp.where(mask, scores[n][kv] + bias_ref[kv], NEG_INF)
            sink = sink_ref[kv][:, 0:1]
            m = jnp.maximum(jnp.max(s, axis=-1, keepdims=True), sink)
            p_n.append(jnp.exp(s - m).astype(BF16))
            t_n.append(jnp.exp(sink - m))
        probs.append(p_n)
        sink_terms.append(t_n)
    for n in batches:
        vc, vn = vc_ref[n], vn_ref[n]
        vout_ref[n] = shifted_cache(vc, vn)
        vals = jnp.concatenate([vc, vn, pad], axis=0)
        acc = None
        for kv in range(N_KV_HEADS):
            rhs = jnp.concatenate([jnp.where(kv_lane[kv], vals, 0.0), jnp.where(kv_lane[kv], 1.0, 0.0)], axis=1)
            part = jnp.dot(probs[n][kv], rhs.astype(BF16), preferred_element_type=F32)
            acc = part if acc is None else acc + part
        denom = acc[:, KV_LANES:] + jnp.where(lane_q < HEAD_DIM, sink_terms[n][0], sink_terms[n][1])
        o_ref[n] = acc[:, :KV_LANES] / denom


def _attn_sample(q, kc, vc, kn, vn, bias, sink, steps):
    nb = q.shape[0]
    q_rows = GROUP * steps
    bt = SAMPLE_BATCH_TILE

    def batch_spec(r):
        return pl.BlockSpec((bt, r, KV_LANES), lambda i: (i, 0, 0))

    return pl.pallas_call(
        functools.partial(_attn_sample_kernel, steps=steps),
        grid=(nb // bt,),
        in_specs=[batch_spec(q_rows), batch_spec(WINDOW), batch_spec(WINDOW),
                  batch_spec(SUBLANES), batch_spec(SUBLANES),
                  _resident((N_KV_HEADS, q_rows, KEY_TILE)), _resident((N_KV_HEADS, q_rows, KV_LANES))],
        out_specs=[batch_spec(q_rows), batch_spec(WINDOW), batch_spec(WINDOW)],
        out_shape=[jax.ShapeDtypeStruct((nb, q_rows, KV_LANES), F32),
                   jax.ShapeDtypeStruct((nb, WINDOW, KV_LANES), F32),
                   jax.ShapeDtypeStruct((nb, WINDOW, KV_LANES), F32)],
        compiler_params=_params(),
        name="attn_sample",
    )(q, kc, vc, kn, vn, bias, sink)


def kernel(x_prompt, x_sample, state_conv, cache_k, cache_v, g_mix, g_ffn, g_final, w_conv_in, conv_w,
           w_conv_out, w_q, w_k, w_v, w_o, sinks, rel_table, w_gate, w_up, w_down):
    batch, seq, _ = x_prompt.shape
    dec_batch, dec_seq, _ = x_sample.shape
    assert batch == 1 and seq % TOKEN_TILE == 0 and TOKEN_TILE % WINDOW == 0
    assert dec_batch % SAMPLE_BATCH_TILE == 0 and dec_seq <= SUBLANES
    assert (dec_batch * dec_seq) % TOKEN_TILE == 0
    assert g_mix.shape[0] == 2, "layer 0 is the conv mixer, layer 1 the attention mixer"

    win = w_conv_in[0]
    wout = w_conv_out[0]
    wq =w_q[0].reshape(D_MODEL, N_KV_HEADS, GROUP, HEAD_DIM).transpose(0, 2, 1, 3).reshape(D_MODEL, D_MODEL)
    wqkv = jnp.concatenate([wq, w_k[0], w_v[0]], axis=1).astype(BF16)
    wo = w_o[0].reshape(N_KV_HEADS, GROUP, HEAD_DIM, D_MODEL).transpose(1, 0, 2, 3).reshape(D_MODEL, D_MODEL)
    wo = wo.astype(BF16)
    gm =g_mix.reshape(2, 1, D_MODEL)
    gf = g_ffn.reshape(2, 1, D_MODEL)
    gfin = g_final.reshape(1, D_MODEL)
    cw = conv_w[0]
    sink_vec = sinks[0]

    bucket_p = jnp.asarray(_t5_bucket_np(
        (np.arange(WINDOW)[:, None] + WINDOW) - np.arange(KEY_TILE)[None, :]))
    dist_s = np.arange(dec_seq)[:, None] + WINDOW - np.arange(KEY_TILE)[None, :]
    bucket_s = jnp.asarray(np.tile(_t5_bucket_np(dist_s), (GROUP, 1)))
    bias_p, bias_s, sink_s = _tables(bucket_p, bucket_s, rel_table, sink_vec, dec_seq)

    xp = x_prompt.reshape(seq, D_MODEL)
    xp, tail_p, wg, wu, wd = _conv_prompt(xp, gm[0], win, cw, wout, w_gate, w_up, w_down)
    xp = _ffn(xp, gf, wg, wu, wd, gfin, 0, False)
    xp, k_last, v_last = _attn_prompt(xp, gm[1], wqkv, wo, bias_p, sink_vec)
    y_prompt = _ffn(xp, gf, wg, wu, wd, gfin, 1, True).reshape(batch, seq, D_MODEL)
    state_conv_prompt = tail_p[SUBLANES - CONV_STATE:].reshape(1, batch, CONV_STATE, D_MODEL)
    cache_k_prompt = k_last.reshape(1, batch, WINDOW, N_KV_HEADS, HEAD_DIM)
    cache_v_prompt = v_last.reshape(1, batch, WINDOW, N_KV_HEADS, HEAD_DIM)

    n_s = dec_batch * dec_seq
    xs = x_sample.reshape(n_s, D_MODEL)
    prev = jnp.pad(state_conv[0], ((0, 0), (0, dec_seq - CONV_STATE), (0, 0))).reshape(n_s, D_MODEL)
    xs, u_s = _conv_sample(xs, prev, gm[0], win, cw, wout, dec_seq)
    state_conv_sample = u_s.reshape(dec_batch, dec_seq, D_MODEL)[:, dec_seq - CONV_STATE:][None]
    xs = _ffn(xs, gf, wg, wu, wd, gfin, 0, False)
    qkv = _proj(xs, gm[1], wqkv)
    q_s = qkv[:, :D_MODEL].reshape(dec_batch, dec_seq, GROUP, KV_LANES).transpose(0, 2, 1, 3)
    q_s = q_s.reshape(dec_batch, GROUP * dec_seq, KV_LANES)
    pad_new = ((0, 0), (0, SUBLANES - dec_seq), (0, 0))
    k_new = jnp.pad(qkv[:, D_MODEL:D_MODEL + KV_LANES].reshape(dec_batch, dec_seq, KV_LANES), pad_new)
    v_new = jnp.pad(qkv[:, D_MODEL + KV_LANES:].reshape(dec_batch, dec_seq, KV_LANES), pad_new)
    kc = cache_k[0].reshape(dec_batch, WINDOW, KV_LANES)
    vc = cache_v[0].reshape(dec_batch, WINDOW, KV_LANES)
    o_s, k_out, v_out = _attn_sample(q_s, kc, vc, k_new, v_new, bias_s, sink_s, dec_seq)
    o_s = o_s.reshape(dec_batch, GROUP, dec_seq, KV_LANES).transpose(0, 2, 1, 3).reshape(n_s, D_MODEL)
    xs = _oproj(xs, o_s, wo)
    y_sample = _ffn(xs, gf, wg, wu, wd, gfin, 1, True).reshape(dec_batch, dec_seq, D_MODEL)
    cache_k_sample = k_out.reshape(1, dec_batch, WINDOW, N_KV_HEADS, HEAD_DIM)
    cache_v_sample = v_out.reshape(1, dec_batch, WINDOW, N_KV_HEADS, HEAD_DIM)

    return (y_prompt, y_sample, state_conv_prompt, state_conv_sample,
            cache_k_prompt, cache_k_sample, cache_v_prompt, cache_v_sample)
```

```python
import functools
import math

import numpy as np
import jax
import jax.numpy as jnp
from jax import lax
from jax.experimental import pallas as pl
from jax.experimental.pallas import tpu as pltpu

D_MODEL = 1024
D_FF = 2816
HEAD_DIM = 64
N_HEADS = 16
N_KV_HEADS = 2
GROUP = N_HEADS // N_KV_HEADS
WINDOW = 128
N_BUCKETS = 32
MAX_DISTANCE = 128
CONV_STATE = 2
EPS = 1e-5
NEG_INF = -1e30
ATTN_SCALE = 1.0 / math.sqrt(HEAD_DIM)

KV_LANES = N_KV_HEADS * HEAD_DIM
QKV_COLS = D_MODEL + 2 * KV_LANES
SUBLANES = 8
BF16_SUBLANES = 16
KEY_TILE = 2 * WINDOW

TOKEN_TILE = 512
ATTN_TILE = 1024
MXU_WIDTH = 256
FFN_SPLIT = 2
CONV_SPLIT = 2
SAMPLE_BATCH_TILE = 16
VMEM_LIMIT_BYTES = 56 * 1024 * 1024

F32 = jnp.float32
BF16 = jnp.bfloat16


def _params(n_axes=1):
    return pltpu.CompilerParams(
        dimension_semantics=("arbitrary",) * n_axes,
        vmem_limit_bytes=VMEM_LIMIT_BYTES)


def _resident(shape):
    zeros = (0,) * len(shape)
    return pl.BlockSpec(shape, lambda *_: zeros, pipeline_mode=pl.Buffered(1))


def _layer(shape, layer):
    index = (layer,) + (0,) * len(shape)
    return pl.BlockSpec((None,) + tuple(shape), lambda *_: index, pipeline_mode=pl.Buffered(1))


def _smem():
    return pl.BlockSpec(memory_space=pltpu.SMEM)


def _rms(x, g):
    return x * lax.rsqrt(jnp.mean(x * x, axis=-1, keepdims=True) + EPS) * g


def _pipeline_specs(n, rows):
    tiles = n // rows
    nxt = pl.BlockSpec((rows, D_MODEL), lambda i: (jnp.minimum(i, tiles - 1), 0))
    cur = pl.BlockSpec((rows, D_MODEL), lambda i: (jnp.maximum(i - 1, 0), 0))
    return tiles, nxt, cur


def _two_stage(first, body):
    step = pl.program_id(0)

    @pl.when(step == 0)
    def _():
        first(0)

    @pl.when(step % 2 == 1)
    def _():
        body(0, 1)

    @pl.when((step % 2 == 0) & (step > 0))
    def _():
        body(1, 0)


def _t5_bucket_np(dist):
    n = np.maximum(dist, 0)
    max_exact = N_BUCKETS // 2
    nf = np.maximum(n, 1).astype(np.float32)
    large = max_exact + (np.log(nf / np.float32(max_exact)) / np.float32(math.log(MAX_DISTANCE / max_exact))
                         * np.float32(N_BUCKETS - max_exact)).astype(np.int32)
    large = np.minimum(large, N_BUCKETS - 1)
    return np.where(n < max_exact, n, large).astype(np.int32)


def _ffn_kernel(x_ref, g_ref, wg_ref, wu_ref, wd_ref, gf_ref, o_ref, *, final_norm):
    rows = x_ref.shape[0] // FFN_SPLIT
    halves = [pl.ds(i * rows, rows) for i in range(FFN_SPLIT)]
    acts = []
    for r in halves:
        h = _rms(x_ref[r, :], g_ref[...]).astype(BF16)
        gate = jnp.dot(h, wg_ref[...], preferred_element_type=F32)
        up = jnp.dot(h, wu_ref[...], preferred_element_type=F32)
        acts.append((gate * jax.nn.sigmoid(gate) * up).astype(BF16))
    for r, act in zip(halves, acts):
        y = x_ref[r, :] + jnp.dot(act, wd_ref[...], preferred_element_type=F32)
        if final_norm:
            y = _rms(y, gf_ref[...])
        o_ref[r, :] = y


def _ffn(x, g, wg, wu, wd, g_final, layer, final_norm):
    n = x.shape[0]
    tile = pl.BlockSpec((TOKEN_TILE, D_MODEL), lambda i: (i, 0))
    return pl.pallas_call(
        functools.partial(_ffn_kernel, final_norm=final_norm),
        grid=(n // TOKEN_TILE,),
        in_specs=[tile, _layer((1, D_MODEL), layer), _layer((D_MODEL, D_FF), layer),
                  _layer((D_MODEL, D_FF), layer), _layer((D_FF, D_MODEL), layer),
                  _resident((1, D_MODEL))],
        out_specs=tile,
        out_shape=jax.ShapeDtypeStruct((n, D_MODEL), F32),
        compiler_params=_params(),
        name="ffn_final" if final_norm else "ffn",
    )(x, g, wg, wu, wd, g_final)


def _conv_gates(h, win_ref):
    cx = jnp.dot(h, win_ref[:, D_MODEL:], preferred_element_type=F32)
    u = cx[:, :D_MODEL] * cx[:, D_MODEL:]
    b = jnp.dot(h, win_ref[:, :D_MODEL], preferred_element_type=F32)
    return b, u


def _conv_prompt_kernel(x_ref, g_ref, win_ref, cw_ref, wout_ref, wg_ref, wu_ref, wd_ref,
                        o_ref, tail_out_ref, wg_out_ref, wu_out_ref, wd_out_ref, tail_ref):
    wg_out_ref[...] = wg_ref[...].astype(BF16)
    wu_out_ref[...] = wu_ref[...].astype(BF16)
    wd_out_ref[...] = wd_ref[...].astype(BF16)
    tail = jnp.where(pl.program_id(0) > 0, tail_ref[...], 0.0)
    rows = x_ref.shape[0] // CONV_SPLIT
    groups = [pl.ds(i * rows, rows) for i in range(CONV_SPLIT)]
    gates = [_conv_gates(_rms(x_ref[r, :], g_ref[...]), win_ref) for r in groups]
    cw = cw_ref[...]
    r8 = lax.broadcasted_iota(jnp.int32, (SUBLANES, D_MODEL), 0)
    for r, (b, u) in zip(groups, gates):
        u1 = pltpu.roll(u, 1, 0)
        u2 = pltpu.roll(u, 2, 0)
        u1 = jnp.concatenate([jnp.where(r8 < 1, pltpu.roll(tail, 1, 0), u1[:SUBLANES]), u1[SUBLANES:]], axis=0)
        u2 = jnp.concatenate([jnp.where(r8 < 2, pltpu.roll(tail, 2, 0), u2[:SUBLANES]), u2[SUBLANES:]], axis=0)
        v = cw[0:1] * u2 + cw[1:2] * u1 + cw[2:3] * u
        o_ref[r, :] = x_ref[r, :] + jnp.dot(b * v, wout_ref[...], preferred_element_type=F32)
        tail = u[rows - SUBLANES:]
    tail_ref[...] = tail
    tail_out_ref[...] = tail


def _conv_prompt(x, g, win, cw, wout, w_gate, w_up, w_down):
    n = x.shape[0]
    tiles = n // TOKEN_TILE
    tile = pl.BlockSpec((TOKEN_TILE, D_MODEL), lambda i: (i, 0))

    def slabs(w):
        flat = w.reshape(-1, w.shape[-1])
        rows = flat.shape[0] // tiles
        assert rows * tiles == flat.shape[0] and rows % BF16_SUBLANES == 0
        return flat, pl.BlockSpec((rows, flat.shape[1]), lambda i: (i, 0))

    (wg, wg_spec), (wu, wu_spec), (wd, wd_spec) = slabs(w_gate), slabs(w_up), slabs(w_down)
    out = pl.pallas_call(
        _conv_prompt_kernel,
        grid=(tiles,),
        in_specs=[tile, _resident((1, D_MODEL)), _resident((D_MODEL, 3 * D_MODEL)),
                  _resident((3, D_MODEL)), _resident((D_MODEL, D_MODEL)), wg_spec, wu_spec, wd_spec],
        out_specs=[tile, pl.BlockSpec((SUBLANES, D_MODEL), lambda i: (0, 0)), wg_spec, wu_spec, wd_spec],
        out_shape=[jax.ShapeDtypeStruct((n, D_MODEL), F32),
                   jax.ShapeDtypeStruct((SUBLANES, D_MODEL), F32),
                   jax.ShapeDtypeStruct(wg.shape, BF16),
                   jax.ShapeDtypeStruct(wu.shape, BF16),
                   jax.ShapeDtypeStruct(wd.shape, BF16)],
        scratch_shapes=[pltpu.VMEM((SUBLANES, D_MODEL), F32)],
        compiler_params=_params(),
        name="conv_prompt",
    )(x, g, win, cw, wout, wg, wu, wd)
    y, tail, wg_b, wu_b, wd_b = out
    return y, tail, wg_b.reshape(w_gate.shape), wu_b.reshape(w_up.shape), wd_b.reshape(w_down.shape)


def _conv_sample_kernel(x_ref, prev_ref, g_ref, win_ref, cw_ref, wout_ref, o_ref, u_ref, *, steps):
    x = x_ref[...]
    b, u = _conv_gates(_rms(x, g_ref[...]), win_ref)
    rows = u.shape[0]
    prev = prev_ref[...]
    t = lax.broadcasted_iota(jnp.int32, u.shape, 0) % steps
    u1 = jnp.where(t >= 1, pltpu.roll(u, 1, 0), pltpu.roll(prev, rows - 1, 0))
    u2 = jnp.where(t >= 2, pltpu.roll(u, 2, 0), prev)
    cw = cw_ref[...]
    v = cw[0:1] * u2 + cw[1:2] * u1 + cw[2:3] * u
    o_ref[...] = x + jnp.dot(b * v, wout_ref[...], preferred_element_type=F32)
    u_ref[...] = u


def _conv_sample(x, prev, g, win, cw, wout, steps):
    n = x.shape[0]
    full = _resident((n, D_MODEL))
    return pl.pallas_call(
        functools.partial(_conv_sample_kernel, steps=steps),
        grid=(1,),
        in_specs=[full, full, _resident((1, D_MODEL)), _resident((D_MODEL, 3 * D_MODEL)),
                  _resident((3, D_MODEL)), _resident((D_MODEL, D_MODEL))],
        out_specs=[pl.BlockSpec((n, D_MODEL), lambda i: (0, 0))] * 2,
        out_shape=[jax.ShapeDtypeStruct((n, D_MODEL), F32)] * 2,
        compiler_params=_params(),
        name="conv_sample",
    )(x, prev, g, win, cw, wout)


def _lookup_bias(bucket, table_ref, head):
    acc = jnp.zeros(bucket.shape, F32)
    for b in range(N_BUCKETS):
        acc = jnp.where(bucket == b, table_ref[b, head], acc)
    return acc


def _tables_kernel(bucket_p_ref, bucket_s_ref, table_ref, sinks_ref, bias_p_ref, bias_s_ref, sink_s_ref, *, steps):
    bucket_p = bucket_p_ref[...]
    for head in range(N_HEADS):
        bias_p_ref[head] = _lookup_bias(bucket_p, table_ref, head)
    bucket_s = bucket_s_ref[...]
    grp = lax.broadcasted_iota(jnp.int32, bucket_s.shape, 0) // steps
    grp_l = lax.broadcasted_iota(jnp.int32, (bucket_s.shape[0], KV_LANES), 0) // steps
    for kv in range(N_KV_HEADS):
        bias = jnp.zeros(bucket_s.shape, F32)
        sink = jnp.zeros((bucket_s.shape[0], KV_LANES), F32)
        for gi in range(GROUP):
            head = kv * GROUP + gi
            bias = jnp.where(grp == gi, _lookup_bias(bucket_s, table_ref, head), bias)
            sink = jnp.where(grp_l == gi, sinks_ref[head], sink)
        bias_s_ref[kv] = bias
        sink_s_ref[kv] = sink


def _tables(bucket_p, bucket_s, table, sinks, steps):
    q_rows = bucket_s.shape[0]
    return pl.pallas_call(
        functools.partial(_tables_kernel, steps=steps),
        grid=(1,),
        in_specs=[_resident((WINDOW, KEY_TILE)), _resident((q_rows, KEY_TILE)), _smem(), _smem()],
        out_specs=[pl.BlockSpec((N_HEADS, WINDOW, KEY_TILE), lambda i: (0, 0, 0)),
                   pl.BlockSpec((N_KV_HEADS, q_rows, KEY_TILE), lambda i: (0, 0, 0)),
                   pl.BlockSpec((N_KV_HEADS, q_rows, KV_LANES), lambda i: (0, 0, 0))],
        out_shape=[jax.ShapeDtypeStruct((N_HEADS, WINDOW, KEY_TILE), F32),
                   jax.ShapeDtypeStruct((N_KV_HEADS, q_rows, KEY_TILE), F32),
                   jax.ShapeDtypeStruct((N_KV_HEADS, q_rows, KV_LANES), F32)],
        compiler_params=_params(),
        name="bias_tables",
    )(bucket_p, bucket_s, table, sinks)


def _attn_prompt_kernel(xn_ref, xc_ref, g_ref, wqkv_ref, wo_ref, bias_ref, sinks_ref,
                        o_ref, klast_ref, vlast_ref, qbuf, kbuf, vbuf, kcar, vcar, obuf):
    step = pl.program_id(0)
    rows = xn_ref.shape[0]
    n_blk = rows // WINDOW
    n_chunks = D_MODEL // MXU_WIDTH
    blocks_per_chunk = n_blk // n_chunks

    def norm_next():
        return _rms(xn_ref[...], g_ref[...]).astype(BF16)

    def project_chunk(h, slot, c):
        c0 = c * MXU_WIDTH
        part = jnp.dot(h, wqkv_ref[:, c0:c0 + MXU_WIDTH], preferred_element_type=F32)
        qbuf[slot, :, c0:c0 + MXU_WIDTH] = (part * ATTN_SCALE).astype(BF16)
        if c < n_chunks - 1:
            return
        kv_part = jnp.dot(h, wqkv_ref[:, D_MODEL:], preferred_element_type=F32)
        k = kv_part[:, :KV_LANES]
        v = kv_part[:, KV_LANES:]
        klast_ref[...] = k[rows - WINDOW:]
        vlast_ref[...] = v[rows - WINDOW:]
        lane = lax.broadcasted_iota(jnp.int32, (rows, KV_LANES), 1)
        for kv, half in enumerate((lane < HEAD_DIM, lane >= HEAD_DIM)):
            k_half = jnp.where(half, k, 0.0)
            v_half = jnp.where(half, v, 0.0)
            kbuf[slot, kv] = k_half.astype(BF16)
            vbuf[slot, kv] = v_half.astype(BF16)
            kcar[slot, kv] = k_half[rows - WINDOW:]
            vcar[slot, kv] = v_half[rows - WINDOW:]

    def first(slot):
        kcar[...] = jnp.zeros(kcar.shape, F32)
        vcar[...] = jnp.zeros(vcar.shape, F32)
        h = norm_next()
        for c in range(n_chunks):
            project_chunk(h, slot, c)

    def body(cur, nxt):
        k_prev = [jnp.where(step > 1, kcar[nxt, kv], 0.0).astype(BF16) for kv in range(N_KV_HEADS)]
        v_prev = [jnp.where(step > 1, vcar[nxt, kv], 0.0).astype(BF16) for kv in range(N_KV_HEADS)]
        h = norm_next()

        qi = lax.broadcasted_iota(jnp.int32, (WINDOW, KEY_TILE), 0)
        kj = lax.broadcasted_iota(jnp.int32, (WINDOW, KEY_TILE), 1)
        band = (kj >= qi) & (kj <= qi + WINDOW)
        lane_q = lax.broadcasted_iota(jnp.int32, (WINDOW, KV_LANES), 1)
        lane_k = lax.broadcasted_iota(jnp.int32, (KEY_TILE, KV_LANES), 1)
        ones_half = [(lane_k < HEAD_DIM).astype(F32).astype(BF16), (lane_k >= HEAD_DIM).astype(F32).astype(BF16)]

        def block_scores(blk):
            r0 = blk * WINDOW
            qs = jnp.concatenate(
                [qbuf[cur, r0:r0 + WINDOW, gi * KV_LANES:(gi + 1) * KV_LANES] for gi in range(GROUP)], axis=0)
            if blk == 0:
                keys = [jnp.concatenate([k_prev[kv], kbuf[cur, kv, 0:WINDOW, :]], axis=0) for kv in range(N_KV_HEADS)]
            else:
                keys = [kbuf[cur, kv, r0 - WINDOW:r0 + WINDOW, :] for kv in range(N_KV_HEADS)]
            return [lax.dot_general(qs, keys[kv], (((1,), (1,)), ((), ())), preferred_element_type=F32)
                    for kv in range(N_KV_HEADS)]

        scores = block_scores(0)
        for blk in range(n_blk):
            r0 = blk * WINDOW
            if blk == 0:
                mask = band & (kj >= jnp.where(step > 1, 0, WINDOW))
                vals = [jnp.concatenate([v_prev[kv], vbuf[cur, kv, 0:WINDOW, :]], axis=0) for kv in range(N_KV_HEADS)]
            else:
                mask = band
                vals = [vbuf[cur, kv, r0 - WINDOW:r0 + WINDOW, :] for kv in range(N_KV_HEADS)]
            next_scores = block_scores(blk + 1) if blk + 1 < n_blk else None
            if blk % blocks_per_chunk == 0:
                project_chunk(h, nxt, blk // blocks_per_chunk)
            probs, maxes = [], []
            for kv in range(N_KV_HEADS):
                s_all = scores[kv]
                p_kv, m_kv = [], []
                for gi in range(GROUP):
                    head = kv * GROUP + gi
                    s = s_all[gi * WINDOW:(gi + 1) * WINDOW]
                    s = jnp.where(mask, s + bias_ref[head], NEG_INF)
                    m = jnp.maximum(jnp.max(s, axis=-1, keepdims=True), sinks_ref[head])
                    p_kv.append(jnp.exp(s - m).astype(BF16))
                    m_kv.append(m)
                probs.append(jnp.concatenate(p_kv, axis=0))
                maxes.append(m_kv)
            scores = next_scores
            acc = None
            for kv in range(N_KV_HEADS):
                rhs = jnp.concatenate([vals[kv], ones_half[kv]], axis=1)
                part = jnp.dot(probs[kv], rhs, preferred_element_type=F32)
                acc = part if acc is None else acc + part
            outs = []
            for gi in range(GROUP):
                a = acc[gi * WINDOW:(gi + 1) * WINDOW]
                sink_term = jnp.where(lane_q < HEAD_DIM,
                                      jnp.exp(sinks_ref[gi] - maxes[0][gi]),
                                      jnp.exp(sinks_ref[GROUP + gi] - maxes[1][gi]))
                outs.append(a[:, :KV_LANES] / (a[:, KV_LANES:] + sink_term))
            obuf[r0:r0 + WINDOW, :] = jnp.concatenate(outs, axis=1).astype(BF16)

        o_ref[...] = xc_ref[...] + jnp.dot(obuf[...], wo_ref[...], preferred_element_type=F32)

    _two_stage(first, body)


def _attn_prompt(x, g, wqkv, wo, bias, sinks):
    n = x.shape[0]
    tiles, nxt, cur = _pipeline_specs(n, ATTN_TILE)
    last = pl.BlockSpec((WINDOW, KV_LANES), lambda i: (0, 0))
    return pl.pallas_call(
        _attn_prompt_kernel,
        grid=(tiles + 1,),
        in_specs=[nxt, cur, _resident((1, D_MODEL)), _resident((D_MODEL, QKV_COLS)),
                  _resident((D_MODEL, D_MODEL)), _resident((N_HEADS, WINDOW, KEY_TILE)), _smem()],
        out_specs=[cur, last, last],
        out_shape=[jax.ShapeDtypeStruct((n, D_MODEL), F32),
                   jax.ShapeDtypeStruct((WINDOW, KV_LANES), F32),
                   jax.ShapeDtypeStruct((WINDOW, KV_LANES), F32)],
        scratch_shapes=[pltpu.VMEM((2, ATTN_TILE, D_MODEL), BF16),
                        pltpu.VMEM((2, N_KV_HEADS, ATTN_TILE, KV_LANES), BF16),
                        pltpu.VMEM((2, N_KV_HEADS, ATTN_TILE, KV_LANES), BF16),
                        pltpu.VMEM((2, N_KV_HEADS, WINDOW, KV_LANES), F32),
                        pltpu.VMEM((2, N_KV_HEADS, WINDOW, KV_LANES), F32),
                        pltpu.VMEM((ATTN_TILE, D_MODEL), BF16)],
        compiler_params=_params(),
        name="attn_prompt",
    )(x, x, g, wqkv, wo, bias, sinks)


def _proj_kernel(x_ref, g_ref, w_ref, o_ref):
    h = _rms(x_ref[...], g_ref[...]).astype(BF16)
    o_ref[...] = jnp.dot(h, w_ref[...], preferred_element_type=F32)


def _proj(x, g, w):
    n, cols = x.shape[0], w.shape[1]
    return pl.pallas_call(
        _proj_kernel,
        grid=(1,),
        in_specs=[_resident((n, D_MODEL)), _resident((1, D_MODEL)), _resident((D_MODEL, cols))],
        out_specs=pl.BlockSpec((n, cols), lambda i: (0, 0)),
        out_shape=jax.ShapeDtypeStruct((n, cols), F32),
        compiler_params=_params(),
        name="proj_sample",
    )(x, g, w)


def _oproj_kernel(x_ref, a_ref, w_ref, o_ref):
    o_ref[...] = x_ref[...] + jnp.dot(a_ref[...].astype(BF16), w_ref[...], preferred_element_type=F32)


def _oproj(x, a, w):
    n = x.shape[0]
    return pl.pallas_call(
        _oproj_kernel,
        grid=(1,),
        in_specs=[_resident((n, D_MODEL)), _resident((n, D_MODEL)), _resident((D_MODEL, D_MODEL))],
        out_specs=pl.BlockSpec((n, D_MODEL), lambda i: (0, 0)),
        out_shape=jax.ShapeDtypeStruct((n, D_MODEL), F32),
        compiler_params=_params(),
        name="oproj_sample",
    )(x, a, w)


def _attn_sample_kernel(q_ref, kc_ref, vc_ref, kn_ref, vn_ref, bias_ref, sink_ref,
                        o_ref, kout_ref, vout_ref, *, steps):
    q_rows = GROUP * steps

    qi = lax.broadcasted_iota(jnp.int32, (q_rows, KEY_TILE), 0) % steps
    kj = lax.broadcasted_iota(jnp.int32, (q_rows, KEY_TILE), 1)
    mask = (kj >= qi) & (kj <= qi + WINDOW)
    lane = lax.broadcasted_iota(jnp.int32, (KEY_TILE, KV_LANES), 1)
    lane_q = lax.broadcasted_iota(jnp.int32, (q_rows, KV_LANES), 1)
    kv_lane = (lane < HEAD_DIM, lane >= HEAD_DIM)
    pad = jnp.zeros((KEY_TILE - WINDOW - SUBLANES, KV_LANES), F32)
    r8 = lax.broadcasted_iota(jnp.int32, (SUBLANES, KV_LANES), 0)

    def shifted_cache(cache, new):
        body = pltpu.roll(cache, WINDOW - steps, 0)
        tail = jnp.where(r8 >= SUBLANES - steps, pltpu.roll(new, SUBLANES - steps, 0),
                         body[WINDOW - SUBLANES:])
        return jnp.concatenate([body[:WINDOW - SUBLANES], tail], axis=0)

    batches = range(q_ref.shape[0])
    scores = []
    for n in batches:
        kc, kn = kc_ref[n], kn_ref[n]
        kout_ref[n] = shifted_cache(kc, kn)
        keys = jnp.concatenate([kc, kn, pad], axis=0)
        q = (q_ref[n] * ATTN_SCALE).astype(BF16)
        scores.append([lax.dot_general(q, jnp.where(kv_lane[kv], keys, 0.0).astype(BF16),
                                       (((1,), (1,)), ((), ())), preferred_element_type=F32)
                       for kv in range(N_KV_HEADS)])
    probs, sink_terms = [], []
    for n in batches:
        p_n, t_n = [], []
        for kv in range(N_KV_HEADS):
            s = jnp.where(mask, scores[n][kv] + bias_ref[kv], NEG_INF)
            sink = sink_ref[kv][:, 0:1]
            m = jnp.maximum(jnp.max(s, axis=-1, keepdims=True), sink)
            p_n.append(jnp.exp(s - m).astype(BF16))
            t_n.append(jnp.exp(sink - m))
        probs.append(p_n)
        sink_terms.append(t_n)
    for n in batches:
        vc, vn = vc_ref[n], vn_ref[n]
        vout_ref[n] = shifted_cache(vc, vn)
        vals = jnp.concatenate([vc, vn, pad], axis=0)
        acc = None
        for kv in range(N_KV_HEADS):
            rhs = jnp.concatenate([jnp.where(kv_lane[kv], vals, 0.0), jnp.where(kv_lane[kv], 1.0, 0.0)], axis=1)
            part = jnp.dot(probs[n][kv], rhs.astype(BF16), preferred_element_type=F32)
            acc = part if acc is None else acc + part
        denom = acc[:, KV_LANES:] + jnp.where(lane_q < HEAD_DIM, sink_terms[n][0], sink_terms[n][1])
        o_ref[n] = acc[:, :KV_LANES] / denom


def _attn_sample(q, kc, vc, kn, vn, bias, sink, steps):
    nb = q.shape[0]
    q_rows = GROUP * steps
    bt = SAMPLE_BATCH_TILE

    def batch_spec(r):
        return pl.BlockSpec((bt, r, KV_LANES), lambda i: (i, 0, 0))

    return pl.pallas_call(
        functools.partial(_attn_sample_kernel, steps=steps),
        grid=(nb // bt,),
        in_specs=[batch_spec(q_rows), batch_spec(WINDOW), batch_spec(WINDOW),
                  batch_spec(SUBLANES), batch_spec(SUBLANES),
                  _resident((N_KV_HEADS, q_rows, KEY_TILE)), _resident((N_KV_HEADS, q_rows, KV_LANES))],
        out_specs=[batch_spec(q_rows), batch_spec(WINDOW), batch_spec(WINDOW)],
        out_shape=[jax.ShapeDtypeStruct((nb, q_rows, KV_LANES), F32),
                   jax.ShapeDtypeStruct((nb, WINDOW, KV_LANES), F32),
                   jax.ShapeDtypeStruct((nb, WINDOW, KV_LANES), F32)],
        compiler_params=_params(),
        name="attn_sample",
    )(q, kc, vc, kn, vn, bias, sink)


def kernel(x_prompt, x_sample, state_conv, cache_k, cache_v, g_mix, g_ffn, g_final, w_conv_in, conv_w,
           w_conv_out, w_q, w_k, w_v, w_o, sinks, rel_table, w_gate, w_up, w_down):
    batch, seq, _ = x_prompt.shape
    dec_batch, dec_seq, _ = x_sample.shape
    assert batch == 1 and seq % TOKEN_TILE == 0 and seq % ATTN_TILE == 0
    assert ATTN_TILE % (WINDOW * (D_MODEL // MXU_WIDTH)) == 0
    assert dec_batch % SAMPLE_BATCH_TILE == 0 and dec_seq <= SUBLANES
    assert (dec_batch * dec_seq) % TOKEN_TILE == 0
    assert g_mix.shape[0] == 2, "layer 0 is the conv mixer, layer 1 the attention mixer"

    win = w_conv_in[0]
    wout = w_conv_out[0]
    wq =w_q[0].reshape(D_MODEL, N_KV_HEADS, GROUP, HEAD_DIM).transpose(0, 2, 1, 3).reshape(D_MODEL, D_MODEL)
    wqkv = jnp.concatenate([wq, w_k[0], w_v[0]], axis=1).astype(BF16)
    wo = w_o[0].reshape(N_KV_HEADS, GROUP, HEAD_DIM, D_MODEL).transpose(1, 0, 2, 3).reshape(D_MODEL, D_MODEL)
    wo = wo.astype(BF16)
    gm =g_mix.reshape(2, 1, D_MODEL)
    gf = g_ffn.reshape(2, 1, D_MODEL)
    gfin = g_final.reshape(1, D_MODEL)
    cw = conv_w[0]
    sink_vec = sinks[0]

    bucket_p = jnp.asarray(_t5_bucket_np(
        (np.arange(WINDOW)[:, None] + WINDOW) - np.arange(KEY_TILE)[None, :]))
    dist_s = np.arange(dec_seq)[:, None] + WINDOW - np.arange(KEY_TILE)[None, :]
    bucket_s = jnp.asarray(np.tile(_t5_bucket_np(dist_s), (GROUP, 1)))
    bias_p, bias_s, sink_s = _tables(bucket_p, bucket_s, rel_table, sink_vec, dec_seq)

    n_s = dec_batch * dec_seq
    xp = x_prompt.reshape(seq, D_MODEL)
    xp, tail_p, wg, wu, wd = _conv_prompt(xp, gm[0], win, cw, wout, w_gate, w_up, w_down)
    state_conv_prompt = tail_p[SUBLANES - CONV_STATE:].reshape(1, batch, CONV_STATE, D_MODEL)
    xs = x_sample.reshape(n_s, D_MODEL)
    prev = jnp.pad(state_conv[0], ((0, 0), (0, dec_seq - CONV_STATE), (0, 0))).reshape(n_s, D_MODEL)
    xs, u_s = _conv_sample(xs, prev, gm[0], win, cw, wout, dec_seq)
    state_conv_sample = u_s.reshape(dec_batch, dec_seq, D_MODEL)[:, dec_seq - CONV_STATE:][None]
    xp = _ffn(xp, gf, wg, wu, wd, gfin, 0, False)
    xs = _ffn(xs, gf, wg, wu, wd, gfin, 0, False)

    xp, k_last, v_last = _attn_prompt(xp, gm[1], wqkv, wo, bias_p, sink_vec)
    cache_k_prompt = k_last.reshape(1, batch, WINDOW, N_KV_HEADS, HEAD_DIM)
    cache_v_prompt = v_last.reshape(1, batch, WINDOW, N_KV_HEADS, HEAD_DIM)
    qkv = _proj(xs, gm[1], wqkv)
    q_s = qkv[:, :D_MODEL].reshape(dec_batch, dec_seq, GROUP, KV_LANES).transpose(0, 2, 1, 3)
    q_s = q_s.reshape(dec_batch, GROUP * dec_seq, KV_LANES)
    pad_new = ((0, 0), (0, SUBLANES - dec_seq), (0, 0))
    k_new = jnp.pad(qkv[:, D_MODEL:D_MODEL + KV_LANES].reshape(dec_batch, dec_seq, KV_LANES), pad_new)
    v_new = jnp.pad(qkv[:, D_MODEL + KV_LANES:].reshape(dec_batch, dec_seq, KV_LANES), pad_new)
    kc = cache_k[0].reshape(dec_batch, WINDOW, KV_LANES)
    vc = cache_v[0].reshape(dec_batch, WINDOW, KV_LANES)
    o_s, k_out, v_out = _attn_sample(q_s, kc, vc, k_new, v_new, bias_s, sink_s, dec_seq)
    o_s = o_s.reshape(dec_batch, GROUP, dec_seq, KV_LANES).transpose(0, 2, 1, 3).reshape(n_s, D_MODEL)
    xs = _oproj(xs, o_s, wo)
    y_prompt = _ffn(xp, gf, wg, wu, wd, gfin, 1, True).reshape(batch, seq, D_MODEL)
    y_sample = _ffn(xs, gf, wg, wu, wd, gfin, 1, True).reshape(dec_batch, dec_seq, D_MODEL)
    cache_k_sample = k_out.reshape(1, dec_batch, WINDOW, N_KV_HEADS, HEAD_DIM)
    cache_v_sample = v_out.reshape(1, dec_batch, WINDOW, N_KV_HEADS, HEAD_DIM)

    return (y_prompt, y_sample, state_conv_prompt, state_conv_sample,
            cache_k_prompt, cache_k_sample, cache_v_prompt, cache_v_sample)
```

```python
import functools
import math

import numpy as np
import jax
import jax.numpy as jnp
from jax import lax
from jax.experimental import pallas as pl
from jax.experimental.pallas import tpu as pltpu

D_MODEL = 1024
D_FF = 2816
HEAD_DIM = 64
N_HEADS = 16
N_KV_HEADS = 2
GROUP = N_HEADS // N_KV_HEADS
WINDOW = 128
N_BUCKETS = 32
MAX_DISTANCE = 128
CONV_STATE = 2
EPS = 1e-5
NEG_INF = -1e30
ATTN_SCALE = 1.0 / math.sqrt(HEAD_DIM)

KV_LANES = N_KV_HEADS * HEAD_DIM
QKV_COLS = D_MODEL + 2 * KV_LANES
SUBLANES = 8
BF16_SUBLANES = 16
KEY_TILE = 2 * WINDOW

TOKEN_TILE = 512
ATTN_TILE = 512
MXU_WIDTH = 256
FFN_SPLIT = 2
CONV_SPLIT = 2
SAMPLE_BATCH_TILE = 16
VMEM_LIMIT_BYTES = 56 * 1024 * 1024

F32 = jnp.float32
BF16 = jnp.bfloat16


def _params(n_axes=1):
    return pltpu.CompilerParams(
        dimension_semantics=("arbitrary",) * n_axes,
        vmem_limit_bytes=VMEM_LIMIT_BYTES)


def _resident(shape):
    zeros = (0,) * len(shape)
    return pl.BlockSpec(shape, lambda *_: zeros, pipeline_mode=pl.Buffered(1))


def _layer(shape, layer):
    index = (layer,) + (0,) * len(shape)
    return pl.BlockSpec((None,) + tuple(shape), lambda *_: index, pipeline_mode=pl.Buffered(1))


def _smem():
    return pl.BlockSpec(memory_space=pltpu.SMEM)


def _rms(x, g):
    return x * lax.rsqrt(jnp.mean(x * x, axis=-1, keepdims=True) + EPS) * g


def _pipeline_specs(n, rows):
    tiles = n // rows
    nxt = pl.BlockSpec((rows, D_MODEL), lambda i: (jnp.minimum(i, tiles - 1), 0))
    cur = pl.BlockSpec((rows, D_MODEL), lambda i: (jnp.maximum(i - 1, 0), 0))
    return tiles, nxt, cur


def _two_stage(first, body):
    step = pl.program_id(0)

    @pl.when(step == 0)
    def _():
        first(0)

    @pl.when(step % 2 == 1)
    def _():
        body(0, 1)

    @pl.when((step % 2 == 0) & (step > 0))
    def _():
        body(1, 0)


def _t5_bucket_np(dist):
    n = np.maximum(dist, 0)
    max_exact = N_BUCKETS // 2
    nf = np.maximum(n, 1).astype(np.float32)
    large = max_exact + (np.log(nf / np.float32(max_exact)) / np.float32(math.log(MAX_DISTANCE / max_exact))
                         * np.float32(N_BUCKETS - max_exact)).astype(np.int32)
    large = np.minimum(large, N_BUCKETS - 1)
    return np.where(n < max_exact, n, large).astype(np.int32)


def _ffn_kernel(x_ref, g_ref, wg_ref, wu_ref, wd_ref, gf_ref, *o_refs, final_norm):
    rows = x_ref.shape[0] // FFN_SPLIT
    halves = [pl.ds(i * rows, rows) for i in range(FFN_SPLIT)]
    acts = []
    for r in halves:
        h = _rms(x_ref[r, :], g_ref[...]).astype(BF16)
        gate = jnp.dot(h, wg_ref[...], preferred_element_type=F32)
        up = jnp.dot(h, wu_ref[...], preferred_element_type=F32)
        acts.append((gate * jax.nn.sigmoid(gate) * up).astype(BF16))
    for r, act in zip(halves, acts):
        y = x_ref[r, :] + jnp.dot(act, wd_ref[...], preferred_element_type=F32)
        if final_norm:
            y = _rms(y, gf_ref[...])
        for o_ref in o_refs:
            o_ref[r, :] = y


def _ffn_weight_specs(layer):
    return [_layer((1, D_MODEL), layer), _layer((D_MODEL, D_FF), layer), _layer((D_MODEL, D_FF), layer),
            _layer((D_FF, D_MODEL), layer), _resident((1, D_MODEL))]


def _ffn(x, g, wg, wu, wd, g_final, layer):
    n = x.shape[0]
    tile = pl.BlockSpec((TOKEN_TILE, D_MODEL), lambda i: (i, 0))
    return pl.pallas_call(
        functools.partial(_ffn_kernel, final_norm=False),
        grid=(n // TOKEN_TILE,),
        in_specs=[tile] + _ffn_weight_specs(layer),
        out_specs=tile,
        out_shape=jax.ShapeDtypeStruct((n, D_MODEL), F32),
        compiler_params=_params(),
        name="ffn",
    )(x, g, wg, wu, wd, g_final)


def _ffn_final(x, g, wg, wu, wd, g_final, layer):
    n = x.shape[0]
    tiles = n // TOKEN_TILE - 1
    shape = (TOKEN_TILE, D_MODEL)
    y_prompt, y_sample = pl.pallas_call(
        functools.partial(_ffn_kernel, final_norm=True),
        grid=(tiles + 1,),
        in_specs=[pl.BlockSpec(shape, lambda i: (jnp.where(i == 0, tiles, i - 1), 0))] + _ffn_weight_specs(layer),
        out_specs=[pl.BlockSpec(shape, lambda i: (jnp.maximum(i - 1, 0), 0)),
                   pl.BlockSpec(shape, lambda i: (jnp.minimum(i, 1), 0))],
        out_shape=[jax.ShapeDtypeStruct((tiles * TOKEN_TILE, D_MODEL), F32),
                   jax.ShapeDtypeStruct((2 * TOKEN_TILE, D_MODEL), F32)],
        compiler_params=_params(),
        name="ffn_final",
    )(x, g, wg, wu, wd, g_final)
    return y_prompt, y_sample[:TOKEN_TILE]


def _conv_gates(h, win_ref):
    cx = jnp.dot(h, win_ref[:, D_MODEL:], preferred_element_type=F32)
    u = cx[:, :D_MODEL] * cx[:, D_MODEL:]
    b = jnp.dot(h, win_ref[:, :D_MODEL], preferred_element_type=F32)
    return b, u


def _conv_prompt_kernel(x_ref, g_ref, win_ref, cw_ref, wout_ref, wg_ref, wu_ref, wd_ref,
                        o_ref, tail_out_ref, wg_out_ref, wu_out_ref, wd_out_ref, tail_ref):
    wg_out_ref[...] = wg_ref[...].astype(BF16)
    wu_out_ref[...] = wu_ref[...].astype(BF16)
    wd_out_ref[...] = wd_ref[...].astype(BF16)
    tail = jnp.where(pl.program_id(0) > 0, tail_ref[...], 0.0)
    rows = x_ref.shape[0] // CONV_SPLIT
    groups = [pl.ds(i * rows, rows) for i in range(CONV_SPLIT)]
    gates = [_conv_gates(_rms(x_ref[r, :], g_ref[...]), win_ref) for r in groups]
    cw = cw_ref[...]
    r8 = lax.broadcasted_iota(jnp.int32, (SUBLANES, D_MODEL), 0)
    for r, (b, u) in zip(groups, gates):
        u1 = pltpu.roll(u, 1, 0)
        u2 = pltpu.roll(u, 2, 0)
        u1 = jnp.concatenate([jnp.where(r8 < 1, pltpu.roll(tail, 1, 0), u1[:SUBLANES]), u1[SUBLANES:]], axis=0)
        u2 = jnp.concatenate([jnp.where(r8 < 2, pltpu.roll(tail, 2, 0), u2[:SUBLANES]), u2[SUBLANES:]], axis=0)
        v = cw[0:1] * u2 + cw[1:2] * u1 + cw[2:3] * u
        o_ref[r, :] = x_ref[r, :] + jnp.dot(b * v, wout_ref[...], preferred_element_type=F32)
        tail = u[rows - SUBLANES:]
    tail_ref[...] = tail
    tail_out_ref[...] = tail


def _conv_prompt(x, g, win, cw, wout, w_gate, w_up, w_down):
    n = x.shape[0]
    tiles = n // TOKEN_TILE
    tile = pl.BlockSpec((TOKEN_TILE, D_MODEL), lambda i: (i, 0))

    def slabs(w):
        flat = w.reshape(-1, w.shape[-1])
        rows = flat.shape[0] // tiles
        assert rows * tiles == flat.shape[0] and rows % BF16_SUBLANES == 0
        return flat, pl.BlockSpec((rows, flat.shape[1]), lambda i: (i, 0))

    (wg, wg_spec), (wu, wu_spec), (wd, wd_spec) = slabs(w_gate), slabs(w_up), slabs(w_down)
    out = pl.pallas_call(
        _conv_prompt_kernel,
        grid=(tiles,),
        in_specs=[tile, _resident((1, D_MODEL)), _resident((D_MODEL, 3 * D_MODEL)),
                  _resident((3, D_MODEL)), _resident((D_MODEL, D_MODEL)), wg_spec, wu_spec, wd_spec],
        out_specs=[tile, pl.BlockSpec((SUBLANES, D_MODEL), lambda i: (0, 0)), wg_spec, wu_spec, wd_spec],
        out_shape=[jax.ShapeDtypeStruct((n + TOKEN_TILE, D_MODEL), F32),
                   jax.ShapeDtypeStruct((SUBLANES, D_MODEL), F32),
                   jax.ShapeDtypeStruct(wg.shape, BF16),
                   jax.ShapeDtypeStruct(wu.shape, BF16),
                   jax.ShapeDtypeStruct(wd.shape, BF16)],
        scratch_shapes=[pltpu.VMEM((SUBLANES, D_MODEL), F32)],
        compiler_params=_params(),
        name="conv_prompt",
    )(x, g, win, cw, wout, wg, wu, wd)
    y, tail, wg_b, wu_b, wd_b = out
    return y, tail, wg_b.reshape(w_gate.shape), wu_b.reshape(w_up.shape), wd_b.reshape(w_down.shape)


def _conv_sample_kernel(x_ref, prev_ref, g_ref, win_ref, cw_ref, wout_ref, all_ref, o_ref, u_ref, *, steps):
    del all_ref
    x = x_ref[...]
    b, u = _conv_gates(_rms(x, g_ref[...]), win_ref)
    rows = u.shape[0]
    prev = prev_ref[...]
    t = lax.broadcasted_iota(jnp.int32, u.shape, 0) % steps
    u1 = jnp.where(t >= 1, pltpu.roll(u, 1, 0), pltpu.roll(prev, rows - 1, 0))
    u2 = jnp.where(t >= 2, pltpu.roll(u, 2, 0), prev)
    cw = cw_ref[...]
    v = cw[0:1] * u2 + cw[1:2] * u1 + cw[2:3] * u
    o_ref[...] = x + jnp.dot(b * v, wout_ref[...], preferred_element_type=F32)
    u_ref[...] = u


def _conv_sample(x, prev, g, win, cw, wout, steps, combined):
    n = x.shape[0]
    full = _resident((n, D_MODEL))
    last_tile = combined.shape[0] // n - 1
    return pl.pallas_call(
        functools.partial(_conv_sample_kernel, steps=steps),
        grid=(1,),
        in_specs=[full, full, _resident((1, D_MODEL)), _resident((D_MODEL, 3 * D_MODEL)),
                  _resident((3, D_MODEL)), _resident((D_MODEL, D_MODEL)), pl.BlockSpec(memory_space=pl.ANY)],
        out_specs=[pl.BlockSpec((n, D_MODEL), lambda i: (last_tile, 0)),
                   pl.BlockSpec((n, D_MODEL), lambda i: (0, 0))],
        out_shape=[jax.ShapeDtypeStruct(combined.shape, F32), jax.ShapeDtypeStruct((n, D_MODEL), F32)],
        input_output_aliases={6: 0},
        compiler_params=_params(),
        name="conv_sample",
    )(x, prev, g, win, cw, wout, combined)


def _lookup_bias(bucket, table_ref, head):
    acc = jnp.zeros(bucket.shape, F32)
    for b in range(N_BUCKETS):
        acc = jnp.where(bucket == b, table_ref[b, head], acc)
    return acc


def _tables_kernel(bucket_p_ref, bucket_s_ref, table_ref, sinks_ref, bias_p_ref, bias_s_ref, sink_s_ref, *, steps):
    bucket_p = bucket_p_ref[...]
    for head in range(N_HEADS):
        bias_p_ref[head] = _lookup_bias(bucket_p, table_ref, head)
    bucket_s = bucket_s_ref[...]
    grp = lax.broadcasted_iota(jnp.int32, bucket_s.shape, 0) // steps
    grp_l = lax.broadcasted_iota(jnp.int32, (bucket_s.shape[0], KV_LANES), 0) // steps
    for kv in range(N_KV_HEADS):
        bias = jnp.zeros(bucket_s.shape, F32)
        sink = jnp.zeros((bucket_s.shape[0], KV_LANES), F32)
        for gi in range(GROUP):
            head = kv * GROUP + gi
            bias = jnp.where(grp == gi, _lookup_bias(bucket_s, table_ref, head), bias)
            sink = jnp.where(grp_l == gi, sinks_ref[head], sink)
        bias_s_ref[kv] = bias
        sink_s_ref[kv] = sink


def _tables(bucket_p, bucket_s, table, sinks, steps):
    q_rows = bucket_s.shape[0]
    return pl.pallas_call(
        functools.partial(_tables_kernel, steps=steps),
        grid=(1,),
        in_specs=[_resident((WINDOW, KEY_TILE)), _resident((q_rows, KEY_TILE)), _smem(), _smem()],
        out_specs=[pl.BlockSpec((N_HEADS, WINDOW, KEY_TILE), lambda i: (0, 0, 0)),
                   pl.BlockSpec((N_KV_HEADS, q_rows, KEY_TILE), lambda i: (0, 0, 0)),
                   pl.BlockSpec((N_KV_HEADS, q_rows, KV_LANES), lambda i: (0, 0, 0))],
        out_shape=[jax.ShapeDtypeStruct((N_HEADS, WINDOW, KEY_TILE), F32),
                   jax.ShapeDtypeStruct((N_KV_HEADS, q_rows, KEY_TILE), F32),
                   jax.ShapeDtypeStruct((N_KV_HEADS, q_rows, KV_LANES), F32)],
        compiler_params=_params(),
        name="bias_tables",
    )(bucket_p, bucket_s, table, sinks)


def _attn_prompt_kernel(xn_ref, xc_ref, g_ref, wqkv_ref, wo_ref, bias_ref, sinks_ref,
                        o_ref, klast_ref, vlast_ref, qbuf, kbuf, vbuf, kcar, vcar, obuf):
    step = pl.program_id(0)
    rows = xn_ref.shape[0]
    n_blk = rows // WINDOW
    n_chunks = D_MODEL // MXU_WIDTH
    blocks_per_chunk = n_blk // n_chunks

    def norm_next():
        return _rms(xn_ref[...], g_ref[...]).astype(BF16)

    def project_chunk(h, slot, c):
        c0 = c * MXU_WIDTH
        part = jnp.dot(h, wqkv_ref[:, c0:c0 + MXU_WIDTH], preferred_element_type=F32)
        qbuf[slot, :, c0:c0 + MXU_WIDTH] = (part * ATTN_SCALE).astype(BF16)
        if c < n_chunks - 1:
            return
        kv_part = jnp.dot(h, wqkv_ref[:, D_MODEL:], preferred_element_type=F32)
        k = kv_part[:, :KV_LANES]
        v = kv_part[:, KV_LANES:]
        klast_ref[...] = k[rows - WINDOW:]
        vlast_ref[...] = v[rows - WINDOW:]
        lane = lax.broadcasted_iota(jnp.int32, (rows, KV_LANES), 1)
        for kv, half in enumerate((lane < HEAD_DIM, lane >= HEAD_DIM)):
            k_half = jnp.where(half, k, 0.0)
            v_half = jnp.where(half, v, 0.0)
            kbuf[slot, kv] = k_half.astype(BF16)
            vbuf[slot, kv] = v_half.astype(BF16)
            kcar[slot, kv] = k_half[rows - WINDOW:]
            vcar[slot, kv] = v_half[rows - WINDOW:]

    def first(slot):
        kcar[...] = jnp.zeros(kcar.shape, F32)
        vcar[...] = jnp.zeros(vcar.shape, F32)
        h = norm_next()
        for c in range(n_chunks):
            project_chunk(h, slot, c)

    def body(cur, nxt):
        k_prev = [jnp.where(step > 1, kcar[nxt, kv], 0.0).astype(BF16) for kv in range(N_KV_HEADS)]
        v_prev = [jnp.where(step > 1, vcar[nxt, kv], 0.0).astype(BF16) for kv in range(N_KV_HEADS)]
        h = norm_next()

        qi = lax.broadcasted_iota(jnp.int32, (WINDOW, KEY_TILE), 0)
        kj = lax.broadcasted_iota(jnp.int32, (WINDOW, KEY_TILE), 1)
        band = (kj >= qi) & (kj <= qi + WINDOW)
        lane_q = lax.broadcasted_iota(jnp.int32, (WINDOW, KV_LANES), 1)
        lane_k = lax.broadcasted_iota(jnp.int32, (KEY_TILE, KV_LANES), 1)
        ones_half = [(lane_k < HEAD_DIM).astype(F32).astype(BF16), (lane_k >= HEAD_DIM).astype(F32).astype(BF16)]

        def block_scores(blk):
            r0 = blk * WINDOW
            qs = jnp.concatenate(
                [qbuf[cur, r0:r0 + WINDOW, gi * KV_LANES:(gi + 1) * KV_LANES] for gi in range(GROUP)], axis=0)
            if blk == 0:
                keys = [jnp.concatenate([k_prev[kv], kbuf[cur, kv, 0:WINDOW, :]], axis=0) for kv in range(N_KV_HEADS)]
            else:
                keys = [kbuf[cur, kv, r0 - WINDOW:r0 + WINDOW, :] for kv in range(N_KV_HEADS)]
            return [lax.dot_general(qs, keys[kv], (((1,), (1,)), ((), ())), preferred_element_type=F32)
                    for kv in range(N_KV_HEADS)]

        scores = block_scores(0)
        for blk in range(n_blk):
            r0 = blk * WINDOW
            if blk == 0:
                mask = band & (kj >= jnp.where(step > 1, 0, WINDOW))
                vals = [jnp.concatenate([v_prev[kv], vbuf[cur, kv, 0:WINDOW, :]], axis=0) for kv in range(N_KV_HEADS)]
            else:
                mask = band
                vals = [vbuf[cur, kv, r0 - WINDOW:r0 + WINDOW, :] for kv in range(N_KV_HEADS)]
            next_scores = block_scores(blk + 1) if blk + 1 < n_blk else None
            if blk % blocks_per_chunk == 0:
                project_chunk(h, nxt, blk // blocks_per_chunk)
            probs, maxes = [], []
            for kv in range(N_KV_HEADS):
                s_all = scores[kv]
                p_kv, m_kv = [], []
                for gi in range(GROUP):
                    head = kv * GROUP + gi
                    s = s_all[gi * WINDOW:(gi + 1) * WINDOW]
                    s = jnp.where(mask, s + bias_ref[head], NEG_INF)
                    m = jnp.maximum(jnp.max(s, axis=-1, keepdims=True), sinks_ref[head])
                    p_kv.append(jnp.exp(s - m).astype(BF16))
                    m_kv.append(m)
                probs.append(jnp.concatenate(p_kv, axis=0))
                maxes.append(m_kv)
            scores = next_scores
            acc = None
            for kv in range(N_KV_HEADS):
                rhs = jnp.concatenate([vals[kv], ones_half[kv]], axis=1)
                part = jnp.dot(probs[kv], rhs, preferred_element_type=F32)
                acc = part if acc is None else acc + part
            outs = []
            for gi in range(GROUP):
                a = acc[gi * WINDOW:(gi + 1) * WINDOW]
                sink_term = jnp.where(lane_q < HEAD_DIM,
                                      jnp.exp(sinks_ref[gi] - maxes[0][gi]),
                                      jnp.exp(sinks_ref[GROUP + gi] - maxes[1][gi]))
                outs.append(a[:, :KV_LANES] / (a[:, KV_LANES:] + sink_term))
            obuf[r0:r0 + WINDOW, :] = jnp.concatenate(outs, axis=1).astype(BF16)

        o_ref[...] = xc_ref[...] + jnp.dot(obuf[...], wo_ref[...], preferred_element_type=F32)

    _two_stage(first, body)


def _attn_prompt(x, g, wqkv, wo, bias, sinks, n):
    tiles, nxt, cur = _pipeline_specs(n, ATTN_TILE)
    last = pl.BlockSpec((WINDOW, KV_LANES), lambda i: (0, 0))
    return pl.pallas_call(
        _attn_prompt_kernel,
        grid=(tiles + 1,),
        in_specs=[nxt, cur, _resident((1, D_MODEL)), _resident((D_MODEL, QKV_COLS)),
                  _resident((D_MODEL, D_MODEL)), _resident((N_HEADS, WINDOW, KEY_TILE)), _smem()],
        out_specs=[cur, last, last],
        out_shape=[jax.ShapeDtypeStruct(x.shape, F32),
                   jax.ShapeDtypeStruct((WINDOW, KV_LANES), F32),
                   jax.ShapeDtypeStruct((WINDOW, KV_LANES), F32)],
        scratch_shapes=[pltpu.VMEM((2, ATTN_TILE, D_MODEL), BF16),
                        pltpu.VMEM((2, N_KV_HEADS, ATTN_TILE, KV_LANES), BF16),
                        pltpu.VMEM((2, N_KV_HEADS, ATTN_TILE, KV_LANES), BF16),
                        pltpu.VMEM((2, N_KV_HEADS, WINDOW, KV_LANES), F32),
                        pltpu.VMEM((2, N_KV_HEADS, WINDOW, KV_LANES), F32),
                        pltpu.VMEM((ATTN_TILE, D_MODEL), BF16)],
        compiler_params=_params(),
        name="attn_prompt",
    )(x, x, g, wqkv, wo, bias, sinks)


def _proj_kernel(x_ref, g_ref, w_ref, o_ref):
    h = _rms(x_ref[...], g_ref[...]).astype(BF16)
    o_ref[...] = jnp.dot(h, w_ref[...], preferred_element_type=F32)


def _proj(x, g, w, n):
    cols = w.shape[1]
    last_tile = x.shape[0] // n - 1
    return pl.pallas_call(
        _proj_kernel,
        grid=(1,),
        in_specs=[pl.BlockSpec((n, D_MODEL), lambda i: (last_tile, 0)), _resident((1, D_MODEL)),
                  _resident((D_MODEL, cols))],
        out_specs=pl.BlockSpec((n, cols), lambda i: (0, 0)),
        out_shape=jax.ShapeDtypeStruct((n, cols), F32),
        compiler_params=_params(),
        name="proj_sample",
    )(x, g, w)


def _oproj_kernel(x_ref, a_ref, w_ref, all_ref, o_ref):
    del all_ref
    o_ref[...] = x_ref[...] + jnp.dot(a_ref[...].astype(BF16), w_ref[...], preferred_element_type=F32)


def _oproj(x, a, w, combined):
    n = a.shape[0]
    last_tile = x.shape[0] // n - 1
    sample = pl.BlockSpec((n, D_MODEL), lambda i: (last_tile, 0))
    return pl.pallas_call(
        _oproj_kernel,
        grid=(1,),
        in_specs=[sample, _resident((n, D_MODEL)), _resident((D_MODEL, D_MODEL)),
                  pl.BlockSpec(memory_space=pl.ANY)],
        out_specs=sample,
        out_shape=jax.ShapeDtypeStruct(combined.shape, F32),
        input_output_aliases={3: 0},
        compiler_params=_params(),
        name="oproj_sample",
    )(x, a, w, combined)


def _attn_sample_kernel(q_ref, kc_ref, vc_ref, kn_ref, vn_ref, bias_ref, sink_ref,
                        o_ref, kout_ref, vout_ref, *, steps):
    q_rows = GROUP * steps

    qi = lax.broadcasted_iota(jnp.int32, (q_rows, KEY_TILE), 0) % steps
    kj = lax.broadcasted_iota(jnp.int32, (q_rows, KEY_TILE), 1)
    mask = (kj >= qi) & (kj <= qi + WINDOW)
    lane = lax.broadcasted_iota(jnp.int32, (KEY_TILE, KV_LANES), 1)
    lane_q = lax.broadcasted_iota(jnp.int32, (q_rows, KV_LANES), 1)
    kv_lane = (lane < HEAD_DIM, lane >= HEAD_DIM)
    pad = jnp.zeros((KEY_TILE - WINDOW - SUBLANES, KV_LANES), F32)
    r8 = lax.broadcasted_iota(jnp.int32, (SUBLANES, KV_LANES), 0)

    def shifted_cache(cache, new):
        body = pltpu.roll(cache, WINDOW - steps, 0)
        tail = jnp.where(r8 >= SUBLANES - steps, pltpu.roll(new, SUBLANES - steps, 0),
                         body[WINDOW - SUBLANES:])
        return jnp.concatenate([body[:WINDOW - SUBLANES], tail], axis=0)

    batches = range(q_ref.shape[0])
    scores = []
    for n in batches:
        kc, kn = kc_ref[n], kn_ref[n]
        kout_ref[n] = shifted_cache(kc, kn)
        keys = jnp.concatenate([kc, kn, pad], axis=0)
        q = (q_ref[n] * ATTN_SCALE).astype(BF16)
        scores.append([lax.dot_general(q, jnp.where(kv_lane[kv], keys, 0.0).astype(BF16),
                                       (((1,), (1,)), ((), ())), preferred_element_type=F32)
                       for kv in range(N_KV_HEADS)])
    probs, sink_terms = [], []
    for n in batches:
        p_n, t_n = [], []
        for kv in range(N_KV_HEADS):
            s = jnp.where(mask, scores[n][kv] + bias_ref[kv], NEG_INF)
            sink = sink_ref[kv][:, 0:1]
            m = jnp.maximum(jnp.max(s, axis=-1, keepdims=True), sink)
            p_n.append(jnp.exp(s - m).astype(BF16))
            t_n.append(jnp.exp(sink - m))
        probs.append(p_n)
        sink_terms.append(t_n)
    for n in batches:
        vc, vn = vc_ref[n], vn_ref[n]
        vout_ref[n] = shifted_cache(vc, vn)
        vals = jnp.concatenate([vc, vn, pad], axis=0)
        acc = None
        for kv in range(N_KV_HEADS):
            rhs = jnp.concatenate([jnp.where(kv_lane[kv], vals, 0.0), jnp.where(kv_lane[kv], 1.0, 0.0)], axis=1)
            part = jnp.dot(probs[n][kv], rhs.astype(BF16), preferred_element_type=F32)
            acc = part if acc is None else acc + part
        denom = acc[:, KV_LANES:] + jnp.where(lane_q < HEAD_DIM, sink_terms[n][0], sink_terms[n][1])
        o_ref[n] = acc[:, :KV_LANES] / denom


def _attn_sample(q, kc, vc, kn, vn, bias, sink, steps):
    nb = q.shape[0]
    q_rows = GROUP * steps
    bt = SAMPLE_BATCH_TILE

    def batch_spec(r):
        return pl.BlockSpec((bt, r, KV_LANES), lambda i: (i, 0, 0))

    return pl.pallas_call(
        functools.partial(_attn_sample_kernel, steps=steps),
        grid=(nb // bt,),
        in_specs=[batch_spec(q_rows), batch_spec(WINDOW), batch_spec(WINDOW),
                  batch_spec(SUBLANES), batch_spec(SUBLANES),
                  _resident((N_KV_HEADS, q_rows, KEY_TILE)), _resident((N_KV_HEADS, q_rows, KV_LANES))],
        out_specs=[batch_spec(q_rows), batch_spec(WINDOW), batch_spec(WINDOW)],
        out_shape=[jax.ShapeDtypeStruct((nb, q_rows, KV_LANES), F32),
                   jax.ShapeDtypeStruct((nb, WINDOW, KV_LANES), F32),
                   jax.ShapeDtypeStruct((nb, WINDOW, KV_LANES), F32)],
        compiler_params=_params(),
        name="attn_sample",
    )(q, kc, vc, kn, vn, bias, sink)


def kernel(x_prompt, x_sample, state_conv, cache_k, cache_v, g_mix, g_ffn, g_final, w_conv_in, conv_w,
           w_conv_out, w_q, w_k, w_v, w_o, sinks, rel_table, w_gate, w_up, w_down):
    batch, seq, _ = x_prompt.shape
    dec_batch, dec_seq, _ = x_sample.shape
    assert batch == 1 and seq % TOKEN_TILE == 0 and seq % ATTN_TILE == 0
    assert ATTN_TILE % (WINDOW * (D_MODEL // MXU_WIDTH)) == 0
    assert dec_batch % SAMPLE_BATCH_TILE == 0 and dec_seq <= SUBLANES
    assert dec_batch * dec_seq == TOKEN_TILE, "the sample rows ride the token-wise kernels as one tile"
    assert g_mix.shape[0] == 2, "layer 0 is the conv mixer, layer 1 the attention mixer"

    win = w_conv_in[0]
    wout = w_conv_out[0]
    wq =w_q[0].reshape(D_MODEL, N_KV_HEADS, GROUP, HEAD_DIM).transpose(0, 2, 1, 3).reshape(D_MODEL, D_MODEL)
    wqkv = jnp.concatenate([wq, w_k[0], w_v[0]], axis=1).astype(BF16)
    wo = w_o[0].reshape(N_KV_HEADS, GROUP, HEAD_DIM, D_MODEL).transpose(1, 0, 2, 3).reshape(D_MODEL, D_MODEL)
    wo = wo.astype(BF16)
    gm =g_mix.reshape(2, 1, D_MODEL)
    gf = g_ffn.reshape(2, 1, D_MODEL)
    gfin = g_final.reshape(1, D_MODEL)
    cw = conv_w[0]
    sink_vec = sinks[0]

    bucket_p = jnp.asarray(_t5_bucket_np(
        (np.arange(WINDOW)[:, None] + WINDOW) - np.arange(KEY_TILE)[None, :]))
    dist_s = np.arange(dec_seq)[:, None] + WINDOW - np.arange(KEY_TILE)[None, :]
    bucket_s = jnp.asarray(np.tile(_t5_bucket_np(dist_s), (GROUP, 1)))
    bias_p, bias_s, sink_s = _tables(bucket_p, bucket_s, rel_table, sink_vec, dec_seq)

    n_s = dec_batch * dec_seq
    x, tail_p, wg, wu, wd = _conv_prompt(x_prompt.reshape(seq, D_MODEL), gm[0], win, cw, wout, w_gate, w_up, w_down)
    state_conv_prompt = tail_p[SUBLANES - CONV_STATE:].reshape(1, batch, CONV_STATE, D_MODEL)
    prev = jnp.pad(state_conv[0], ((0, 0), (0, dec_seq - CONV_STATE), (0, 0))).reshape(n_s, D_MODEL)
    x, u_s = _conv_sample(x_sample.reshape(n_s, D_MODEL), prev, gm[0], win, cw, wout, dec_seq, x)
    state_conv_sample = u_s.reshape(dec_batch, dec_seq, D_MODEL)[:, dec_seq - CONV_STATE:][None]
    x = _ffn(x, gf, wg, wu, wd, gfin, 0)

    x_attn, k_last, v_last = _attn_prompt(x, gm[1], wqkv, wo, bias_p, sink_vec, seq)
    cache_k_prompt = k_last.reshape(1, batch, WINDOW, N_KV_HEADS, HEAD_DIM)
    cache_v_prompt = v_last.reshape(1, batch, WINDOW, N_KV_HEADS, HEAD_DIM)
    qkv = _proj(x, gm[1], wqkv, n_s)
    q_s = qkv[:, :D_MODEL].reshape(dec_batch, dec_seq, GROUP, KV_LANES).transpose(0, 2, 1, 3)
    q_s = q_s.reshape(dec_batch, GROUP * dec_seq, KV_LANES)
    pad_new = ((0, 0), (0, SUBLANES - dec_seq), (0, 0))
    k_new = jnp.pad(qkv[:, D_MODEL:D_MODEL + KV_LANES].reshape(dec_batch, dec_seq, KV_LANES), pad_new)
    v_new = jnp.pad(qkv[:, D_MODEL + KV_LANES:].reshape(dec_batch, dec_seq, KV_LANES), pad_new)
    kc = cache_k[0].reshape(dec_batch, WINDOW, KV_LANES)
    vc = cache_v[0].reshape(dec_batch, WINDOW, KV_LANES)
    o_s, k_out, v_out = _attn_sample(q_s, kc, vc, k_new, v_new, bias_s, sink_s, dec_seq)
    o_s = o_s.reshape(dec_batch, GROUP, dec_seq, KV_LANES).transpose(0, 2, 1, 3).reshape(n_s, D_MODEL)
    x = _oproj(x, o_s, wo, x_attn)
    y_prompt, y_sample = _ffn_final(x, gf, wg, wu, wd, gfin, 1)
    y_prompt = y_prompt.reshape(batch, seq, D_MODEL)
    y_sample = y_sample.reshape(dec_batch, dec_seq, D_MODEL)
    cache_k_sample = k_out.reshape(1, dec_batch, WINDOW, N_KV_HEADS, HEAD_DIM)
    cache_v_sample = v_out.reshape(1, dec_batch, WINDOW, N_KV_HEADS, HEAD_DIM)

    return (y_prompt, y_sample, state_conv_prompt, state_conv_sample,
            cache_k_prompt, cache_k_sample, cache_v_prompt, cache_v_sample)
```

```python
import functools
import math

import numpy as np
import jax
import jax.numpy as jnp
from jax import lax
from jax.experimental import pallas as pl
from jax.experimental.pallas import tpu as pltpu

D_MODEL = 1024
D_FF = 2816
HEAD_DIM = 64
N_HEADS = 16
N_KV_HEADS = 2
GROUP = N_HEADS // N_KV_HEADS
WINDOW = 128
N_BUCKETS = 32
MAX_DISTANCE = 128
CONV_STATE = 2
EPS = 1e-5
NEG_INF = -1e30
ATTN_SCALE = 1.0 / math.sqrt(HEAD_DIM)

KV_LANES = N_KV_HEADS * HEAD_DIM
QKV_COLS = D_MODEL + 2 * KV_LANES
SUBLANES = 8
BF16_SUBLANES = 16
KEY_TILE = 2 * WINDOW

TOKEN_TILE = 512
ATTN_TILE = 512
MXU_WIDTH = 256
FFN_SPLIT = 2
CONV_SPLIT = 2
SAMPLE_BATCH_TILE = 32
VMEM_LIMIT_BYTES = 56 * 1024 * 1024

F32 = jnp.float32
BF16 = jnp.bfloat16


def _params(n_axes=1):
    return pltpu.CompilerParams(
        dimension_semantics=("arbitrary",) * n_axes,
        vmem_limit_bytes=VMEM_LIMIT_BYTES)


def _resident(shape):
    zeros = (0,) * len(shape)
    return pl.BlockSpec(shape, lambda *_: zeros, pipeline_mode=pl.Buffered(1))


def _layer(shape, layer):
    index = (layer,) + (0,) * len(shape)
    return pl.BlockSpec((None,) + tuple(shape), lambda *_: index, pipeline_mode=pl.Buffered(1))


def _smem():
    return pl.BlockSpec(memory_space=pltpu.SMEM)


def _rms(x, g):
    return x * lax.rsqrt(jnp.mean(x * x, axis=-1, keepdims=True) + EPS) * g


def _pipeline_specs(n, rows):
    tiles = n // rows
    nxt = pl.BlockSpec((rows, D_MODEL), lambda i: (jnp.minimum(i, tiles - 1), 0))
    cur = pl.BlockSpec((rows, D_MODEL), lambda i: (jnp.maximum(i - 1, 0), 0))
    return tiles, nxt, cur


def _two_stage(first, body):
    step = pl.program_id(0)

    @pl.when(step == 0)
    def _():
        first(0)

    @pl.when(step % 2 == 1)
    def _():
        body(0, 1)

    @pl.when((step % 2 == 0) & (step > 0))
    def _():
        body(1, 0)


def _t5_bucket_np(dist):
    n = np.maximum(dist, 0)
    max_exact = N_BUCKETS // 2
    nf = np.maximum(n, 1).astype(np.float32)
    large = max_exact + (np.log(nf / np.float32(max_exact)) / np.float32(math.log(MAX_DISTANCE / max_exact))
                         * np.float32(N_BUCKETS - max_exact)).astype(np.int32)
    large = np.minimum(large, N_BUCKETS - 1)
    return np.where(n < max_exact, n, large).astype(np.int32)


def _ffn_kernel(x_ref, g_ref, wg_ref, wu_ref, wd_ref, gf_ref, *o_refs, final_norm):
    rows = x_ref.shape[0] // FFN_SPLIT
    halves = [pl.ds(i * rows, rows) for i in range(FFN_SPLIT)]
    acts = []
    for r in halves:
        h = _rms(x_ref[r, :], g_ref[...]).astype(BF16)
        gate = jnp.dot(h, wg_ref[...], preferred_element_type=F32)
        up = jnp.dot(h, wu_ref[...], preferred_element_type=F32)
        acts.append((gate * jax.nn.sigmoid(gate) * up).astype(BF16))
    for r, act in zip(halves, acts):
        y = x_ref[r, :] + jnp.dot(act, wd_ref[...], preferred_element_type=F32)
        if final_norm:
            y = _rms(y, gf_ref[...])
        for o_ref in o_refs:
            o_ref[r, :] = y


def _ffn_weight_specs(layer):
    return [_layer((1, D_MODEL), layer), _layer((D_MODEL, D_FF), layer), _layer((D_MODEL, D_FF), layer),
            _layer((D_FF, D_MODEL), layer), _resident((1, D_MODEL))]


def _ffn(x, g, wg, wu, wd, g_final, layer):
    n = x.shape[0]
    tile = pl.BlockSpec((TOKEN_TILE, D_MODEL), lambda i: (i, 0))
    return pl.pallas_call(
        functools.partial(_ffn_kernel, final_norm=False),
        grid=(n // TOKEN_TILE,),
        in_specs=[tile] + _ffn_weight_specs(layer),
        out_specs=tile,
        out_shape=jax.ShapeDtypeStruct((n, D_MODEL), F32),
        compiler_params=_params(),
        name="ffn",
    )(x, g, wg, wu, wd, g_final)


def _ffn_final(x, g, wg, wu, wd, g_final, layer):
    n = x.shape[0]
    tiles = n // TOKEN_TILE - 1
    shape = (TOKEN_TILE, D_MODEL)
    y_prompt, y_sample = pl.pallas_call(
        functools.partial(_ffn_kernel, final_norm=True),
        grid=(tiles + 1,),
        in_specs=[pl.BlockSpec(shape, lambda i: (jnp.where(i == 0, tiles, i - 1), 0))] + _ffn_weight_specs(layer),
        out_specs=[pl.BlockSpec(shape, lambda i: (jnp.maximum(i - 1, 0), 0)),
                   pl.BlockSpec(shape, lambda i: (jnp.minimum(i, 1), 0))],
        out_shape=[jax.ShapeDtypeStruct((tiles * TOKEN_TILE, D_MODEL), F32),
                   jax.ShapeDtypeStruct((2 * TOKEN_TILE, D_MODEL), F32)],
        compiler_params=_params(),
        name="ffn_final",
    )(x, g, wg, wu, wd, g_final)
    return y_prompt, y_sample[:TOKEN_TILE]


def _conv_gates(h, win_ref):
    cx = jnp.dot(h, win_ref[:, D_MODEL:], preferred_element_type=F32)
    u = cx[:, :D_MODEL] * cx[:, D_MODEL:]
    b = jnp.dot(h, win_ref[:, :D_MODEL], preferred_element_type=F32)
    return b, u


def _conv_prompt_kernel(x_ref, g_ref, win_ref, cw_ref, wout_ref, wg_ref, wu_ref, wd_ref,
                        o_ref, tail_out_ref, wg_out_ref, wu_out_ref, wd_out_ref, tail_ref):
    wg_out_ref[...] = wg_ref[...].astype(BF16)
    wu_out_ref[...] = wu_ref[...].astype(BF16)
    wd_out_ref[...] = wd_ref[...].astype(BF16)
    tail = jnp.where(pl.program_id(0) > 0, tail_ref[...], 0.0)
    rows = x_ref.shape[0] // CONV_SPLIT
    groups = [pl.ds(i * rows, rows) for i in range(CONV_SPLIT)]
    gates = [_conv_gates(_rms(x_ref[r, :], g_ref[...]), win_ref) for r in groups]
    cw = cw_ref[...]
    r8 = lax.broadcasted_iota(jnp.int32, (SUBLANES, D_MODEL), 0)
    for r, (b, u) in zip(groups, gates):
        u1 = pltpu.roll(u, 1, 0)
        u2 = pltpu.roll(u, 2, 0)
        u1 = jnp.concatenate([jnp.where(r8 < 1, pltpu.roll(tail, 1, 0), u1[:SUBLANES]), u1[SUBLANES:]], axis=0)
        u2 = jnp.concatenate([jnp.where(r8 < 2, pltpu.roll(tail, 2, 0), u2[:SUBLANES]), u2[SUBLANES:]], axis=0)
        v = cw[0:1] * u2 + cw[1:2] * u1 + cw[2:3] * u
        o_ref[r, :] = x_ref[r, :] + jnp.dot(b * v, wout_ref[...], preferred_element_type=F32)
        tail = u[rows - SUBLANES:]
    tail_ref[...] = tail
    tail_out_ref[...] = tail


def _conv_prompt(x, g, win, cw, wout, w_gate, w_up, w_down):
    n = x.shape[0]
    tiles = n // TOKEN_TILE
    tile = pl.BlockSpec((TOKEN_TILE, D_MODEL), lambda i: (i, 0))

    def slabs(w):
        flat = w.reshape(-1, w.shape[-1])
        rows = flat.shape[0] // tiles
        assert rows * tiles == flat.shape[0] and rows % BF16_SUBLANES == 0
        return flat, pl.BlockSpec((rows, flat.shape[1]), lambda i: (i, 0))

    (wg, wg_spec), (wu, wu_spec), (wd, wd_spec) = slabs(w_gate), slabs(w_up), slabs(w_down)
    out = pl.pallas_call(
        _conv_prompt_kernel,
        grid=(tiles,),
        in_specs=[tile, _resident((1, D_MODEL)), _resident((D_MODEL, 3 * D_MODEL)),
                  _resident((3, D_MODEL)), _resident((D_MODEL, D_MODEL)), wg_spec, wu_spec, wd_spec],
        out_specs=[tile, pl.BlockSpec((SUBLANES, D_MODEL), lambda i: (0, 0)), wg_spec, wu_spec, wd_spec],
        out_shape=[jax.ShapeDtypeStruct((n + TOKEN_TILE, D_MODEL), F32),
                   jax.ShapeDtypeStruct((SUBLANES, D_MODEL), F32),
                   jax.ShapeDtypeStruct(wg.shape, BF16),
                   jax.ShapeDtypeStruct(wu.shape, BF16),
                   jax.ShapeDtypeStruct(wd.shape, BF16)],
        scratch_shapes=[pltpu.VMEM((SUBLANES, D_MODEL), F32)],
        compiler_params=_params(),
        name="conv_prompt",
    )(x, g, win, cw, wout, wg, wu, wd)
    y, tail, wg_b, wu_b, wd_b = out
    return y, tail, wg_b.reshape(w_gate.shape), wu_b.reshape(w_up.shape), wd_b.reshape(w_down.shape)


def _conv_sample_kernel(x_ref, prev_ref, g_ref, win_ref, cw_ref, wout_ref, all_ref, o_ref, u_ref, *, steps):
    del all_ref
    x = x_ref[...]
    b, u = _conv_gates(_rms(x, g_ref[...]), win_ref)
    rows = u.shape[0]
    prev = prev_ref[...]
    t = lax.broadcasted_iota(jnp.int32, u.shape, 0) % steps
    u1 = jnp.where(t >= 1, pltpu.roll(u, 1, 0), pltpu.roll(prev, rows - 1, 0))
    u2 = jnp.where(t >= 2, pltpu.roll(u, 2, 0), prev)
    cw = cw_ref[...]
    v = cw[0:1] * u2 + cw[1:2] * u1 + cw[2:3] * u
    o_ref[...] = x + jnp.dot(b * v, wout_ref[...], preferred_element_type=F32)
    u_ref[...] = u


def _conv_sample(x, prev, g, win, cw, wout, steps, combined):
    n = x.shape[0]
    full = _resident((n, D_MODEL))
    last_tile = combined.shape[0] // n - 1
    return pl.pallas_call(
        functools.partial(_conv_sample_kernel, steps=steps),
        grid=(1,),
        in_specs=[full, full, _resident((1, D_MODEL)), _resident((D_MODEL, 3 * D_MODEL)),
                  _resident((3, D_MODEL)), _resident((D_MODEL, D_MODEL)), pl.BlockSpec(memory_space=pl.ANY)],
        out_specs=[pl.BlockSpec((n, D_MODEL), lambda i: (last_tile, 0)),
                   pl.BlockSpec((n, D_MODEL), lambda i: (0, 0))],
        out_shape=[jax.ShapeDtypeStruct(combined.shape, F32), jax.ShapeDtypeStruct((n, D_MODEL), F32)],
        input_output_aliases={6: 0},
        compiler_params=_params(),
        name="conv_sample",
    )(x, prev, g, win, cw, wout, combined)


def _lookup_bias(bucket, table_ref, head):
    acc = jnp.zeros(bucket.shape, F32)
    for b in range(N_BUCKETS):
        acc = jnp.where(bucket == b, table_ref[b, head], acc)
    return acc


def _tables_kernel(bucket_p_ref, bucket_s_ref, table_ref, sinks_ref, bias_p_ref, bias_s_ref, sink_s_ref, *, steps):
    bucket_p = bucket_p_ref[...]
    for head in range(N_HEADS):
        bias_p_ref[head] = _lookup_bias(bucket_p, table_ref, head)
    bucket_s = bucket_s_ref[...]
    grp = lax.broadcasted_iota(jnp.int32, bucket_s.shape, 0) // steps
    grp_l = lax.broadcasted_iota(jnp.int32, (bucket_s.shape[0], KV_LANES), 0) // steps
    for kv in range(N_KV_HEADS):
        bias = jnp.zeros(bucket_s.shape, F32)
        sink = jnp.zeros((bucket_s.shape[0], KV_LANES), F32)
        for gi in range(GROUP):
            head = kv * GROUP + gi
            bias = jnp.where(grp == gi, _lookup_bias(bucket_s, table_ref, head), bias)
            sink = jnp.where(grp_l == gi, sinks_ref[head], sink)
        bias_s_ref[kv] = bias
        sink_s_ref[kv] = sink


def _tables(bucket_p, bucket_s, table, sinks, steps):
    q_rows = bucket_s.shape[0]
    return pl.pallas_call(
        functools.partial(_tables_kernel, steps=steps),
        grid=(1,),
        in_specs=[_resident((WINDOW, KEY_TILE)), _resident((q_rows, KEY_TILE)), _smem(), _smem()],
        out_specs=[pl.BlockSpec((N_HEADS, WINDOW, KEY_TILE), lambda i: (0, 0, 0)),
                   pl.BlockSpec((N_KV_HEADS, q_rows, KEY_TILE), lambda i: (0, 0, 0)),
                   pl.BlockSpec((N_KV_HEADS, q_rows, KV_LANES), lambda i: (0, 0, 0))],
        out_shape=[jax.ShapeDtypeStruct((N_HEADS, WINDOW, KEY_TILE), F32),
                   jax.ShapeDtypeStruct((N_KV_HEADS, q_rows, KEY_TILE), F32),
                   jax.ShapeDtypeStruct((N_KV_HEADS, q_rows, KV_LANES), F32)],
        compiler_params=_params(),
        name="bias_tables",
    )(bucket_p, bucket_s, table, sinks)


def _attn_prompt_kernel(xn_ref, xc_ref, g_ref, wqkv_ref, wo_ref, bias_ref, sinks_ref,
                        o_ref, klast_ref, vlast_ref, qbuf, kbuf, vbuf, kcar, vcar, obuf):
    step = pl.program_id(0)
    rows = xn_ref.shape[0]
    n_blk = rows // WINDOW
    n_chunks = D_MODEL // MXU_WIDTH
    blocks_per_chunk = n_blk // n_chunks

    def norm_next():
        return _rms(xn_ref[...], g_ref[...]).astype(BF16)

    def project_chunk(h, slot, c):
        c0 = c * MXU_WIDTH
        part = jnp.dot(h, wqkv_ref[:, c0:c0 + MXU_WIDTH], preferred_element_type=F32)
        qbuf[slot, :, c0:c0 + MXU_WIDTH] = (part * ATTN_SCALE).astype(BF16)
        if c < n_chunks - 1:
            return
        kv_part = jnp.dot(h, wqkv_ref[:, D_MODEL:], preferred_element_type=F32)
        k = kv_part[:, :KV_LANES]
        v = kv_part[:, KV_LANES:]
        klast_ref[...] = k[rows - WINDOW:]
        vlast_ref[...] = v[rows - WINDOW:]
        lane = lax.broadcasted_iota(jnp.int32, (rows, KV_LANES), 1)
        for kv, half in enumerate((lane < HEAD_DIM, lane >= HEAD_DIM)):
            k_half = jnp.where(half, k, 0.0)
            v_half = jnp.where(half, v, 0.0)
            kbuf[slot, kv] = k_half.astype(BF16)
            vbuf[slot, kv] = v_half.astype(BF16)
            kcar[slot, kv] = k_half[rows - WINDOW:]
            vcar[slot, kv] = v_half[rows - WINDOW:]

    def first(slot):
        kcar[...] = jnp.zeros(kcar.shape, F32)
        vcar[...] = jnp.zeros(vcar.shape, F32)
        h = norm_next()
        for c in range(n_chunks):
            project_chunk(h, slot, c)

    def body(cur, nxt):
        k_prev = [jnp.where(step > 1, kcar[nxt, kv], 0.0).astype(BF16) for kv in range(N_KV_HEADS)]
        v_prev = [jnp.where(step > 1, vcar[nxt, kv], 0.0).astype(BF16) for kv in range(N_KV_HEADS)]
        h = norm_next()

        qi = lax.broadcasted_iota(jnp.int32, (WINDOW, KEY_TILE), 0)
        kj = lax.broadcasted_iota(jnp.int32, (WINDOW, KEY_TILE), 1)
        band = (kj >= qi) & (kj <= qi + WINDOW)
        lane_q = lax.broadcasted_iota(jnp.int32, (WINDOW, KV_LANES), 1)
        lane_k = lax.broadcasted_iota(jnp.int32, (KEY_TILE, KV_LANES), 1)
        ones_half = [(lane_k < HEAD_DIM).astype(F32).astype(BF16), (lane_k >= HEAD_DIM).astype(F32).astype(BF16)]

        def block_scores(blk):
            r0 = blk * WINDOW
            qs = jnp.concatenate(
                [qbuf[cur, r0:r0 + WINDOW, gi * KV_LANES:(gi + 1) * KV_LANES] for gi in range(GROUP)], axis=0)
            if blk == 0:
                keys = [jnp.concatenate([k_prev[kv], kbuf[cur, kv, 0:WINDOW, :]], axis=0) for kv in range(N_KV_HEADS)]
            else:
                keys = [kbuf[cur, kv, r0 - WINDOW:r0 + WINDOW, :] for kv in range(N_KV_HEADS)]
            return [lax.dot_general(qs, keys[kv], (((1,), (1,)), ((), ())), preferred_element_type=F32)
                    for kv in range(N_KV_HEADS)]

        scores = block_scores(0)
        for blk in range(n_blk):
            r0 = blk * WINDOW
            if blk == 0:
                mask = band & (kj >= jnp.where(step > 1, 0, WINDOW))
                vals = [jnp.concatenate([v_prev[kv], vbuf[cur, kv, 0:WINDOW, :]], axis=0) for kv in range(N_KV_HEADS)]
            else:
                mask = band
                vals = [vbuf[cur, kv, r0 - WINDOW:r0 + WINDOW, :] for kv in range(N_KV_HEADS)]
            next_scores = block_scores(blk + 1) if blk + 1 < n_blk else None
            if blk % blocks_per_chunk == 0:
                project_chunk(h, nxt, blk // blocks_per_chunk)
            probs, maxes = [], []
            for kv in range(N_KV_HEADS):
                s_all = scores[kv]
                p_kv, m_kv = [], []
                for gi in range(GROUP):
                    head = kv * GROUP + gi
                    s = s_all[gi * WINDOW:(gi + 1) * WINDOW]
                    s = jnp.where(mask, s + bias_ref[head], NEG_INF)
                    m = jnp.maximum(jnp.max(s, axis=-1, keepdims=True), sinks_ref[head])
                    p_kv.append(jnp.exp(s - m).astype(BF16))
                    m_kv.append(m)
                probs.append(jnp.concatenate(p_kv, axis=0))
                maxes.append(m_kv)
            scores = next_scores
            acc = None
            for kv in range(N_KV_HEADS):
                rhs = jnp.concatenate([vals[kv], ones_half[kv]], axis=1)
                part = jnp.dot(probs[kv], rhs, preferred_element_type=F32)
                acc = part if acc is None else acc + part
            outs = []
            for gi in range(GROUP):
                a = acc[gi * WINDOW:(gi + 1) * WINDOW]
                sink_term = jnp.where(lane_q < HEAD_DIM,
                                      jnp.exp(sinks_ref[gi] - maxes[0][gi]),
                                      jnp.exp(sinks_ref[GROUP + gi] - maxes[1][gi]))
                outs.append(a[:, :KV_LANES] / (a[:, KV_LANES:] + sink_term))
            obuf[r0:r0 + WINDOW, :] = jnp.concatenate(outs, axis=1).astype(BF16)

        o_ref[...] = xc_ref[...] + jnp.dot(obuf[...], wo_ref[...], preferred_element_type=F32)

    _two_stage(first, body)


def _attn_prompt(x, g, wqkv, wo, bias, sinks, n):
    tiles, nxt, cur = _pipeline_specs(n, ATTN_TILE)
    last = pl.BlockSpec((WINDOW, KV_LANES), lambda i: (0, 0))
    return pl.pallas_call(
        _attn_prompt_kernel,
        grid=(tiles + 1,),
        in_specs=[nxt, cur, _resident((1, D_MODEL)), _resident((D_MODEL, QKV_COLS)),
                  _resident((D_MODEL, D_MODEL)), _resident((N_HEADS, WINDOW, KEY_TILE)), _smem()],
        out_specs=[cur, last, last],
        out_shape=[jax.ShapeDtypeStruct(x.shape, F32),
                   jax.ShapeDtypeStruct((WINDOW, KV_LANES), F32),
                   jax.ShapeDtypeStruct((WINDOW, KV_LANES), F32)],
        scratch_shapes=[pltpu.VMEM((2, ATTN_TILE, D_MODEL), BF16),
                        pltpu.VMEM((2, N_KV_HEADS, ATTN_TILE, KV_LANES), BF16),
                        pltpu.VMEM((2, N_KV_HEADS, ATTN_TILE, KV_LANES), BF16),
                        pltpu.VMEM((2, N_KV_HEADS, WINDOW, KV_LANES), F32),
                        pltpu.VMEM((2, N_KV_HEADS, WINDOW, KV_LANES), F32),
                        pltpu.VMEM((ATTN_TILE, D_MODEL), BF16)],
        compiler_params=_params(),
        name="attn_prompt",
    )(x, x, g, wqkv, wo, bias, sinks)


def _proj_kernel(x_ref, g_ref, wq_ref, wkvt_ref, q_ref, kvt_ref):
    h = _rms(x_ref[...], g_ref[...]).astype(BF16)
    q_ref[...] = jnp.dot(h, wq_ref[...], preferred_element_type=F32)
    kvt_ref[...] = lax.dot_general(wkvt_ref[...], h, (((1,), (1,)), ((), ())), preferred_element_type=F32)


def _proj(x, g, wqkv, wkvt, n):
    last_tile = x.shape[0] // n - 1
    return pl.pallas_call(
        _proj_kernel,
        grid=(1,),
        in_specs=[pl.BlockSpec((n, D_MODEL), lambda i: (last_tile, 0)), _resident((1, D_MODEL)),
                  _resident((D_MODEL, D_MODEL)), _resident((2 * KV_LANES, D_MODEL))],
        out_specs=[pl.BlockSpec((n, D_MODEL), lambda i: (0, 0)), pl.BlockSpec((2 * KV_LANES, n), lambda i: (0, 0))],
        out_shape=[jax.ShapeDtypeStruct((n, D_MODEL), F32), jax.ShapeDtypeStruct((2 * KV_LANES, n), F32)],
        compiler_params=_params(),
        name="proj_sample",
    )(x, g, wqkv, wkvt)


def _oproj_kernel(x_ref, a_ref, w_ref, all_ref, o_ref):
    del all_ref
    o_ref[...] = x_ref[...] + jnp.dot(a_ref[...].astype(BF16), w_ref[...], preferred_element_type=F32)


def _oproj(x, a, w, combined):
    n = a.shape[0]
    last_tile = x.shape[0] // n - 1
    sample = pl.BlockSpec((n, D_MODEL), lambda i: (last_tile, 0))
    return pl.pallas_call(
        _oproj_kernel,
        grid=(1,),
        in_specs=[sample, _resident((n, D_MODEL)), _resident((D_MODEL, D_MODEL)),
                  pl.BlockSpec(memory_space=pl.ANY)],
        out_specs=sample,
        out_shape=jax.ShapeDtypeStruct(combined.shape, F32),
        input_output_aliases={3: 0},
        compiler_params=_params(),
        name="oproj_sample",
    )(x, a, w, combined)


def _attn_sample_kernel(q_ref, kt_ref, vt_ref, kvn_ref, bias_ref, sink_ref, o_ref, kout_ref, vout_ref, *, steps):
    q_rows = GROUP * steps
    qi = lax.broadcasted_iota(jnp.int32, (q_rows, KEY_TILE), 0) % steps
    kj = lax.broadcasted_iota(jnp.int32, (q_rows, KEY_TILE), 1)
    new_at = KEY_TILE - steps
    mask = ((kj < WINDOW) & (kj >= qi)) | ((kj >= new_at) & (kj - new_at <= qi))
    row = lax.broadcasted_iota(jnp.int32, (KV_LANES, KEY_TILE), 0)
    kv_row = (row < HEAD_DIM, row >= HEAD_DIM)
    lane_q = lax.broadcasted_iota(jnp.int32, (q_rows, KV_LANES), 1)
    lane_c = lax.broadcasted_iota(jnp.int32, (KV_LANES, WINDOW), 1)
    new_k = kvn_ref[0:KV_LANES, :]
    new_v = kvn_ref[KV_LANES:, :]

    def new_cols(new_t, b):
        return pltpu.roll(new_t, (WINDOW - steps - b * steps) % WINDOW, 1)

    def shifted_cache(cache_t, new_last):
        return jnp.where(lane_c < WINDOW - steps, pltpu.roll(cache_t, WINDOW - steps, 1), new_last)

    batches = range(q_ref.shape[0])
    scores = []
    for b in batches:
        kt = kt_ref[b]
        new_last = new_cols(new_k, b)
        kout_ref[b] = shifted_cache(kt, new_last)
        keys = jnp.concatenate([kt, new_last], axis=1)
        q = (q_ref[b] * ATTN_SCALE).astype(BF16)
        scores.append([jnp.dot(q, jnp.where(kv_row[kv], keys, 0.0).astype(BF16), preferred_element_type=F32)
                       for kv in range(N_KV_HEADS)])
    probs, sink_terms = [], []
    for b in batches:
        p_b, t_b = [], []
        for kv in range(N_KV_HEADS):
            s = jnp.where(mask, scores[b][kv] + bias_ref[kv], NEG_INF)
            sink = sink_ref[kv][:, 0:1]
            m = jnp.maximum(jnp.max(s, axis=-1, keepdims=True), sink)
            p_b.append(jnp.exp(s - m).astype(BF16))
            t_b.append(jnp.exp(sink - m))
        probs.append(p_b)
        sink_terms.append(t_b)
    for b in batches:
        vt = vt_ref[b]
        new_last = new_cols(new_v, b)
        vout_ref[b] = shifted_cache(vt, new_last)
        vals = jnp.concatenate([vt, new_last], axis=1)
        acc = None
        for kv in range(N_KV_HEADS):
            rhs = jnp.concatenate([jnp.where(kv_row[kv], vals, 0.0), jnp.where(kv_row[kv], 1.0, 0.0)], axis=0)
            part = lax.dot_general(probs[b][kv], rhs.astype(BF16), (((1,), (1,)), ((), ())),
                                   preferred_element_type=F32)
            acc = part if acc is None else acc + part
        denom = acc[:, KV_LANES:] + jnp.where(lane_q < HEAD_DIM, sink_terms[b][0], sink_terms[b][1])
        o_ref[b] = acc[:, :KV_LANES] / denom


def _attn_sample(q, kt, vt, kvn, bias, sink, steps):
    nb = q.shape[0]
    q_rows = GROUP * steps
    bt = SAMPLE_BATCH_TILE
    assert bt * steps == WINDOW and nb % bt == 0

    def batch_spec(r):
        return pl.BlockSpec((bt, r, KV_LANES), lambda i: (i, 0, 0))

    return pl.pallas_call(
        functools.partial(_attn_sample_kernel, steps=steps),
        grid=(nb // bt,),
        in_specs=[batch_spec(q_rows), batch_spec(KV_LANES), batch_spec(KV_LANES),
                  pl.BlockSpec((2 * KV_LANES, bt * steps), lambda i: (0, i)),
                  _resident((N_KV_HEADS, q_rows, KEY_TILE)), _resident((N_KV_HEADS, q_rows, KV_LANES))],
        out_specs=[batch_spec(q_rows), batch_spec(KV_LANES), batch_spec(KV_LANES)],
        out_shape=[jax.ShapeDtypeStruct((nb, q_rows, KV_LANES), F32),
                   jax.ShapeDtypeStruct((nb, KV_LANES, WINDOW), F32),
                   jax.ShapeDtypeStruct((nb, KV_LANES, WINDOW), F32)],
        compiler_params=_params(),
        name="attn_sample",
    )(q, kt, vt, kvn, bias, sink)


def kernel(x_prompt, x_sample, state_conv, cache_k, cache_v, g_mix, g_ffn, g_final, w_conv_in, conv_w,
           w_conv_out, w_q, w_k, w_v, w_o, sinks, rel_table, w_gate, w_up, w_down):
    batch, seq, _ = x_prompt.shape
    dec_batch, dec_seq, _ = x_sample.shape
    assert batch == 1 and seq % TOKEN_TILE == 0 and seq % ATTN_TILE == 0
    assert ATTN_TILE % (WINDOW * (D_MODEL // MXU_WIDTH)) == 0
    assert dec_batch % SAMPLE_BATCH_TILE == 0 and dec_seq <= SUBLANES
    assert dec_batch * dec_seq == TOKEN_TILE, "the sample rows ride the token-wise kernels as one tile"
    assert g_mix.shape[0] == 2, "layer 0 is the conv mixer, layer 1 the attention mixer"

    win = w_conv_in[0]
    wout = w_conv_out[0]
    wq = w_q[0].reshape(D_MODEL, N_KV_HEADS, GROUP, HEAD_DIM).transpose(0, 2, 1, 3).reshape(D_MODEL, D_MODEL)
    wqkv = jnp.concatenate([wq, w_k[0], w_v[0]], axis=1).astype(BF16)
    wo = w_o[0].reshape(N_KV_HEADS, GROUP, HEAD_DIM, D_MODEL).transpose(1, 0, 2, 3).reshape(D_MODEL, D_MODEL)
    wo = wo.astype(BF16)
    wkvt = jnp.concatenate([w_k[0], w_v[0]], axis=1).T.astype(BF16)
    gm = g_mix.reshape(2, 1, D_MODEL)
    gf = g_ffn.reshape(2, 1, D_MODEL)
    gfin = g_final.reshape(1, D_MODEL)
    cw = conv_w[0]
    sink_vec = sinks[0]

    bucket_p = jnp.asarray(_t5_bucket_np(
        (np.arange(WINDOW)[:, None] + WINDOW) - np.arange(KEY_TILE)[None, :]))
    key_s = np.arange(KEY_TILE)[None, :]
    key_pos = np.where(key_s < WINDOW, key_s, key_s - (KEY_TILE - dec_seq) + WINDOW)
    dist_s = np.arange(dec_seq)[:, None] + WINDOW - key_pos
    bucket_s = jnp.asarray(np.tile(_t5_bucket_np(dist_s), (GROUP, 1)))
    bias_p, bias_s, sink_s = _tables(bucket_p, bucket_s, rel_table, sink_vec, dec_seq)

    n_s = dec_batch * dec_seq
    x, tail_p, wg, wu, wd = _conv_prompt(x_prompt.reshape(seq, D_MODEL), gm[0], win, cw, wout, w_gate, w_up, w_down)
    state_conv_prompt = tail_p[SUBLANES - CONV_STATE:].reshape(1, batch, CONV_STATE, D_MODEL)
    prev = jnp.pad(state_conv[0], ((0, 0), (0, dec_seq - CONV_STATE), (0, 0))).reshape(n_s, D_MODEL)
    x, u_s = _conv_sample(x_sample.reshape(n_s, D_MODEL), prev, gm[0], win, cw, wout, dec_seq, x)
    state_conv_sample = u_s.reshape(dec_batch, dec_seq, D_MODEL)[:, dec_seq - CONV_STATE:][None]
    x = _ffn(x, gf, wg, wu, wd, gfin, 0)

    x_attn, k_last, v_last = _attn_prompt(x, gm[1], wqkv, wo, bias_p, sink_vec, seq)
    cache_k_prompt = k_last.reshape(1, batch, WINDOW, N_KV_HEADS, HEAD_DIM)
    cache_v_prompt = v_last.reshape(1, batch, WINDOW, N_KV_HEADS, HEAD_DIM)
    q_s, kv_new_t = _proj(x, gm[1], wqkv, wkvt, n_s)
    q_s = q_s.reshape(dec_batch, dec_seq, GROUP, KV_LANES).transpose(0, 2, 1, 3)
    q_s = q_s.reshape(dec_batch, GROUP * dec_seq, KV_LANES)
    kt = cache_k[0].transpose(0, 2, 3, 1).reshape(dec_batch, KV_LANES, WINDOW)
    vt = cache_v[0].transpose(0, 2, 3, 1).reshape(dec_batch, KV_LANES, WINDOW)
    o_s, kt_out, vt_out = _attn_sample(q_s, kt, vt, kv_new_t, bias_s, sink_s, dec_seq)
    o_s = o_s.reshape(dec_batch, GROUP, dec_seq, KV_LANES).transpose(0, 2, 1, 3).reshape(n_s, D_MODEL)
    x = _oproj(x, o_s, wo, x_attn)
    y_prompt, y_sample = _ffn_final(x, gf, wg, wu, wd, gfin, 1)
    y_prompt = y_prompt.reshape(batch, seq, D_MODEL)
    y_sample = y_sample.reshape(dec_batch, dec_seq, D_MODEL)
    cache_k_sample = kt_out.reshape(dec_batch, N_KV_HEADS, HEAD_DIM, WINDOW).transpose(0, 3, 1, 2)[None]
    cache_v_sample = vt_out.reshape(dec_batch, N_KV_HEADS, HEAD_DIM, WINDOW).transpose(0, 3, 1, 2)[None]

    return (y_prompt, y_sample, state_conv_prompt, state_conv_sample,
            cache_k_prompt, cache_k_sample, cache_v_prompt, cache_v_sample)
```

```python
import functools
import math

import numpy as np
import jax
import jax.numpy as jnp
from jax import lax
from jax.experimental import pallas as pl
from jax.experimental.pallas import tpu as pltpu

D_MODEL = 1024
D_FF = 2816
HEAD_DIM = 64
N_HEADS = 16
N_KV_HEADS = 2
GROUP = N_HEADS // N_KV_HEADS
WINDOW = 128
N_BUCKETS = 32
MAX_DISTANCE = 128
CONV_STATE = 2
EPS = 1e-5
NEG_INF = -1e30
ATTN_SCALE = 1.0 / math.sqrt(HEAD_DIM)

KV_LANES = N_KV_HEADS * HEAD_DIM
QKV_COLS = D_MODEL + 2 * KV_LANES
SUBLANES = 8
BF16_SUBLANES = 16
KEY_TILE = 2 * WINDOW

TOKEN_TILE = 512
ATTN_TILE = 512
MXU_WIDTH = 256
FFN_SPLIT = 2
CONV_SPLIT = 2
SAMPLE_BATCH_TILE = 32
VMEM_LIMIT_BYTES = 56 * 1024 * 1024

F32 = jnp.float32
BF16 = jnp.bfloat16


def _params(n_axes=1):
    return pltpu.CompilerParams(
        dimension_semantics=("arbitrary",) * n_axes,
        vmem_limit_bytes=VMEM_LIMIT_BYTES)


def _resident(shape):
    zeros = (0,) * len(shape)
    return pl.BlockSpec(shape, lambda *_: zeros, pipeline_mode=pl.Buffered(1))


def _layer(shape, layer):
    index = (layer,) + (0,) * len(shape)
    return pl.BlockSpec((None,) + tuple(shape), lambda *_: index, pipeline_mode=pl.Buffered(1))


def _smem():
    return pl.BlockSpec(memory_space=pltpu.SMEM)


def _rms(x, g):
    return x * lax.rsqrt(jnp.mean(x * x, axis=-1, keepdims=True) + EPS) * g


def _pipeline_specs(n, rows):
    tiles = n // rows
    nxt = pl.BlockSpec((rows, D_MODEL), lambda i: (jnp.minimum(i, tiles - 1), 0))
    cur = pl.BlockSpec((rows, D_MODEL), lambda i: (jnp.maximum(i - 1, 0), 0))
    return tiles, nxt, cur


def _two_stage(first, body):
    step = pl.program_id(0)

    @pl.when(step == 0)
    def _():
        first(0)

    @pl.when(step % 2 == 1)
    def _():
        body(0, 1)

    @pl.when((step % 2 == 0) & (step > 0))
    def _():
        body(1, 0)


def _t5_bucket_np(dist):
    n = np.maximum(dist, 0)
    max_exact = N_BUCKETS // 2
    nf = np.maximum(n, 1).astype(np.float32)
    large = max_exact + (np.log(nf / np.float32(max_exact)) / np.float32(math.log(MAX_DISTANCE / max_exact))
                         * np.float32(N_BUCKETS - max_exact)).astype(np.int32)
    large = np.minimum(large, N_BUCKETS - 1)
    return np.where(n < max_exact, n, large).astype(np.int32)


def _ffn_kernel(x_ref, g_ref, wg_ref, wu_ref, wd_ref, gf_ref, *o_refs, final_norm):
    rows = x_ref.shape[0] // FFN_SPLIT
    halves = [pl.ds(i * rows, rows) for i in range(FFN_SPLIT)]
    acts = []
    for r in halves:
        h = _rms(x_ref[r, :], g_ref[...]).astype(BF16)
        gate = jnp.dot(h, wg_ref[...], preferred_element_type=F32)
        up = jnp.dot(h, wu_ref[...], preferred_element_type=F32)
        acts.append((gate * jax.nn.sigmoid(gate) * up).astype(BF16))
    for r, act in zip(halves, acts):
        y = x_ref[r, :] + jnp.dot(act, wd_ref[...], preferred_element_type=F32)
        if final_norm:
            y = _rms(y, gf_ref[...])
        for o_ref in o_refs:
            o_ref[r, :] = y


def _ffn_weight_specs(layer):
    return [_layer((1, D_MODEL), layer), _layer((D_MODEL, D_FF), layer), _layer((D_MODEL, D_FF), layer),
            _layer((D_FF, D_MODEL), layer), _resident((1, D_MODEL))]


def _ffn(x, g, wg, wu, wd, g_final, layer):
    n = x.shape[0]
    tile = pl.BlockSpec((TOKEN_TILE, D_MODEL), lambda i: (i, 0))
    return pl.pallas_call(
        functools.partial(_ffn_kernel, final_norm=False),
        grid=(n // TOKEN_TILE,),
        in_specs=[tile] + _ffn_weight_specs(layer),
        out_specs=tile,
        out_shape=jax.ShapeDtypeStruct((n, D_MODEL), F32),
        compiler_params=_params(),
        name="ffn",
    )(x, g, wg, wu, wd, g_final)


def _ffn_final(x, g, wg, wu, wd, g_final, layer):
    n = x.shape[0]
    tiles = n // TOKEN_TILE - 1
    shape = (TOKEN_TILE, D_MODEL)
    y_prompt, y_sample = pl.pallas_call(
        functools.partial(_ffn_kernel, final_norm=True),
        grid=(tiles + 1,),
        in_specs=[pl.BlockSpec(shape, lambda i: (jnp.where(i == 0, tiles, i - 1), 0))] + _ffn_weight_specs(layer),
        out_specs=[pl.BlockSpec(shape, lambda i: (jnp.maximum(i - 1, 0), 0)),
                   pl.BlockSpec(shape, lambda i: (jnp.minimum(i, 1), 0))],
        out_shape=[jax.ShapeDtypeStruct((tiles * TOKEN_TILE, D_MODEL), F32),
                   jax.ShapeDtypeStruct((2 * TOKEN_TILE, D_MODEL), F32)],
        compiler_params=_params(),
        name="ffn_final",
    )(x, g, wg, wu, wd, g_final)
    return y_prompt, y_sample[:TOKEN_TILE]


def _conv_gates(h, win_ref):
    cx = jnp.dot(h, win_ref[:, D_MODEL:], preferred_element_type=F32)
    u = cx[:, :D_MODEL] * cx[:, D_MODEL:]
    b = jnp.dot(h, win_ref[:, :D_MODEL], preferred_element_type=F32)
    return b, u


def _conv_kernel(x_ref, xs_ref, prev_ref, g_ref, win_ref, cw_ref, wout_ref, wg_ref, wu_ref, wd_ref,
                 o_ref, tail_out_ref, us_ref, wg_out_ref, wu_out_ref, wd_out_ref, tail_ref, *, steps):
    step = pl.program_id(0)
    last = pl.num_programs(0) - 1
    cw = cw_ref[...]

    @pl.when(step < last)
    def _():
        wg_out_ref[...] = wg_ref[...].astype(BF16)
        wu_out_ref[...] = wu_ref[...].astype(BF16)
        wd_out_ref[...] = wd_ref[...].astype(BF16)
        tail = jnp.where(step > 0, tail_ref[...], 0.0)
        rows = x_ref.shape[0] // CONV_SPLIT
        groups = [pl.ds(i * rows, rows) for i in range(CONV_SPLIT)]
        gates = [_conv_gates(_rms(x_ref[r, :], g_ref[...]), win_ref) for r in groups]
        r8 = lax.broadcasted_iota(jnp.int32, (SUBLANES, D_MODEL), 0)
        for r, (b, u) in zip(groups, gates):
            u1 = pltpu.roll(u, 1, 0)
            u2 = pltpu.roll(u, 2, 0)
            u1 = jnp.concatenate([jnp.where(r8 < 1, pltpu.roll(tail, 1, 0), u1[:SUBLANES]), u1[SUBLANES:]], axis=0)
            u2 = jnp.concatenate([jnp.where(r8 < 2, pltpu.roll(tail, 2, 0), u2[:SUBLANES]), u2[SUBLANES:]], axis=0)
            v = cw[0:1] * u2 + cw[1:2] * u1 + cw[2:3] * u
            o_ref[r, :] = x_ref[r, :] + jnp.dot(b * v, wout_ref[...], preferred_element_type=F32)
            tail = u[rows - SUBLANES:]
        tail_ref[...] = tail
        tail_out_ref[...] = tail

    @pl.when(step == last)
    def _():
        x = xs_ref[...]
        b, u = _conv_gates(_rms(x, g_ref[...]), win_ref)
        rows = u.shape[0]
        prev = prev_ref[...]
        t = lax.broadcasted_iota(jnp.int32, u.shape, 0) % steps
        u1 = jnp.where(t >= 1, pltpu.roll(u, 1, 0), pltpu.roll(prev, rows - 1, 0))
        u2 = jnp.where(t >= 2, pltpu.roll(u, 2, 0), prev)
        v = cw[0:1] * u2 + cw[1:2] * u1 + cw[2:3] * u
        o_ref[...] = x + jnp.dot(b * v, wout_ref[...], preferred_element_type=F32)
        us_ref[...] = u


def _conv(x, xs, prev, g, win, cw, wout, w_gate, w_up, w_down, steps):
    n = x.shape[0]
    tiles = n // TOKEN_TILE
    assert xs.shape == (TOKEN_TILE, D_MODEL)
    tile_in = pl.BlockSpec((TOKEN_TILE, D_MODEL), lambda i: (jnp.minimum(i, tiles - 1), 0))
    tile_out = pl.BlockSpec((TOKEN_TILE, D_MODEL), lambda i: (i, 0))
    sample = _resident((TOKEN_TILE, D_MODEL))

    def slabs(w):
        flat = w.reshape(-1, w.shape[-1])
        rows = flat.shape[0] // tiles
        assert rows * tiles == flat.shape[0] and rows % BF16_SUBLANES == 0
        return flat, pl.BlockSpec((rows, flat.shape[1]), lambda i: (jnp.minimum(i, tiles - 1), 0))

    (wg, wg_spec), (wu, wu_spec), (wd, wd_spec) = slabs(w_gate), slabs(w_up), slabs(w_down)
    y, tail, u_s, wg_b, wu_b, wd_b = pl.pallas_call(
        functools.partial(_conv_kernel, steps=steps),
        grid=(tiles + 1,),
        in_specs=[tile_in, sample, sample, _resident((1, D_MODEL)), _resident((D_MODEL, 3 * D_MODEL)),
                  _resident((3, D_MODEL)), _resident((D_MODEL, D_MODEL)), wg_spec, wu_spec, wd_spec],
        out_specs=[tile_out, pl.BlockSpec((SUBLANES, D_MODEL), lambda i: (0, 0)),
                   pl.BlockSpec((TOKEN_TILE, D_MODEL), lambda i: (0, 0)), wg_spec, wu_spec, wd_spec],
        out_shape=[jax.ShapeDtypeStruct((n + TOKEN_TILE, D_MODEL), F32),
                   jax.ShapeDtypeStruct((SUBLANES, D_MODEL), F32),
                   jax.ShapeDtypeStruct((TOKEN_TILE, D_MODEL), F32),
                   jax.ShapeDtypeStruct(wg.shape, BF16),
                   jax.ShapeDtypeStruct(wu.shape, BF16),
                   jax.ShapeDtypeStruct(wd.shape, BF16)],
        scratch_shapes=[pltpu.VMEM((SUBLANES, D_MODEL), F32)],
        compiler_params=_params(),
        name="conv",
    )(x, xs, prev, g, win, cw, wout, wg, wu, wd)
    return y, tail, u_s, wg_b.reshape(w_gate.shape), wu_b.reshape(w_up.shape), wd_b.reshape(w_down.shape)


def _lookup_bias(bucket, table_ref, head):
    acc = jnp.zeros(bucket.shape, F32)
    for b in range(N_BUCKETS):
        acc = jnp.where(bucket == b, table_ref[b, head], acc)
    return acc


def _tables_kernel(bucket_p_ref, bucket_s_ref, table_ref, sinks_ref, bias_p_ref, bias_s_ref, sink_s_ref, *, steps):
    bucket_p = bucket_p_ref[...]
    for head in range(N_HEADS):
        bias_p_ref[head] = _lookup_bias(bucket_p, table_ref, head)
    bucket_s = bucket_s_ref[...]
    grp = lax.broadcasted_iota(jnp.int32, bucket_s.shape, 0) // steps
    grp_l = lax.broadcasted_iota(jnp.int32, (bucket_s.shape[0], KV_LANES), 0) // steps
    for kv in range(N_KV_HEADS):
        bias = jnp.zeros(bucket_s.shape, F32)
        sink = jnp.zeros((bucket_s.shape[0], KV_LANES), F32)
        for gi in range(GROUP):
            head = kv * GROUP + gi
            bias = jnp.where(grp == gi, _lookup_bias(bucket_s, table_ref, head), bias)
            sink = jnp.where(grp_l == gi, sinks_ref[head], sink)
        bias_s_ref[kv] = bias
        sink_s_ref[kv] = sink


def _tables(bucket_p, bucket_s, table, sinks, steps):
    q_rows = bucket_s.shape[0]
    return pl.pallas_call(
        functools.partial(_tables_kernel, steps=steps),
        grid=(1,),
        in_specs=[_resident((WINDOW, KEY_TILE)), _resident((q_rows, KEY_TILE)), _smem(), _smem()],
        out_specs=[pl.BlockSpec((N_HEADS, WINDOW, KEY_TILE), lambda i: (0, 0, 0)),
                   pl.BlockSpec((N_KV_HEADS, q_rows, KEY_TILE), lambda i: (0, 0, 0)),
                   pl.BlockSpec((N_KV_HEADS, q_rows, KV_LANES), lambda i: (0, 0, 0))],
        out_shape=[jax.ShapeDtypeStruct((N_HEADS, WINDOW, KEY_TILE), F32),
                   jax.ShapeDtypeStruct((N_KV_HEADS, q_rows, KEY_TILE), F32),
                   jax.ShapeDtypeStruct((N_KV_HEADS, q_rows, KV_LANES), F32)],
        compiler_params=_params(),
        name="bias_tables",
    )(bucket_p, bucket_s, table, sinks)


def _attn_prompt_kernel(xn_ref, xc_ref, g_ref, wqkv_ref, wo_ref, bias_ref, sinks_ref,
                        o_ref, klast_ref, vlast_ref, qbuf, kbuf, vbuf, kcar, vcar, obuf):
    step = pl.program_id(0)
    rows = xn_ref.shape[0]
    n_blk = rows // WINDOW
    n_chunks = D_MODEL // MXU_WIDTH
    blocks_per_chunk = n_blk // n_chunks

    def norm_next():
        return _rms(xn_ref[...], g_ref[...]).astype(BF16)

    def project_chunk(h, slot, c):
        c0 = c * MXU_WIDTH
        part = jnp.dot(h, wqkv_ref[:, c0:c0 + MXU_WIDTH], preferred_element_type=F32)
        qbuf[slot, :, c0:c0 + MXU_WIDTH] = (part * ATTN_SCALE).astype(BF16)
        if c < n_chunks - 1:
            return
        kv_part = jnp.dot(h, wqkv_ref[:, D_MODEL:], preferred_element_type=F32)
        k = kv_part[:, :KV_LANES]
        v = kv_part[:, KV_LANES:]
        klast_ref[...] = k[rows - WINDOW:]
        vlast_ref[...] = v[rows - WINDOW:]
        lane = lax.broadcasted_iota(jnp.int32, (rows, KV_LANES), 1)
        for kv, half in enumerate((lane < HEAD_DIM, lane >= HEAD_DIM)):
            k_half = jnp.where(half, k, 0.0)
            v_half = jnp.where(half, v, 0.0)
            kbuf[slot, kv] = k_half.astype(BF16)
            vbuf[slot, kv] = v_half.astype(BF16)
            kcar[slot, kv] = k_half[rows - WINDOW:]
            vcar[slot, kv] = v_half[rows - WINDOW:]

    def first(slot):
        o_ref[...] = jnp.zeros(o_ref.shape, F32)
        kcar[...] = jnp.zeros(kcar.shape, F32)
        vcar[...] = jnp.zeros(vcar.shape, F32)
        h = norm_next()
        for c in range(n_chunks):
            project_chunk(h, slot, c)

    def body(cur, nxt):
        k_prev = [jnp.where(step > 1, kcar[nxt, kv], 0.0).astype(BF16) for kv in range(N_KV_HEADS)]
        v_prev = [jnp.where(step > 1, vcar[nxt, kv], 0.0).astype(BF16) for kv in range(N_KV_HEADS)]
        h = norm_next()

        qi = lax.broadcasted_iota(jnp.int32, (WINDOW, KEY_TILE), 0)
        kj = lax.broadcasted_iota(jnp.int32, (WINDOW, KEY_TILE), 1)
        band = (kj >= qi) & (kj <= qi + WINDOW)
        lane_q = lax.broadcasted_iota(jnp.int32, (WINDOW, KV_LANES), 1)
        lane_k = lax.broadcasted_iota(jnp.int32, (KEY_TILE, KV_LANES), 1)
        ones_half = [(lane_k < HEAD_DIM).astype(F32).astype(BF16), (lane_k >= HEAD_DIM).astype(F32).astype(BF16)]

        def block_scores(blk):
            r0 = blk * WINDOW
            qs = jnp.concatenate(
                [qbuf[cur, r0:r0 + WINDOW, gi * KV_LANES:(gi + 1) * KV_LANES] for gi in range(GROUP)], axis=0)
            if blk == 0:
                keys = [jnp.concatenate([k_prev[kv], kbuf[cur, kv, 0:WINDOW, :]], axis=0) for kv in range(N_KV_HEADS)]
            else:
                keys = [kbuf[cur, kv, r0 - WINDOW:r0 + WINDOW, :] for kv in range(N_KV_HEADS)]
            return [lax.dot_general(qs, keys[kv], (((1,), (1,)), ((), ())), preferred_element_type=F32)
                    for kv in range(N_KV_HEADS)]

        scores = block_scores(0)
        for blk in range(n_blk):
            r0 = blk * WINDOW
            if blk == 0:
                mask = band & (kj >= jnp.where(step > 1, 0, WINDOW))
                vals = [jnp.concatenate([v_prev[kv], vbuf[cur, kv, 0:WINDOW, :]], axis=0) for kv in range(N_KV_HEADS)]
            else:
                mask = band
                vals = [vbuf[cur, kv, r0 - WINDOW:r0 + WINDOW, :] for kv in range(N_KV_HEADS)]
            next_scores = block_scores(blk + 1) if blk + 1 < n_blk else None
            if blk % blocks_per_chunk == 0:
                project_chunk(h, nxt, blk // blocks_per_chunk)
            probs, maxes = [], []
            for kv in range(N_KV_HEADS):
                s_all = scores[kv]
                p_kv, m_kv = [], []
                for gi in range(GROUP):
                    head = kv * GROUP + gi
                    s = s_all[gi * WINDOW:(gi + 1) * WINDOW]
                    s = jnp.where(mask, s + bias_ref[head], NEG_INF)
                    m = jnp.maximum(jnp.max(s, axis=-1, keepdims=True), sinks_ref[head])
                    p_kv.append(jnp.exp(s - m).astype(BF16))
                    m_kv.append(m)
                probs.append(jnp.concatenate(p_kv, axis=0))
                maxes.append(m_kv)
            scores = next_scores
            acc = None
            for kv in range(N_KV_HEADS):
                rhs = jnp.concatenate([vals[kv], ones_half[kv]], axis=1)
                part = jnp.dot(probs[kv], rhs, preferred_element_type=F32)
                acc = part if acc is None else acc + part
            outs = []
            for gi in range(GROUP):
                a = acc[gi * WINDOW:(gi + 1) * WINDOW]
                sink_term = jnp.where(lane_q < HEAD_DIM,
                                      jnp.exp(sinks_ref[gi] - maxes[0][gi]),
                                      jnp.exp(sinks_ref[GROUP + gi] - maxes[1][gi]))
                outs.append(a[:, :KV_LANES] / (a[:, KV_LANES:] + sink_term))
            obuf[r0:r0 + WINDOW, :] = jnp.concatenate(outs, axis=1).astype(BF16)

        o_ref[...] = xc_ref[...] + jnp.dot(obuf[...], wo_ref[...], preferred_element_type=F32)

    _two_stage(first, body)


def _attn_prompt(x, g, wqkv, wo, bias, sinks, n):
    tiles, nxt, cur = _pipeline_specs(n, ATTN_TILE)
    assert x.shape[0] == n + ATTN_TILE
    out = pl.BlockSpec((ATTN_TILE, D_MODEL), lambda i: (jnp.where(i == 0, tiles, i - 1), 0))
    last = pl.BlockSpec((WINDOW, KV_LANES), lambda i: (0, 0))
    return pl.pallas_call(
        _attn_prompt_kernel,
        grid=(tiles + 1,),
        in_specs=[nxt, cur, _resident((1, D_MODEL)), _resident((D_MODEL, QKV_COLS)),
                  _resident((D_MODEL, D_MODEL)), _resident((N_HEADS, WINDOW, KEY_TILE)), _smem()],
        out_specs=[out, last, last],
        out_shape=[jax.ShapeDtypeStruct(x.shape, F32),
                   jax.ShapeDtypeStruct((WINDOW, KV_LANES), F32),
                   jax.ShapeDtypeStruct((WINDOW, KV_LANES), F32)],
        scratch_shapes=[pltpu.VMEM((2, ATTN_TILE, D_MODEL), BF16),
                        pltpu.VMEM((2, N_KV_HEADS, ATTN_TILE, KV_LANES), BF16),
                        pltpu.VMEM((2, N_KV_HEADS, ATTN_TILE, KV_LANES), BF16),
                        pltpu.VMEM((2, N_KV_HEADS, WINDOW, KV_LANES), F32),
                        pltpu.VMEM((2, N_KV_HEADS, WINDOW, KV_LANES), F32),
                        pltpu.VMEM((ATTN_TILE, D_MODEL), BF16)],
        compiler_params=_params(),
        name="attn_prompt",
    )(x, x, g, wqkv, wo, bias, sinks)


def _proj_kernel(x_ref, g_ref, wq_ref, wkvt_ref, q_ref, kvt_ref):
    h = _rms(x_ref[...], g_ref[...]).astype(BF16)
    q_ref[...] = jnp.dot(h, wq_ref[...], preferred_element_type=F32)
    kvt_ref[...] = lax.dot_general(wkvt_ref[...], h, (((1,), (1,)), ((), ())), preferred_element_type=F32)


def _proj(x, g, wqkv, wkvt, n):
    last_tile = x.shape[0] // n - 1
    return pl.pallas_call(
        _proj_kernel,
        grid=(1,),
        in_specs=[pl.BlockSpec((n, D_MODEL), lambda i: (last_tile, 0)), _resident((1, D_MODEL)),
                  _resident((D_MODEL, D_MODEL)), _resident((2 * KV_LANES, D_MODEL))],
        out_specs=[pl.BlockSpec((n, D_MODEL), lambda i: (0, 0)), pl.BlockSpec((2 * KV_LANES, n), lambda i: (0, 0))],
        out_shape=[jax.ShapeDtypeStruct((n, D_MODEL), F32), jax.ShapeDtypeStruct((2 * KV_LANES, n), F32)],
        compiler_params=_params(),
        name="proj_sample",
    )(x, g, wqkv, wkvt)


def _oproj_kernel(x_ref, a_ref, w_ref, all_ref, o_ref):
    del all_ref
    o_ref[...] = x_ref[...] + jnp.dot(a_ref[...].astype(BF16), w_ref[...], preferred_element_type=F32)


def _oproj(x, a, w, combined):
    n = a.shape[0]
    last_tile = x.shape[0] // n - 1
    sample = pl.BlockSpec((n, D_MODEL), lambda i: (last_tile, 0))
    return pl.pallas_call(
        _oproj_kernel,
        grid=(1,),
        in_specs=[sample, _resident((n, D_MODEL)), _resident((D_MODEL, D_MODEL)),
                  pl.BlockSpec(memory_space=pl.ANY)],
        out_specs=sample,
        out_shape=jax.ShapeDtypeStruct(combined.shape, F32),
        input_output_aliases={3: 0},
        compiler_params=_params(),
        name="oproj_sample",
    )(x, a, w, combined)


def _attn_sample_kernel(q_ref, kt_ref, vt_ref, kvn_ref, bias_ref, sink_ref, o_ref, kout_ref, vout_ref, *, steps):
    q_rows = GROUP * steps
    qi = lax.broadcasted_iota(jnp.int32, (q_rows, KEY_TILE), 0) % steps
    kj = lax.broadcasted_iota(jnp.int32, (q_rows, KEY_TILE), 1)
    new_at = KEY_TILE - steps
    mask = ((kj < WINDOW) & (kj >= qi)) | ((kj >= new_at) & (kj - new_at <= qi))
    row = lax.broadcasted_iota(jnp.int32, (KV_LANES, KEY_TILE), 0)
    kv_row = (row < HEAD_DIM, row >= HEAD_DIM)
    lane_q = lax.broadcasted_iota(jnp.int32, (q_rows, KV_LANES), 1)
    lane_c = lax.broadcasted_iota(jnp.int32, (KV_LANES, WINDOW), 1)
    new_k = kvn_ref[0:KV_LANES, :]
    new_v = kvn_ref[KV_LANES:, :]

    def new_cols(new_t, b):
        return pltpu.roll(new_t, (WINDOW - steps - b * steps) % WINDOW, 1)

    def shifted_cache(cache_t, new_last):
        return jnp.where(lane_c < WINDOW - steps, pltpu.roll(cache_t, WINDOW - steps, 1), new_last)

    batches = range(q_ref.shape[0])
    scores = []
    for b in batches:
        kt = kt_ref[b]
        new_last = new_cols(new_k, b)
        kout_ref[b] = shifted_cache(kt, new_last)
        keys = jnp.concatenate([kt, new_last], axis=1)
        q = (q_ref[b] * ATTN_SCALE).astype(BF16)
        scores.append([jnp.dot(q, jnp.where(kv_row[kv], keys, 0.0).astype(BF16), preferred_element_type=F32)
                       for kv in range(N_KV_HEADS)])
    probs, sink_terms = [], []
    for b in batches:
        p_b, t_b = [], []
        for kv in range(N_KV_HEADS):
            s = jnp.where(mask, scores[b][kv] + bias_ref[kv], NEG_INF)
            sink = sink_ref[kv][:, 0:1]
            m = jnp.maximum(jnp.max(s, axis=-1, keepdims=True), sink)
            p_b.append(jnp.exp(s - m).astype(BF16))
            t_b.append(jnp.exp(sink - m))
        probs.append(p_b)
        sink_terms.append(t_b)
    for b in batches:
        vt = vt_ref[b]
        new_last = new_cols(new_v, b)
        vout_ref[b] = shifted_cache(vt, new_last)
        vals = jnp.concatenate([vt, new_last], axis=1)
        acc = None
        for kv in range(N_KV_HEADS):
            rhs = jnp.concatenate([jnp.where(kv_row[kv], vals, 0.0), jnp.where(kv_row[kv], 1.0, 0.0)], axis=0)
            part = lax.dot_general(probs[b][kv], rhs.astype(BF16), (((1,), (1,)), ((), ())),
                                   preferred_element_type=F32)
            acc = part if acc is None else acc + part
        denom = acc[:, KV_LANES:] + jnp.where(lane_q < HEAD_DIM, sink_terms[b][0], sink_terms[b][1])
        o_ref[b] = acc[:, :KV_LANES] / denom


def _attn_sample(q, kt, vt, kvn, bias, sink, steps):
    nb = q.shape[0]
    q_rows = GROUP * steps
    bt = SAMPLE_BATCH_TILE
    assert bt * steps == WINDOW and nb % bt == 0

    def batch_spec(r):
        return pl.BlockSpec((bt, r, KV_LANES), lambda i: (i, 0, 0))

    return pl.pallas_call(
        functools.partial(_attn_sample_kernel, steps=steps),
        grid=(nb // bt,),
        in_specs=[batch_spec(q_rows), batch_spec(KV_LANES), batch_spec(KV_LANES),
                  pl.BlockSpec((2 * KV_LANES, bt * steps), lambda i: (0, i)),
                  _resident((N_KV_HEADS, q_rows, KEY_TILE)), _resident((N_KV_HEADS, q_rows, KV_LANES))],
        out_specs=[batch_spec(q_rows), batch_spec(KV_LANES), batch_spec(KV_LANES)],
        out_shape=[jax.ShapeDtypeStruct((nb, q_rows, KV_LANES), F32),
                   jax.ShapeDtypeStruct((nb, KV_LANES, WINDOW), F32),
                   jax.ShapeDtypeStruct((nb, KV_LANES, WINDOW), F32)],
        compiler_params=_params(),
        name="attn_sample",
    )(q, kt, vt, kvn, bias, sink)


def kernel(x_prompt, x_sample, state_conv, cache_k, cache_v, g_mix, g_ffn, g_final, w_conv_in, conv_w,
           w_conv_out, w_q, w_k, w_v, w_o, sinks, rel_table, w_gate, w_up, w_down):
    batch, seq, _ = x_prompt.shape
    dec_batch, dec_seq, _ = x_sample.shape
    assert batch == 1 and seq % TOKEN_TILE == 0 and seq % ATTN_TILE == 0
    assert ATTN_TILE % (WINDOW * (D_MODEL // MXU_WIDTH)) == 0
    assert dec_batch % SAMPLE_BATCH_TILE == 0 and dec_seq <= SUBLANES
    assert dec_batch * dec_seq == TOKEN_TILE, "the sample rows ride the token-wise kernels as one tile"
    assert g_mix.shape[0] == 2, "layer 0 is the conv mixer, layer 1 the attention mixer"

    win = w_conv_in[0]
    wout = w_conv_out[0]
    wq = w_q[0].reshape(D_MODEL, N_KV_HEADS, GROUP, HEAD_DIM).transpose(0, 2, 1, 3).reshape(D_MODEL, D_MODEL)
    wqkv = jnp.concatenate([wq, w_k[0], w_v[0]], axis=1).astype(BF16)
    wo = w_o[0].reshape(N_KV_HEADS, GROUP, HEAD_DIM, D_MODEL).transpose(1, 0, 2, 3).reshape(D_MODEL, D_MODEL)
    wo = wo.astype(BF16)
    wkvt = jnp.concatenate([w_k[0], w_v[0]], axis=1).T.astype(BF16)
    gm = g_mix.reshape(2, 1, D_MODEL)
    gf = g_ffn.reshape(2, 1, D_MODEL)
    gfin = g_final.reshape(1, D_MODEL)
    cw = conv_w[0]
    sink_vec = sinks[0]

    bucket_p = jnp.asarray(_t5_bucket_np(
        (np.arange(WINDOW)[:, None] + WINDOW) - np.arange(KEY_TILE)[None, :]))
    key_s = np.arange(KEY_TILE)[None, :]
    key_pos = np.where(key_s < WINDOW, key_s, key_s - (KEY_TILE - dec_seq) + WINDOW)
    dist_s = np.arange(dec_seq)[:, None] + WINDOW - key_pos
    bucket_s = jnp.asarray(np.tile(_t5_bucket_np(dist_s), (GROUP, 1)))
    bias_p, bias_s, sink_s = _tables(bucket_p, bucket_s, rel_table, sink_vec, dec_seq)

    n_s = dec_batch * dec_seq
    prev = jnp.pad(state_conv[0], ((0, 0), (0, dec_seq - CONV_STATE), (0, 0))).reshape(n_s, D_MODEL)
    x, tail_p, u_s, wg, wu, wd = _conv(x_prompt.reshape(seq, D_MODEL), x_sample.reshape(n_s, D_MODEL), prev,
                                       gm[0], win, cw, wout, w_gate, w_up, w_down, dec_seq)
    state_conv_prompt = tail_p[SUBLANES - CONV_STATE:].reshape(1, batch, CONV_STATE, D_MODEL)
    state_conv_sample = u_s.reshape(dec_batch, dec_seq, D_MODEL)[:, dec_seq - CONV_STATE:][None]
    x = _ffn(x, gf, wg, wu, wd, gfin, 0)

    x_attn, k_last, v_last = _attn_prompt(x, gm[1], wqkv, wo, bias_p, sink_vec, seq)
    cache_k_prompt = k_last.reshape(1, batch, WINDOW, N_KV_HEADS, HEAD_DIM)
    cache_v_prompt = v_last.reshape(1, batch, WINDOW, N_KV_HEADS, HEAD_DIM)
    q_s, kv_new_t = _proj(x, gm[1], wqkv, wkvt, n_s)
    q_s = q_s.reshape(dec_batch, dec_seq, GROUP, KV_LANES).transpose(0, 2, 1, 3)
    q_s = q_s.reshape(dec_batch, GROUP * dec_seq, KV_LANES)
    kt = cache_k[0].transpose(0, 2, 3, 1).reshape(dec_batch, KV_LANES, WINDOW)
    vt = cache_v[0].transpose(0, 2, 3, 1).reshape(dec_batch, KV_LANES, WINDOW)
    o_s, kt_out, vt_out = _attn_sample(q_s, kt, vt, kv_new_t, bias_s, sink_s, dec_seq)
    o_s = o_s.reshape(dec_batch, GROUP, dec_seq, KV_LANES).transpose(0, 2, 1, 3).reshape(n_s, D_MODEL)
    x = _oproj(x, o_s, wo, x_attn)
    y_prompt, y_sample = _ffn_final(x, gf, wg, wu, wd, gfin, 1)
    y_prompt = y_prompt.reshape(batch, seq, D_MODEL)
    y_sample = y_sample.reshape(dec_batch, dec_seq, D_MODEL)
    cache_k_sample = kt_out.reshape(dec_batch, N_KV_HEADS, HEAD_DIM, WINDOW).transpose(0, 3, 1, 2)[None]
    cache_v_sample = vt_out.reshape(dec_batch, N_KV_HEADS, HEAD_DIM, WINDOW).transpose(0, 3, 1, 2)[None]

    return (y_prompt, y_sample, state_conv_prompt, state_conv_sample,
            cache_k_prompt, cache_k_sample, cache_v_prompt, cache_v_sample)
```

```python
import functools
import math

import numpy as np
import jax
import jax.numpy as jnp
from jax import lax
from jax.experimental import pallas as pl
from jax.experimental.pallas import tpu as pltpu

D_MODEL = 1024
D_FF = 2816
HEAD_DIM = 64
N_HEADS = 16
N_KV_HEADS = 2
GROUP = N_HEADS // N_KV_HEADS
WINDOW = 128
N_BUCKETS = 32
MAX_DISTANCE = 128
CONV_STATE = 2
EPS = 1e-5
NEG_INF = -1e30
ATTN_SCALE = 1.0 / math.sqrt(HEAD_DIM)

KV_LANES = N_KV_HEADS * HEAD_DIM
QKV_COLS = D_MODEL + 2 * KV_LANES
SUBLANES = 8
BF16_SUBLANES = 16
KEY_TILE = 2 * WINDOW

TOKEN_TILE = 512
ATTN_TILE = 512
MXU_WIDTH = 256
FFN_SPLIT = 2
CONV_SPLIT = 2
SAMPLE_BATCH_TILE = 32
VMEM_LIMIT_BYTES = 56 * 1024 * 1024

F32 = jnp.float32
BF16 = jnp.bfloat16


def _params():
    return pltpu.CompilerParams(dimension_semantics=("arbitrary",), vmem_limit_bytes=VMEM_LIMIT_BYTES)


def _resident(shape):
    zeros = (0,) * len(shape)
    return pl.BlockSpec(shape, lambda *_: zeros, pipeline_mode=pl.Buffered(1))


def _layer(shape, layer):
    index = (layer,) + (0,) * len(shape)
    return pl.BlockSpec((None,) + tuple(shape), lambda *_: index, pipeline_mode=pl.Buffered(1))


def _smem():
    return pl.BlockSpec(memory_space=pltpu.SMEM)


def _rms(x, g):
    return x * lax.rsqrt(jnp.mean(x * x, axis=-1, keepdims=True) + EPS) * g


def _pipeline_specs(n, rows):
    tiles = n // rows
    nxt = pl.BlockSpec((rows, D_MODEL), lambda i: (jnp.minimum(i, tiles - 1), 0))
    cur = pl.BlockSpec((rows, D_MODEL), lambda i: (jnp.maximum(i - 1, 0), 0))
    return tiles, nxt, cur


def _two_stage(first, body):
    step = pl.program_id(0)

    @pl.when(step == 0)
    def _():
        first(0)

    @pl.when(step % 2 == 1)
    def _():
        body(0, 1)

    @pl.when((step % 2 == 0) & (step > 0))
    def _():
        body(1, 0)


def _t5_bucket_np(dist):
    n = np.maximum(dist, 0)
    max_exact = N_BUCKETS // 2
    nf = np.maximum(n, 1).astype(np.float32)
    large = max_exact + (np.log(nf / np.float32(max_exact)) / np.float32(math.log(MAX_DISTANCE / max_exact))
                         * np.float32(N_BUCKETS - max_exact)).astype(np.int32)
    large = np.minimum(large, N_BUCKETS - 1)
    return np.where(n < max_exact, n, large).astype(np.int32)


def _ffn_kernel(x_ref, g_ref, wg_ref, wu_ref, wd_ref, gf_ref, *o_refs, final_norm):
    rows = x_ref.shape[0] // FFN_SPLIT
    halves = [pl.ds(i * rows, rows) for i in range(FFN_SPLIT)]
    acts = []
    for r in halves:
        h = _rms(x_ref[r, :], g_ref[...]).astype(BF16)
        gate = jnp.dot(h, wg_ref[...], preferred_element_type=F32)
        up = jnp.dot(h, wu_ref[...], preferred_element_type=F32)
        acts.append((gate * jax.nn.sigmoid(gate) * up).astype(BF16))
    for r, act in zip(halves, acts):
        y = x_ref[r, :] + jnp.dot(act, wd_ref[...], preferred_element_type=F32)
        if final_norm:
            y = _rms(y, gf_ref[...])
        for o_ref in o_refs:
            o_ref[r, :] = y


def _ffn_weight_specs(layer):
    return [_layer((1, D_MODEL), layer), _layer((D_MODEL, D_FF), layer), _layer((D_MODEL, D_FF), layer),
            _layer((D_FF, D_MODEL), layer), _resident((1, D_MODEL))]


def _ffn(x, g, wg, wu, wd, g_final, layer):
    n = x.shape[0]
    tile = pl.BlockSpec((TOKEN_TILE, D_MODEL), lambda i: (i, 0))
    return pl.pallas_call(
        functools.partial(_ffn_kernel, final_norm=False),
        grid=(n // TOKEN_TILE,),
        in_specs=[tile] + _ffn_weight_specs(layer),
        out_specs=tile,
        out_shape=jax.ShapeDtypeStruct((n, D_MODEL), F32),
        compiler_params=_params(),
        name="ffn",
    )(x, g, wg, wu, wd, g_final)


def _ffn_final(x, g, wg, wu, wd, g_final, layer):
    n = x.shape[0]
    tiles = n // TOKEN_TILE - 1
    shape = (TOKEN_TILE, D_MODEL)
    y_prompt, y_sample = pl.pallas_call(
        functools.partial(_ffn_kernel, final_norm=True),
        grid=(tiles + 1,),
        in_specs=[pl.BlockSpec(shape, lambda i: (jnp.where(i == 0, tiles, i - 1), 0))] + _ffn_weight_specs(layer),
        out_specs=[pl.BlockSpec(shape, lambda i: (jnp.maximum(i - 1, 0), 0)),
                   pl.BlockSpec(shape, lambda i: (jnp.minimum(i, 1), 0))],
        out_shape=[jax.ShapeDtypeStruct((tiles * TOKEN_TILE, D_MODEL), F32),
                   jax.ShapeDtypeStruct((2 * TOKEN_TILE, D_MODEL), F32)],
        compiler_params=_params(),
        name="ffn_final",
    )(x, g, wg, wu, wd, g_final)
    return y_prompt, y_sample[:TOKEN_TILE]


def _conv_gates(h, win_ref):
    cx = jnp.dot(h, win_ref[:, D_MODEL:], preferred_element_type=F32)
    u = cx[:, :D_MODEL] * cx[:, D_MODEL:]
    b = jnp.dot(h, win_ref[:, :D_MODEL], preferred_element_type=F32)
    return b, u


def _conv_kernel(x_ref, xs_ref, prev_ref, g_ref, win_ref, cw_ref, wout_ref, wg_ref, wu_ref, wd_ref,
                 o_ref, tail_out_ref, us_ref, wg_out_ref, wu_out_ref, wd_out_ref, tail_ref, *, steps):
    step = pl.program_id(0)
    last = pl.num_programs(0) - 1
    cw = cw_ref[...]

    @pl.when(step < last)
    def _():
        wg_out_ref[...] = wg_ref[...].astype(BF16)
        wu_out_ref[...] = wu_ref[...].astype(BF16)
        wd_out_ref[...] = wd_ref[...].astype(BF16)
        tail = jnp.where(step > 0, tail_ref[...], 0.0)
        rows = x_ref.shape[0] // CONV_SPLIT
        groups = [pl.ds(i * rows, rows) for i in range(CONV_SPLIT)]
        gates = [_conv_gates(_rms(x_ref[r, :], g_ref[...]), win_ref) for r in groups]
        r8 = lax.broadcasted_iota(jnp.int32, (SUBLANES, D_MODEL), 0)
        for r, (b, u) in zip(groups, gates):
            u1 = pltpu.roll(u, 1, 0)
            u2 = pltpu.roll(u, 2, 0)
            u1 = jnp.concatenate([jnp.where(r8 < 1, pltpu.roll(tail, 1, 0), u1[:SUBLANES]), u1[SUBLANES:]], axis=0)
            u2 = jnp.concatenate([jnp.where(r8 < 2, pltpu.roll(tail, 2, 0), u2[:SUBLANES]), u2[SUBLANES:]], axis=0)
            v = cw[0:1] * u2 + cw[1:2] * u1 + cw[2:3] * u
            o_ref[r, :] = x_ref[r, :] + jnp.dot(b * v, wout_ref[...], preferred_element_type=F32)
            tail = u[rows - SUBLANES:]
        tail_ref[...] = tail
        tail_out_ref[...] = tail

    @pl.when(step == last)
    def _():
        rows = xs_ref.shape[0] // CONV_SPLIT
        groups = [pl.ds(i * rows, rows) for i in range(CONV_SPLIT)]
        gates = [_conv_gates(_rms(xs_ref[r, :], g_ref[...]), win_ref) for r in groups]
        t = lax.broadcasted_iota(jnp.int32, (rows, D_MODEL), 0) % steps
        for r, (b, u) in zip(groups, gates):
            prev = prev_ref[r, :]
            u1 = jnp.where(t >= 1, pltpu.roll(u, 1, 0), pltpu.roll(prev, rows - 1, 0))
            u2 = jnp.where(t >= 2, pltpu.roll(u, 2, 0), prev)
            v = cw[0:1] * u2 + cw[1:2] * u1 + cw[2:3] * u
            o_ref[r, :] = xs_ref[r, :] + jnp.dot(b * v, wout_ref[...], preferred_element_type=F32)
            us_ref[r, :] = u


def _conv(x, xs, prev, g, win, cw, wout, w_gate, w_up, w_down, steps):
    n = x.shape[0]
    tiles = n // TOKEN_TILE
    assert xs.shape == (TOKEN_TILE, D_MODEL) and (TOKEN_TILE // CONV_SPLIT) % steps == 0
    tile_in =pl.BlockSpec((TOKEN_TILE, D_MODEL), lambda i: (jnp.minimum(i, tiles - 1), 0))
    tile_out = pl.BlockSpec((TOKEN_TILE, D_MODEL), lambda i: (i, 0))
    sample = _resident((TOKEN_TILE, D_MODEL))

    def slabs(w):
        flat = w.reshape(-1, w.shape[-1])
        rows = flat.shape[0] // tiles
        assert rows * tiles == flat.shape[0] and rows % BF16_SUBLANES == 0
        return flat, pl.BlockSpec((rows, flat.shape[1]), lambda i: (jnp.minimum(i, tiles - 1), 0))

    (wg, wg_spec), (wu, wu_spec), (wd, wd_spec) = slabs(w_gate), slabs(w_up), slabs(w_down)
    y, tail, u_s, wg_b, wu_b, wd_b = pl.pallas_call(
        functools.partial(_conv_kernel, steps=steps),
        grid=(tiles + 1,),
        in_specs=[tile_in, sample, sample, _resident((1, D_MODEL)), _resident((D_MODEL, 3 * D_MODEL)),
                  _resident((3, D_MODEL)), _resident((D_MODEL, D_MODEL)), wg_spec, wu_spec, wd_spec],
        out_specs=[tile_out, pl.BlockSpec((SUBLANES, D_MODEL), lambda i: (0, 0)),
                   pl.BlockSpec((TOKEN_TILE, D_MODEL), lambda i: (0, 0)), wg_spec, wu_spec, wd_spec],
        out_shape=[jax.ShapeDtypeStruct((n + TOKEN_TILE, D_MODEL), F32),
                   jax.ShapeDtypeStruct((SUBLANES, D_MODEL), F32),
                   jax.ShapeDtypeStruct((TOKEN_TILE, D_MODEL), F32),
                   jax.ShapeDtypeStruct(wg.shape, BF16),
                   jax.ShapeDtypeStruct(wu.shape, BF16),
                   jax.ShapeDtypeStruct(wd.shape, BF16)],
        scratch_shapes=[pltpu.VMEM((SUBLANES, D_MODEL), F32)],
        compiler_params=_params(),
        name="conv",
    )(x, xs, prev, g, win, cw, wout, wg, wu, wd)
    return y, tail, u_s, wg_b.reshape(w_gate.shape), wu_b.reshape(w_up.shape), wd_b.reshape(w_down.shape)


def _lookup_bias(bucket, table_ref, head):
    acc = jnp.zeros(bucket.shape, F32)
    for b in range(N_BUCKETS):
        acc = jnp.where(bucket == b, table_ref[b, head], acc)
    return acc


def _tables_kernel(bucket_p_ref, bucket_s_ref, table_ref, sinks_ref, bias_p_ref, bias_s_ref, sink_s_ref):
    bucket_row0 = bucket_p_ref[...]
    for head in range(N_HEADS):
        row0 = _lookup_bias(bucket_row0, table_ref, head)
        rows = jnp.concatenate([row0] * (WINDOW // SUBLANES), axis=0)
        bias_p_ref[head] = pltpu.roll(rows, 0, 1, stride=1, stride_axis=0)
    bucket_s = bucket_s_ref[...]
    grp = lax.broadcasted_iota(jnp.int32, bucket_s.shape, 0) % GROUP
    grp_l = lax.broadcasted_iota(jnp.int32, (bucket_s.shape[0], KV_LANES), 0) % GROUP
    for kv in range(N_KV_HEADS):
        bias = jnp.zeros(bucket_s.shape, F32)
        sink = jnp.zeros((bucket_s.shape[0], KV_LANES), F32)
        for gi in range(GROUP):
            head = kv * GROUP + gi
            bias = jnp.where(grp == gi, _lookup_bias(bucket_s, table_ref, head), bias)
            sink = jnp.where(grp_l == gi, sinks_ref[head], sink)
        bias_s_ref[kv] = bias
        sink_s_ref[kv] = sink


def _tables(bucket_p, bucket_s, table, sinks):
    q_rows = bucket_s.shape[0]
    return pl.pallas_call(
        _tables_kernel,
        grid=(1,),
        in_specs=[_resident((SUBLANES, KEY_TILE)), _resident((q_rows, KEY_TILE)), _smem(), _smem()],
        out_specs=[pl.BlockSpec((N_HEADS, WINDOW, KEY_TILE), lambda i: (0, 0, 0)),
                   pl.BlockSpec((N_KV_HEADS, q_rows, KEY_TILE), lambda i: (0, 0, 0)),
                   pl.BlockSpec((N_KV_HEADS, q_rows, KV_LANES), lambda i: (0, 0, 0))],
        out_shape=[jax.ShapeDtypeStruct((N_HEADS, WINDOW, KEY_TILE), F32),
                   jax.ShapeDtypeStruct((N_KV_HEADS, q_rows, KEY_TILE), F32),
                   jax.ShapeDtypeStruct((N_KV_HEADS, q_rows, KV_LANES), F32)],
        compiler_params=_params(),
        name="bias_tables",
    )(bucket_p, bucket_s, table, sinks)


def _attn_prompt_kernel(xn_ref, xc_ref, g_ref, wqkv_ref, wo_ref, bias_ref, sinks_ref,
                        o_ref, klast_ref, vlast_ref, qbuf, kbuf, vbuf, kcar, vcar, obuf):
    step = pl.program_id(0)
    rows = xn_ref.shape[0]
    n_blk = rows // WINDOW
    n_chunks = D_MODEL // MXU_WIDTH
    blocks_per_chunk = n_blk // n_chunks

    def norm_next():
        return _rms(xn_ref[...], g_ref[...]).astype(BF16)

    def project_chunk(h, slot, c):
        c0 = c * MXU_WIDTH
        part = jnp.dot(h, wqkv_ref[:, c0:c0 + MXU_WIDTH], preferred_element_type=F32)
        qbuf[slot, :, c0:c0 + MXU_WIDTH] = (part * ATTN_SCALE).astype(BF16)
        if c < n_chunks - 1:
            return
        kv_part = jnp.dot(h, wqkv_ref[:, D_MODEL:], preferred_element_type=F32)
        k = kv_part[:, :KV_LANES]
        v = kv_part[:, KV_LANES:]
        klast_ref[...] = k[rows - WINDOW:]
        vlast_ref[...] = v[rows - WINDOW:]
        lane = lax.broadcasted_iota(jnp.int32, (rows, KV_LANES), 1)
        for kv, half in enumerate((lane < HEAD_DIM, lane >= HEAD_DIM)):
            k_half = jnp.where(half, k, 0.0)
            v_half = jnp.where(half, v, 0.0)
            kbuf[slot, kv] = k_half.astype(BF16)
            vbuf[slot, kv] = v_half.astype(BF16)
            kcar[slot, kv] = k_half[rows - WINDOW:]
            vcar[slot, kv] = v_half[rows - WINDOW:]

    def first(slot):
        o_ref[...] = jnp.zeros(o_ref.shape, F32)
        kcar[...] = jnp.zeros(kcar.shape, F32)
        vcar[...] = jnp.zeros(vcar.shape, F32)
        h = norm_next()
        for c in range(n_chunks):
            project_chunk(h, slot, c)

    def body(cur, nxt):
        k_prev = [jnp.where(step > 1, kcar[nxt, kv], 0.0).astype(BF16) for kv in range(N_KV_HEADS)]
        v_prev = [jnp.where(step > 1, vcar[nxt, kv], 0.0).astype(BF16) for kv in range(N_KV_HEADS)]
        h = norm_next()

        qi = lax.broadcasted_iota(jnp.int32, (WINDOW, KEY_TILE), 0)
        kj = lax.broadcasted_iota(jnp.int32, (WINDOW, KEY_TILE), 1)
        band = (kj >= qi) & (kj <= qi + WINDOW)
        lane_q = lax.broadcasted_iota(jnp.int32, (WINDOW, KV_LANES), 1)
        lane_k = lax.broadcasted_iota(jnp.int32, (KEY_TILE, KV_LANES), 1)
        ones_half = [(lane_k < HEAD_DIM).astype(F32).astype(BF16), (lane_k >= HEAD_DIM).astype(F32).astype(BF16)]

        def block_scores(blk):
            r0 = blk * WINDOW
            qs = jnp.concatenate(
                [qbuf[cur, r0:r0 + WINDOW, gi * KV_LANES:(gi + 1) * KV_LANES] for gi in range(GROUP)], axis=0)
            if blk == 0:
                keys = [jnp.concatenate([k_prev[kv], kbuf[cur, kv, 0:WINDOW, :]], axis=0) for kv in range(N_KV_HEADS)]
            else:
                keys = [kbuf[cur, kv, r0 - WINDOW:r0 + WINDOW, :] for kv in range(N_KV_HEADS)]
            return [lax.dot_general(qs, keys[kv], (((1,), (1,)), ((), ())), preferred_element_type=F32)
                    for kv in range(N_KV_HEADS)]

        scores = block_scores(0)
        for blk in range(n_blk):
            r0 = blk * WINDOW
            if blk == 0:
                mask = band & (kj >= jnp.where(step > 1, 0, WINDOW))
                vals = [jnp.concatenate([v_prev[kv], vbuf[cur, kv, 0:WINDOW, :]], axis=0) for kv in range(N_KV_HEADS)]
            else:
                mask = band
                vals = [vbuf[cur, kv, r0 - WINDOW:r0 + WINDOW, :] for kv in range(N_KV_HEADS)]
            next_scores = block_scores(blk + 1) if blk + 1 < n_blk else None
            if blk % blocks_per_chunk == 0:
                project_chunk(h, nxt, blk // blocks_per_chunk)
            probs, maxes = [], []
            for kv in range(N_KV_HEADS):
                s_all = scores[kv]
                p_kv, m_kv = [], []
                for gi in range(GROUP):
                    head = kv * GROUP + gi
                    s = s_all[gi * WINDOW:(gi + 1) * WINDOW]
                    s = jnp.where(mask, s + bias_ref[head], NEG_INF)
                    m = jnp.maximum(jnp.max(s, axis=-1, keepdims=True), sinks_ref[head])
                    p_kv.append(jnp.exp(s - m).astype(BF16))
                    m_kv.append(m)
                probs.append(jnp.concatenate(p_kv, axis=0))
                maxes.append(m_kv)
            scores = next_scores
            acc = None
            for kv in range(N_KV_HEADS):
                rhs = jnp.concatenate([vals[kv], ones_half[kv]], axis=1)
                part = jnp.dot(probs[kv], rhs, preferred_element_type=F32)
                acc = part if acc is None else acc + part
            outs = []
            for gi in range(GROUP):
                a = acc[gi * WINDOW:(gi + 1) * WINDOW]
                sink_term = jnp.where(lane_q < HEAD_DIM,
                                      jnp.exp(sinks_ref[gi] - maxes[0][gi]),
                                      jnp.exp(sinks_ref[GROUP + gi] - maxes[1][gi]))
                outs.append(a[:, :KV_LANES] / (a[:, KV_LANES:] + sink_term))
            obuf[r0:r0 + WINDOW, :] = jnp.concatenate(outs, axis=1).astype(BF16)

        o_ref[...] = xc_ref[...] + jnp.dot(obuf[...], wo_ref[...], preferred_element_type=F32)

    _two_stage(first, body)


def _attn_prompt(x, g, wqkv, wo, bias, sinks, n):
    tiles, nxt, cur = _pipeline_specs(n, ATTN_TILE)
    assert x.shape[0] == n + ATTN_TILE
    out = pl.BlockSpec((ATTN_TILE, D_MODEL), lambda i: (jnp.where(i == 0, tiles, i - 1), 0))
    last = pl.BlockSpec((WINDOW, KV_LANES), lambda i: (0, 0))
    return pl.pallas_call(
        _attn_prompt_kernel,
        grid=(tiles + 1,),
        in_specs=[nxt, cur, _resident((1, D_MODEL)), _resident((D_MODEL, QKV_COLS)),
                  _resident((D_MODEL, D_MODEL)), _resident((N_HEADS, WINDOW, KEY_TILE)), _smem()],
        out_specs=[out, last, last],
        out_shape=[jax.ShapeDtypeStruct(x.shape, F32),
                   jax.ShapeDtypeStruct((WINDOW, KV_LANES), F32),
                   jax.ShapeDtypeStruct((WINDOW, KV_LANES), F32)],
        scratch_shapes=[pltpu.VMEM((2, ATTN_TILE, D_MODEL), BF16),
                        pltpu.VMEM((2, N_KV_HEADS, ATTN_TILE, KV_LANES), BF16),
                        pltpu.VMEM((2, N_KV_HEADS, ATTN_TILE, KV_LANES), BF16),
                        pltpu.VMEM((2, N_KV_HEADS, WINDOW, KV_LANES), F32),
                        pltpu.VMEM((2, N_KV_HEADS, WINDOW, KV_LANES), F32),
                        pltpu.VMEM((ATTN_TILE, D_MODEL), BF16)],
        compiler_params=_params(),
        name="attn_prompt",
    )(x, x, g, wqkv, wo, bias, sinks)


def _proj_kernel(x_ref, g_ref, wq_ref, wkvt_ref, q_ref, kvt_ref):
    h = _rms(x_ref[...], g_ref[...]).astype(BF16)
    q_ref[...] = jnp.dot(h, wq_ref[...], preferred_element_type=F32)
    kvt_ref[...] = lax.dot_general(wkvt_ref[...], h, (((1,), (1,)), ((), ())), preferred_element_type=F32)


def _proj(x, g, wqkv, wkvt, n):
    last_tile = x.shape[0] // n - 1
    return pl.pallas_call(
        _proj_kernel,
        grid=(1,),
        in_specs=[pl.BlockSpec((n, D_MODEL), lambda i: (last_tile, 0)), _resident((1, D_MODEL)),
                  _resident((D_MODEL, D_MODEL)), _resident((2 * KV_LANES, D_MODEL))],
        out_specs=[pl.BlockSpec((n, D_MODEL), lambda i: (0, 0)), pl.BlockSpec((2 * KV_LANES, n), lambda i: (0, 0))],
        out_shape=[jax.ShapeDtypeStruct((n, D_MODEL), F32), jax.ShapeDtypeStruct((2 * KV_LANES, n), F32)],
        compiler_params=_params(),
        name="proj_sample",
    )(x, g, wqkv, wkvt)


def _oproj_kernel(x_ref, a_ref, w_ref, all_ref, o_ref):
    del all_ref
    o_ref[...] = x_ref[...] + jnp.dot(a_ref[...].astype(BF16), w_ref[...], preferred_element_type=F32)


def _oproj(x, a, w, combined):
    n = a.shape[0]
    last_tile = x.shape[0] // n - 1
    sample = pl.BlockSpec((n, D_MODEL), lambda i: (last_tile, 0))
    return pl.pallas_call(
        _oproj_kernel,
        grid=(1,),
        in_specs=[sample, _resident((n, D_MODEL)), _resident((D_MODEL, D_MODEL)),
                  pl.BlockSpec(memory_space=pl.ANY)],
        out_specs=sample,
        out_shape=jax.ShapeDtypeStruct(combined.shape, F32),
        input_output_aliases={3: 0},
        compiler_params=_params(),
        name="oproj_sample",
    )(x, a, w, combined)


def _attn_sample_kernel(q_ref, kt_ref, vt_ref, kvn_ref, bias_ref, sink_ref, o_ref, kout_ref, vout_ref, *, steps):
    q_rows = GROUP * steps
    qi = lax.broadcasted_iota(jnp.int32, (q_rows, KEY_TILE), 0) // GROUP
    kj = lax.broadcasted_iota(jnp.int32, (q_rows, KEY_TILE), 1)
    new_at = KEY_TILE - steps
    mask = ((kj < WINDOW) & (kj >= qi)) | ((kj >= new_at) & (kj - new_at <= qi))
    row = lax.broadcasted_iota(jnp.int32, (KV_LANES, KEY_TILE), 0)
    kv_row = (row < HEAD_DIM, row >= HEAD_DIM)
    lane_q = lax.broadcasted_iota(jnp.int32, (q_rows, KV_LANES), 1)
    lane_c = lax.broadcasted_iota(jnp.int32, (KV_LANES, WINDOW), 1)
    new_k = kvn_ref[0:KV_LANES, :]
    new_v = kvn_ref[KV_LANES:, :]

    def new_cols(new_t, b):
        return pltpu.roll(new_t, (WINDOW - steps - b * steps) % WINDOW, 1)

    def shifted_cache(cache_t, new_last):
        return jnp.where(lane_c < WINDOW - steps, pltpu.roll(cache_t, WINDOW - steps, 1), new_last)

    batches = range(q_ref.shape[0])
    scores = []
    for b in batches:
        kt = kt_ref[b]
        new_last = new_cols(new_k, b)
        kout_ref[b] = shifted_cache(kt, new_last)
        keys = jnp.concatenate([kt, new_last], axis=1)
        q = (q_ref[b] * ATTN_SCALE).astype(BF16)
        scores.append([jnp.dot(q, jnp.where(kv_row[kv], keys, 0.0).astype(BF16), preferred_element_type=F32)
                       for kv in range(N_KV_HEADS)])
    probs, sink_terms = [], []
    for b in batches:
        p_b, t_b = [], []
        for kv in range(N_KV_HEADS):
            s = jnp.where(mask, scores[b][kv] + bias_ref[kv], NEG_INF)
            sink = sink_ref[kv][:, 0:1]
            m = jnp.maximum(jnp.max(s, axis=-1, keepdims=True), sink)
            p_b.append(jnp.exp(s - m).astype(BF16))
            t_b.append(jnp.exp(sink - m))
        probs.append(p_b)
        sink_terms.append(t_b)
    for b in batches:
        vt = vt_ref[b]
        new_last = new_cols(new_v, b)
        vout_ref[b] = shifted_cache(vt, new_last)
        vals = jnp.concatenate([vt, new_last], axis=1)
        acc = None
        for kv in range(N_KV_HEADS):
            rhs = jnp.concatenate([jnp.where(kv_row[kv], vals, 0.0), jnp.where(kv_row[kv], 1.0, 0.0)], axis=0)
            part = lax.dot_general(probs[b][kv], rhs.astype(BF16), (((1,), (1,)), ((), ())),
                                   preferred_element_type=F32)
            acc = part if acc is None else acc + part
        denom = acc[:, KV_LANES:] + jnp.where(lane_q < HEAD_DIM, sink_terms[b][0], sink_terms[b][1])
        o_ref[b] = acc[:, :KV_LANES] / denom


def _attn_sample(q, kt, vt, kvn, bias, sink, steps):
    nb = q.shape[0]
    q_rows = GROUP * steps
    bt = SAMPLE_BATCH_TILE
    assert bt * steps == WINDOW and nb % bt == 0

    def batch_spec(r):
        return pl.BlockSpec((bt, r, KV_LANES), lambda i: (i, 0, 0))

    return pl.pallas_call(
        functools.partial(_attn_sample_kernel, steps=steps),
        grid=(nb // bt,),
        in_specs=[batch_spec(q_rows), batch_spec(KV_LANES), batch_spec(KV_LANES),
                  pl.BlockSpec((2 * KV_LANES, bt * steps), lambda i: (0, i)),
                  _resident((N_KV_HEADS, q_rows, KEY_TILE)), _resident((N_KV_HEADS, q_rows, KV_LANES))],
        out_specs=[batch_spec(q_rows), batch_spec(KV_LANES), batch_spec(KV_LANES)],
        out_shape=[jax.ShapeDtypeStruct((nb, q_rows, KV_LANES), F32),
                   jax.ShapeDtypeStruct((nb, KV_LANES, WINDOW), F32),
                   jax.ShapeDtypeStruct((nb, KV_LANES, WINDOW), F32)],
        compiler_params=_params(),
        name="attn_sample",
    )(q, kt, vt, kvn, bias, sink)


def kernel(x_prompt, x_sample, state_conv, cache_k, cache_v, g_mix, g_ffn, g_final, w_conv_in, conv_w,
           w_conv_out, w_q, w_k, w_v, w_o, sinks, rel_table, w_gate, w_up, w_down):
    batch, seq, _ = x_prompt.shape
    dec_batch, dec_seq, _ = x_sample.shape
    assert batch == 1 and seq % TOKEN_TILE == 0 and seq % ATTN_TILE == 0
    assert ATTN_TILE % (WINDOW * (D_MODEL // MXU_WIDTH)) == 0
    assert dec_batch % SAMPLE_BATCH_TILE == 0 and dec_seq <= SUBLANES
    assert dec_batch * dec_seq == TOKEN_TILE, "the sample rows ride the token-wise kernels as one tile"
    assert g_mix.shape[0] == 2, "layer 0 is the conv mixer, layer 1 the attention mixer"

    win = w_conv_in[0]
    wout = w_conv_out[0]
    wq = w_q[0].reshape(D_MODEL, N_KV_HEADS, GROUP, HEAD_DIM).transpose(0, 2, 1, 3).reshape(D_MODEL, D_MODEL)
    wqkv = jnp.concatenate([wq, w_k[0], w_v[0]], axis=1).astype(BF16)
    wo = w_o[0].reshape(N_KV_HEADS, GROUP, HEAD_DIM, D_MODEL).transpose(1, 0, 2, 3).reshape(D_MODEL, D_MODEL)
    wo = wo.astype(BF16)
    wkvt = jnp.concatenate([w_k[0], w_v[0]], axis=1).T.astype(BF16)
    gm = g_mix.reshape(2, 1, D_MODEL)
    gf = g_ffn.reshape(2, 1, D_MODEL)
    gfin = g_final.reshape(1, D_MODEL)
    cw = conv_w[0]
    sink_vec = sinks[0]

    bucket_p = jnp.asarray(np.tile(_t5_bucket_np(WINDOW - np.arange(KEY_TILE)[None, :]), (SUBLANES, 1)))
    key_s = np.arange(KEY_TILE)[None, :]
    key_pos = np.where(key_s < WINDOW, key_s, key_s - (KEY_TILE - dec_seq) + WINDOW)
    dist_s = np.arange(dec_seq)[:, None] + WINDOW - key_pos
    bucket_s = jnp.asarray(np.repeat(_t5_bucket_np(dist_s), GROUP, axis=0))
    bias_p, bias_s, sink_s = _tables(bucket_p, bucket_s, rel_table, sink_vec)

    n_s = dec_batch * dec_seq
    prev = jnp.pad(state_conv[0], ((0, 0), (0, dec_seq - CONV_STATE), (0, 0))).reshape(n_s, D_MODEL)
    x, tail_p, u_s, wg, wu, wd = _conv(x_prompt.reshape(seq, D_MODEL), x_sample.reshape(n_s, D_MODEL), prev,
                                       gm[0], win, cw, wout, w_gate, w_up, w_down, dec_seq)
    state_conv_prompt = tail_p[SUBLANES - CONV_STATE:].reshape(1, batch, CONV_STATE, D_MODEL)
    state_conv_sample = u_s.reshape(dec_batch, dec_seq, D_MODEL)[:, dec_seq - CONV_STATE:][None]
    x = _ffn(x, gf, wg, wu, wd, gfin, 0)

    x_attn, k_last, v_last = _attn_prompt(x, gm[1], wqkv, wo, bias_p, sink_vec, seq)
    cache_k_prompt = k_last.reshape(1, batch, WINDOW, N_KV_HEADS, HEAD_DIM)
    cache_v_prompt = v_last.reshape(1, batch, WINDOW, N_KV_HEADS, HEAD_DIM)
    q_s, kv_new_t = _proj(x, gm[1], wqkv, wkvt, n_s)
    q_s = q_s.reshape(dec_batch, dec_seq * GROUP, KV_LANES)
    kt = cache_k[0].transpose(0, 2, 3, 1).reshape(dec_batch, KV_LANES, WINDOW)
    vt = cache_v[0].transpose(0, 2, 3, 1).reshape(dec_batch, KV_LANES, WINDOW)
    o_s, kt_out, vt_out = _attn_sample(q_s, kt, vt, kv_new_t, bias_s, sink_s, dec_seq)
    o_s = o_s.reshape(n_s, D_MODEL)
    x = _oproj(x, o_s, wo, x_attn)
    y_prompt, y_sample = _ffn_final(x, gf, wg, wu, wd, gfin, 1)
    y_prompt = y_prompt.reshape(batch, seq, D_MODEL)
    y_sample = y_sample.reshape(dec_batch, dec_seq, D_MODEL)
    cache_k_sample = kt_out.reshape(dec_batch, N_KV_HEADS, HEAD_DIM, WINDOW).transpose(0, 3, 1, 2)[None]
    cache_v_sample = vt_out.reshape(dec_batch, N_KV_HEADS, HEAD_DIM, WINDOW).transpose(0, 3, 1, 2)[None]

    return (y_prompt, y_sample, state_conv_prompt, state_conv_sample,
            cache_k_prompt, cache_k_sample, cache_v_prompt, cache_v_sample)
```

```python
import functools
import math

import numpy as np
import jax
import jax.numpy as jnp
from jax import lax
from jax.experimental import pallas as pl
from jax.experimental.pallas import tpu as pltpu

D_MODEL = 1024
D_FF = 2816
HEAD_DIM = 64
N_HEADS = 16
N_KV_HEADS = 2
GROUP = N_HEADS // N_KV_HEADS
WINDOW = 128
N_BUCKETS = 32
MAX_DISTANCE = 128
CONV_STATE = 2
EPS = 1e-5
NEG_INF = -1e30
ATTN_SCALE = 1.0 / math.sqrt(HEAD_DIM)

KV_LANES = N_KV_HEADS * HEAD_DIM
QKV_COLS = D_MODEL + 2 * KV_LANES
SUBLANES = 8
BF16_SUBLANES = 16
KEY_TILE = 2 * WINDOW

TOKEN_TILE = 512
ATTN_TILE = 512
MXU_WIDTH = 256
FFN_SPLIT = 2
CONV_SPLIT = 2
SAMPLE_BATCH_TILE = 32
VMEM_LIMIT_BYTES = 56 * 1024 * 1024

F32 = jnp.float32
BF16 = jnp.bfloat16


def _params():
    return pltpu.CompilerParams(dimension_semantics=("arbitrary",), vmem_limit_bytes=VMEM_LIMIT_BYTES)


def _resident(shape):
    zeros = (0,) * len(shape)
    return pl.BlockSpec(shape, lambda *_: zeros, pipeline_mode=pl.Buffered(1))


def _layer(shape, layer):
    index = (layer,) + (0,) * len(shape)
    return pl.BlockSpec((None,) + tuple(shape), lambda *_: index, pipeline_mode=pl.Buffered(1))


def _smem():
    return pl.BlockSpec(memory_space=pltpu.SMEM)


def _rms(x, g):
    return x * lax.rsqrt(jnp.mean(x * x, axis=-1, keepdims=True) + EPS) * g


def _pipeline_specs(n, rows):
    tiles = n // rows
    nxt = pl.BlockSpec((rows, D_MODEL), lambda i: (jnp.minimum(i, tiles - 1), 0))
    cur = pl.BlockSpec((rows, D_MODEL), lambda i: (jnp.maximum(i - 1, 0), 0))
    return tiles, nxt, cur


def _two_stage(first, body):
    step = pl.program_id(0)

    @pl.when(step == 0)
    def _():
        first(0)

    @pl.when(step % 2 == 1)
    def _():
        body(0, 1)

    @pl.when((step % 2 == 0) & (step > 0))
    def _():
        body(1, 0)


def _t5_bucket_np(dist):
    n = np.maximum(dist, 0)
    max_exact = N_BUCKETS // 2
    nf = np.maximum(n, 1).astype(np.float32)
    large = max_exact + (np.log(nf / np.float32(max_exact)) / np.float32(math.log(MAX_DISTANCE / max_exact))
                         * np.float32(N_BUCKETS - max_exact)).astype(np.int32)
    large = np.minimum(large, N_BUCKETS - 1)
    return np.where(n < max_exact, n, large).astype(np.int32)


def _ffn_kernel(x_ref, g_ref, wg_ref, wu_ref, wd_ref, gf_ref, *o_refs, final_norm):
    rows = x_ref.shape[0] // FFN_SPLIT
    halves = [pl.ds(i * rows, rows) for i in range(FFN_SPLIT)]
    acts = []
    for r in halves:
        h = _rms(x_ref[r, :], g_ref[...]).astype(BF16)
        gate = jnp.dot(h, wg_ref[...], preferred_element_type=F32)
        up = jnp.dot(h, wu_ref[...], preferred_element_type=F32)
        acts.append((gate * jax.nn.sigmoid(gate) * up).astype(BF16))
    for r, act in zip(halves, acts):
        y = x_ref[r, :] + jnp.dot(act, wd_ref[...], preferred_element_type=F32)
        if final_norm:
            y = _rms(y, gf_ref[...])
        for o_ref in o_refs:
            o_ref[r, :] = y


def _ffn_weight_specs(layer):
    return [_layer((1, D_MODEL), layer), _layer((D_MODEL, D_FF), layer), _layer((D_MODEL, D_FF), layer),
            _layer((D_FF, D_MODEL), layer), _resident((1, D_MODEL))]


def _ffn_proj_kernel(x_ref, g_ref, wg_ref, wu_ref, wd_ref, gf_ref, gq_ref, wq_ref, wkvt_ref,
                     o_ref, q_ref, kvt_ref):
    _ffn_kernel(x_ref, g_ref, wg_ref, wu_ref, wd_ref, gf_ref, o_ref, final_norm=False)

    @pl.when(pl.program_id(0) == pl.num_programs(0) - 1)
    def _():
        h = _rms(o_ref[...], gq_ref[...]).astype(BF16)
        q_ref[...] = jnp.dot(h, wq_ref[...], preferred_element_type=F32)
        kvt_ref[...] = lax.dot_general(wkvt_ref[...], h, (((1,), (1,)), ((), ())), preferred_element_type=F32)


def _ffn(x, g, wg, wu, wd, g_final, layer, g_q, wqkv, wkvt):
    n = x.shape[0]
    tile = pl.BlockSpec((TOKEN_TILE, D_MODEL), lambda i: (i, 0))

    def once(shape):
        return pl.BlockSpec(shape, lambda i: (0, 0))

    return pl.pallas_call(
        _ffn_proj_kernel,
        grid=(n // TOKEN_TILE,),
        in_specs=[tile] + _ffn_weight_specs(layer) + [_resident((1, D_MODEL)), _resident((D_MODEL, D_MODEL)),
                                                      _resident((2 * KV_LANES, D_MODEL))],
        out_specs=[tile, once((TOKEN_TILE, D_MODEL)), once((2 * KV_LANES, TOKEN_TILE))],
        out_shape=[jax.ShapeDtypeStruct((n, D_MODEL), F32),
                   jax.ShapeDtypeStruct((TOKEN_TILE, D_MODEL), F32),
                   jax.ShapeDtypeStruct((2 * KV_LANES, TOKEN_TILE), F32)],
        compiler_params=_params(),
        name="ffn",
    )(x, g, wg, wu, wd, g_final, g_q, wqkv, wkvt)


def _ffn_final(x, g, wg, wu, wd, g_final, layer):
    n = x.shape[0]
    tiles = n // TOKEN_TILE - 1
    shape = (TOKEN_TILE, D_MODEL)
    y_prompt, y_sample = pl.pallas_call(
        functools.partial(_ffn_kernel, final_norm=True),
        grid=(tiles + 1,),
        in_specs=[pl.BlockSpec(shape, lambda i: (jnp.where(i == 0, tiles, i - 1), 0))] + _ffn_weight_specs(layer),
        out_specs=[pl.BlockSpec(shape, lambda i: (jnp.maximum(i - 1, 0), 0)),
                   pl.BlockSpec(shape, lambda i: (jnp.minimum(i, 1), 0))],
        out_shape=[jax.ShapeDtypeStruct((tiles * TOKEN_TILE, D_MODEL), F32),
                   jax.ShapeDtypeStruct((2 * TOKEN_TILE, D_MODEL), F32)],
        compiler_params=_params(),
        name="ffn_final",
    )(x, g, wg, wu, wd, g_final)
    return y_prompt, y_sample[:TOKEN_TILE]


def _conv_gates(h, win_ref):
    cx = jnp.dot(h, win_ref[:, D_MODEL:], preferred_element_type=F32)
    u = cx[:, :D_MODEL] * cx[:, D_MODEL:]
    b = jnp.dot(h, win_ref[:, :D_MODEL], preferred_element_type=F32)
    return b, u


def _conv_kernel(x_ref, xs_ref, prev_ref, g_ref, win_ref, cw_ref, wout_ref, wg_ref, wu_ref, wd_ref,
                 bucket_p_ref, bucket_s_ref, table_ref, sinks_ref,
                 o_ref, tail_out_ref, us_ref, wg_out_ref, wu_out_ref, wd_out_ref,
                 bias_p_ref, bias_s_ref, sink_s_ref, tail_ref, *, steps):
    step = pl.program_id(0)
    last = pl.num_programs(0) - 1
    cw = cw_ref[...]

    @pl.when(step < last)
    def _():
        wg_out_ref[...] = wg_ref[...].astype(BF16)
        wu_out_ref[...] = wu_ref[...].astype(BF16)
        wd_out_ref[...] = wd_ref[...].astype(BF16)
        tail = jnp.where(step > 0, tail_ref[...], 0.0)
        rows = x_ref.shape[0] // CONV_SPLIT
        groups = [pl.ds(i * rows, rows) for i in range(CONV_SPLIT)]
        gates = [_conv_gates(_rms(x_ref[r, :], g_ref[...]), win_ref) for r in groups]
        r8 = lax.broadcasted_iota(jnp.int32, (SUBLANES, D_MODEL), 0)
        for r, (b, u) in zip(groups, gates):
            u1 = pltpu.roll(u, 1, 0)
            u2 = pltpu.roll(u, 2, 0)
            u1 = jnp.concatenate([jnp.where(r8 < 1, pltpu.roll(tail, 1, 0), u1[:SUBLANES]), u1[SUBLANES:]], axis=0)
            u2 = jnp.concatenate([jnp.where(r8 < 2, pltpu.roll(tail, 2, 0), u2[:SUBLANES]), u2[SUBLANES:]], axis=0)
            v = cw[0:1] * u2 + cw[1:2] * u1 + cw[2:3] * u
            o_ref[r, :] = x_ref[r, :] + jnp.dot(b * v, wout_ref[...], preferred_element_type=F32)
            tail = u[rows - SUBLANES:]
        tail_ref[...] = tail
        tail_out_ref[...] = tail

    @pl.when(step == last)
    def _():
        x = xs_ref[...]
        b, u = _conv_gates(_rms(x, g_ref[...]), win_ref)
        rows = u.shape[0]
        prev = prev_ref[...]
        t = lax.broadcasted_iota(jnp.int32, u.shape, 0) % steps
        u1 = jnp.where(t >= 1, pltpu.roll(u, 1, 0), pltpu.roll(prev, rows - 1, 0))
        u2 = jnp.where(t >= 2, pltpu.roll(u, 2, 0), prev)
        v = cw[0:1] * u2 + cw[1:2] * u1 + cw[2:3] * u
        o_ref[...] = x + jnp.dot(b * v, wout_ref[...], preferred_element_type=F32)
        us_ref[...] = u
        _bias_tables(bucket_p_ref, bucket_s_ref, table_ref, sinks_ref, bias_p_ref, bias_s_ref, sink_s_ref)


def _conv(x, xs, prev, g, win, cw, wout, w_gate, w_up, w_down, steps, bucket_p, bucket_s, table, sinks):
    n = x.shape[0]
    tiles = n // TOKEN_TILE
    assert xs.shape == (TOKEN_TILE, D_MODEL)
    tile_in = pl.BlockSpec((TOKEN_TILE, D_MODEL), lambda i: (jnp.minimum(i, tiles - 1), 0))
    tile_out = pl.BlockSpec((TOKEN_TILE, D_MODEL), lambda i: (i, 0))
    sample = _resident((TOKEN_TILE, D_MODEL))

    def slabs(w):
        flat = w.reshape(-1, w.shape[-1])
        rows = flat.shape[0] // tiles
        assert rows * tiles == flat.shape[0] and rows % BF16_SUBLANES == 0
        return flat, pl.BlockSpec((rows, flat.shape[1]), lambda i: (jnp.minimum(i, tiles - 1), 0))

    (wg, wg_spec), (wu, wu_spec), (wd, wd_spec) = slabs(w_gate), slabs(w_up), slabs(w_down)
    q_rows = bucket_s.shape[0]
    table_shapes = [(N_HEADS, WINDOW, KEY_TILE), (N_KV_HEADS, q_rows, KEY_TILE), (N_KV_HEADS, q_rows, KV_LANES)]
    y, tail, u_s, wg_b, wu_b, wd_b, bias_p, bias_s, sink_s = pl.pallas_call(
        functools.partial(_conv_kernel, steps=steps),
        grid=(tiles + 1,),
        in_specs=[tile_in, sample, sample, _resident((1, D_MODEL)), _resident((D_MODEL, 3 * D_MODEL)),
                  _resident((3, D_MODEL)), _resident((D_MODEL, D_MODEL)), wg_spec, wu_spec, wd_spec,
                  _resident((SUBLANES, KEY_TILE)), _resident((q_rows, KEY_TILE)), _smem(), _smem()],
        out_specs=[tile_out, pl.BlockSpec((SUBLANES, D_MODEL), lambda i: (0, 0)),
                   pl.BlockSpec((TOKEN_TILE, D_MODEL), lambda i: (0, 0)), wg_spec, wu_spec, wd_spec]
        + [pl.BlockSpec(shape, lambda i: (0, 0, 0)) for shape in table_shapes],
        out_shape=[jax.ShapeDtypeStruct((n + TOKEN_TILE, D_MODEL), F32),
                   jax.ShapeDtypeStruct((SUBLANES, D_MODEL), F32),
                   jax.ShapeDtypeStruct((TOKEN_TILE, D_MODEL), F32),
                   jax.ShapeDtypeStruct(wg.shape, BF16),
                   jax.ShapeDtypeStruct(wu.shape, BF16),
                   jax.ShapeDtypeStruct(wd.shape, BF16)]
        + [jax.ShapeDtypeStruct(shape, F32) for shape in table_shapes],
        scratch_shapes=[pltpu.VMEM((SUBLANES, D_MODEL), F32)],
        compiler_params=_params(),
        name="conv",
    )(x, xs, prev, g, win, cw, wout, wg, wu, wd, bucket_p, bucket_s, table, sinks)
    weights = (wg_b.reshape(w_gate.shape), wu_b.reshape(w_up.shape), wd_b.reshape(w_down.shape))
    return y, tail, u_s, weights, (bias_p, bias_s, sink_s)


def _lookup_bias(bucket, table_ref, head):
    acc = jnp.zeros(bucket.shape, F32)
    for b in range(N_BUCKETS):
        acc = jnp.where(bucket == b, table_ref[b, head], acc)
    return acc


def _bias_tables(bucket_p_ref, bucket_s_ref, table_ref, sinks_ref, bias_p_ref, bias_s_ref, sink_s_ref):
    bucket_row0 = bucket_p_ref[...]
    for head in range(N_HEADS):
        row0 = _lookup_bias(bucket_row0, table_ref, head)
        rows = jnp.concatenate([row0] * (WINDOW // SUBLANES), axis=0)
        bias_p_ref[head] = pltpu.roll(rows, 0, 1, stride=1, stride_axis=0)
    bucket_s = bucket_s_ref[...]
    grp = lax.broadcasted_iota(jnp.int32, bucket_s.shape, 0) % GROUP
    grp_l = lax.broadcasted_iota(jnp.int32, (bucket_s.shape[0], KV_LANES), 0) % GROUP
    for kv in range(N_KV_HEADS):
        bias = jnp.zeros(bucket_s.shape, F32)
        sink = jnp.zeros((bucket_s.shape[0], KV_LANES), F32)
        for gi in range(GROUP):
            head = kv * GROUP + gi
            bias = jnp.where(grp == gi, _lookup_bias(bucket_s, table_ref, head), bias)
            sink = jnp.where(grp_l == gi, sinks_ref[head], sink)
        bias_s_ref[kv] = bias
        sink_s_ref[kv] = sink


def _attn_prompt_kernel(xn_ref, xc_ref, g_ref, wqkv_ref, wo_ref, bias_ref, sinks_ref,
                        o_ref, klast_ref, vlast_ref, qbuf, kbuf, vbuf, kcar, vcar, obuf):
    step = pl.program_id(0)
    rows = xn_ref.shape[0]
    n_blk = rows // WINDOW
    n_chunks = D_MODEL // MXU_WIDTH
    blocks_per_chunk = n_blk // n_chunks

    def norm_next():
        return _rms(xn_ref[...], g_ref[...]).astype(BF16)

    def project_chunk(h, slot, c):
        c0 = c * MXU_WIDTH
        part = jnp.dot(h, wqkv_ref[:, c0:c0 + MXU_WIDTH], preferred_element_type=F32)
        qbuf[slot, :, c0:c0 + MXU_WIDTH] = (part * ATTN_SCALE).astype(BF16)
        if c < n_chunks - 1:
            return
        kv_part = jnp.dot(h, wqkv_ref[:, D_MODEL:], preferred_element_type=F32)
        k = kv_part[:, :KV_LANES]
        v = kv_part[:, KV_LANES:]
        klast_ref[...] = k[rows - WINDOW:]
        vlast_ref[...] = v[rows - WINDOW:]
        lane = lax.broadcasted_iota(jnp.int32, (rows, KV_LANES), 1)
        for kv, half in enumerate((lane < HEAD_DIM, lane >= HEAD_DIM)):
            k_half = jnp.where(half, k, 0.0)
            v_half = jnp.where(half, v, 0.0)
            kbuf[slot, kv] = k_half.astype(BF16)
            vbuf[slot, kv] = v_half.astype(BF16)
            kcar[slot, kv] = k_half[rows - WINDOW:]
            vcar[slot, kv] = v_half[rows - WINDOW:]

    def first(slot):
        o_ref[...] = jnp.zeros(o_ref.shape, F32)
        kcar[...] = jnp.zeros(kcar.shape, F32)
        vcar[...] = jnp.zeros(vcar.shape, F32)
        h = norm_next()
        for c in range(n_chunks):
            project_chunk(h, slot, c)

    def body(cur, nxt):
        k_prev = [jnp.where(step > 1, kcar[nxt, kv], 0.0).astype(BF16) for kv in range(N_KV_HEADS)]
        v_prev = [jnp.where(step > 1, vcar[nxt, kv], 0.0).astype(BF16) for kv in range(N_KV_HEADS)]
        h = norm_next()

        qi = lax.broadcasted_iota(jnp.int32, (WINDOW, KEY_TILE), 0)
        kj = lax.broadcasted_iota(jnp.int32, (WINDOW, KEY_TILE), 1)
        band = (kj >= qi) & (kj <= qi + WINDOW)
        lane_q = lax.broadcasted_iota(jnp.int32, (WINDOW, KV_LANES), 1)
        lane_k = lax.broadcasted_iota(jnp.int32, (KEY_TILE, KV_LANES), 1)
        ones_half = [(lane_k < HEAD_DIM).astype(F32).astype(BF16), (lane_k >= HEAD_DIM).astype(F32).astype(BF16)]

        def block_scores(blk):
            r0 = blk * WINDOW
            qs = jnp.concatenate(
                [qbuf[cur, r0:r0 + WINDOW, gi * KV_LANES:(gi + 1) * KV_LANES] for gi in range(GROUP)], axis=0)
            if blk == 0:
                keys = [jnp.concatenate([k_prev[kv], kbuf[cur, kv, 0:WINDOW, :]], axis=0) for kv in range(N_KV_HEADS)]
            else:
                keys = [kbuf[cur, kv, r0 - WINDOW:r0 + WINDOW, :] for kv in range(N_KV_HEADS)]
            return [lax.dot_general(qs, keys[kv], (((1,), (1,)), ((), ())), preferred_element_type=F32)
                    for kv in range(N_KV_HEADS)]

        scores = block_scores(0)
        for blk in range(n_blk):
            r0 = blk * WINDOW
            if blk == 0:
                mask = band & (kj >= jnp.where(step > 1, 0, WINDOW))
                vals = [jnp.concatenate([v_prev[kv], vbuf[cur, kv, 0:WINDOW, :]], axis=0) for kv in range(N_KV_HEADS)]
            else:
                mask = band
                vals = [vbuf[cur, kv, r0 - WINDOW:r0 + WINDOW, :] for kv in range(N_KV_HEADS)]
            next_scores = block_scores(blk + 1) if blk + 1 < n_blk else None
            if blk % blocks_per_chunk == 0:
                project_chunk(h, nxt, blk // blocks_per_chunk)
            probs, maxes = [], []
            for kv in range(N_KV_HEADS):
                s_all = scores[kv]
                p_kv, m_kv = [], []
                for gi in range(GROUP):
                    head = kv * GROUP + gi
                    s = s_all[gi * WINDOW:(gi + 1) * WINDOW]
                    s = jnp.where(mask, s + bias_ref[head], NEG_INF)
                    m = jnp.maximum(jnp.max(s, axis=-1, keepdims=True), sinks_ref[head])
                    p_kv.append(jnp.exp(s - m).astype(BF16))
                    m_kv.append(m)
                probs.append(jnp.concatenate(p_kv, axis=0))
                maxes.append(m_kv)
            scores = next_scores
            acc = None
            for kv in range(N_KV_HEADS):
                rhs = jnp.concatenate([vals[kv], ones_half[kv]], axis=1)
                part = jnp.dot(probs[kv], rhs, preferred_element_type=F32)
                acc = part if acc is None else acc + part
            outs = []
            for gi in range(GROUP):
                a = acc[gi * WINDOW:(gi + 1) * WINDOW]
                sink_term = jnp.where(lane_q < HEAD_DIM,
                                      jnp.exp(sinks_ref[gi] - maxes[0][gi]),
                                      jnp.exp(sinks_ref[GROUP + gi] - maxes[1][gi]))
                outs.append(a[:, :KV_LANES] / (a[:, KV_LANES:] + sink_term))
            obuf[r0:r0 + WINDOW, :] = jnp.concatenate(outs, axis=1).astype(BF16)

        o_ref[...] = xc_ref[...] + jnp.dot(obuf[...], wo_ref[...], preferred_element_type=F32)

    _two_stage(first, body)


def _attn_prompt(x, g, wqkv, wo, bias, sinks, n):
    tiles, nxt, cur = _pipeline_specs(n, ATTN_TILE)
    assert x.shape[0] == n + ATTN_TILE
    out = pl.BlockSpec((ATTN_TILE, D_MODEL), lambda i: (jnp.where(i == 0, tiles, i - 1), 0))
    last = pl.BlockSpec((WINDOW, KV_LANES), lambda i: (0, 0))
    return pl.pallas_call(
        _attn_prompt_kernel,
        grid=(tiles + 1,),
        in_specs=[nxt, cur, _resident((1, D_MODEL)), _resident((D_MODEL, QKV_COLS)),
                  _resident((D_MODEL, D_MODEL)), _resident((N_HEADS, WINDOW, KEY_TILE)), _smem()],
        out_specs=[out, last, last],
        out_shape=[jax.ShapeDtypeStruct(x.shape, F32),
                   jax.ShapeDtypeStruct((WINDOW, KV_LANES), F32),
                   jax.ShapeDtypeStruct((WINDOW, KV_LANES), F32)],
        scratch_shapes=[pltpu.VMEM((2, ATTN_TILE, D_MODEL), BF16),
                        pltpu.VMEM((2, N_KV_HEADS, ATTN_TILE, KV_LANES), BF16),
                        pltpu.VMEM((2, N_KV_HEADS, ATTN_TILE, KV_LANES), BF16),
                        pltpu.VMEM((2, N_KV_HEADS, WINDOW, KV_LANES), F32),
                        pltpu.VMEM((2, N_KV_HEADS, WINDOW, KV_LANES), F32),
                        pltpu.VMEM((ATTN_TILE, D_MODEL), BF16)],
        compiler_params=_params(),
        name="attn_prompt",
    )(x, x, g, wqkv, wo, bias, sinks)


def _oproj_kernel(x_ref, a_ref, w_ref, all_ref, o_ref):
    del all_ref
    o_ref[...] = x_ref[...] + jnp.dot(a_ref[...].astype(BF16), w_ref[...], preferred_element_type=F32)


def _oproj(x, a, w, combined):
    n = a.shape[0]
    last_tile = x.shape[0] // n - 1
    sample = pl.BlockSpec((n, D_MODEL), lambda i: (last_tile, 0))
    return pl.pallas_call(
        _oproj_kernel,
        grid=(1,),
        in_specs=[sample, _resident((n, D_MODEL)), _resident((D_MODEL, D_MODEL)),
                  pl.BlockSpec(memory_space=pl.ANY)],
        out_specs=sample,
        out_shape=jax.ShapeDtypeStruct(combined.shape, F32),
        input_output_aliases={3: 0},
        compiler_params=_params(),
        name="oproj_sample",
    )(x, a, w, combined)


def _attn_sample_kernel(q_ref, kt_ref, vt_ref, kvn_ref, bias_ref, sink_ref, o_ref, kout_ref, vout_ref, *, steps):
    q_rows = GROUP * steps
    qi = lax.broadcasted_iota(jnp.int32, (q_rows, KEY_TILE), 0) // GROUP
    kj = lax.broadcasted_iota(jnp.int32, (q_rows, KEY_TILE), 1)
    new_at = KEY_TILE - steps
    mask = ((kj < WINDOW) & (kj >= qi)) | ((kj >= new_at) & (kj - new_at <= qi))
    row = lax.broadcasted_iota(jnp.int32, (KV_LANES, KEY_TILE), 0)
    kv_row = (row < HEAD_DIM, row >= HEAD_DIM)
    lane_q = lax.broadcasted_iota(jnp.int32, (q_rows, KV_LANES), 1)
    lane_c = lax.broadcasted_iota(jnp.int32, (KV_LANES, WINDOW), 1)
    new_k = kvn_ref[0:KV_LANES, :]
    new_v = kvn_ref[KV_LANES:, :]

    def new_cols(new_t, b):
        return pltpu.roll(new_t, (WINDOW - steps - b * steps) % WINDOW, 1)

    def shifted_cache(cache_t, new_last):
        return jnp.where(lane_c < WINDOW - steps, pltpu.roll(cache_t, WINDOW - steps, 1), new_last)

    batches = range(q_ref.shape[0])
    scores = []
    for b in batches:
        kt = kt_ref[b]
        new_last = new_cols(new_k, b)
        kout_ref[b] = shifted_cache(kt, new_last)
        keys = jnp.concatenate([kt, new_last], axis=1)
        q = (q_ref[b] * ATTN_SCALE).astype(BF16)
        scores.append([jnp.dot(q, jnp.where(kv_row[kv], keys, 0.0).astype(BF16), preferred_element_type=F32)
                       for kv in range(N_KV_HEADS)])
    probs, sink_terms = [], []
    for b in batches:
        p_b, t_b = [], []
        for kv in range(N_KV_HEADS):
            s = jnp.where(mask, scores[b][kv] + bias_ref[kv], NEG_INF)
            sink = sink_ref[kv][:, 0:1]
            m = jnp.maximum(jnp.max(s, axis=-1, keepdims=True), sink)
            p_b.append(jnp.exp(s - m).astype(BF16))
            t_b.append(jnp.exp(sink - m))
        probs.append(p_b)
        sink_terms.append(t_b)
    for b in batches:
        vt = vt_ref[b]
        new_last = new_cols(new_v, b)
        vout_ref[b] = shifted_cache(vt, new_last)
        vals = jnp.concatenate([vt, new_last], axis=1)
        acc = None
        for kv in range(N_KV_HEADS):
            rhs = jnp.concatenate([jnp.where(kv_row[kv], vals, 0.0), jnp.where(kv_row[kv], 1.0, 0.0)], axis=0)
            part = lax.dot_general(probs[b][kv], rhs.astype(BF16), (((1,), (1,)), ((), ())),
                                   preferred_element_type=F32)
            acc = part if acc is None else acc + part
        denom = acc[:, KV_LANES:] + jnp.where(lane_q < HEAD_DIM, sink_terms[b][0], sink_terms[b][1])
        o_ref[b] = acc[:, :KV_LANES] / denom


def _attn_sample(q, kt, vt, kvn, bias, sink, steps):
    nb = q.shape[0]
    q_rows = GROUP * steps
    bt = SAMPLE_BATCH_TILE
    assert bt * steps == WINDOW and nb % bt == 0

    def batch_spec(r):
        return pl.BlockSpec((bt, r, KV_LANES), lambda i: (i, 0, 0))

    return pl.pallas_call(
        functools.partial(_attn_sample_kernel, steps=steps),
        grid=(nb // bt,),
        in_specs=[batch_spec(q_rows), batch_spec(KV_LANES), batch_spec(KV_LANES),
                  pl.BlockSpec((2 * KV_LANES, bt * steps), lambda i: (0, i)),
                  _resident((N_KV_HEADS, q_rows, KEY_TILE)), _resident((N_KV_HEADS, q_rows, KV_LANES))],
        out_specs=[batch_spec(q_rows), batch_spec(KV_LANES), batch_spec(KV_LANES)],
        out_shape=[jax.ShapeDtypeStruct((nb, q_rows, KV_LANES), F32),
                   jax.ShapeDtypeStruct((nb, KV_LANES, WINDOW), F32),
                   jax.ShapeDtypeStruct((nb, KV_LANES, WINDOW), F32)],
        compiler_params=_params(),
        name="attn_sample",
    )(q, kt, vt, kvn, bias, sink)


def kernel(x_prompt, x_sample, state_conv, cache_k, cache_v, g_mix, g_ffn, g_final, w_conv_in, conv_w,
           w_conv_out, w_q, w_k, w_v, w_o, sinks, rel_table, w_gate, w_up, w_down):
    batch, seq, _ = x_prompt.shape
    dec_batch, dec_seq, _ = x_sample.shape
    assert batch == 1 and seq % TOKEN_TILE == 0 and seq % ATTN_TILE == 0
    assert ATTN_TILE % (WINDOW * (D_MODEL // MXU_WIDTH)) == 0
    assert dec_batch % SAMPLE_BATCH_TILE == 0 and dec_seq <= SUBLANES
    assert dec_batch * dec_seq == TOKEN_TILE, "the sample rows ride the token-wise kernels as one tile"
    assert g_mix.shape[0] == 2, "layer 0 is the conv mixer, layer 1 the attention mixer"

    win = w_conv_in[0]
    wout = w_conv_out[0]
    wq = w_q[0].reshape(D_MODEL, N_KV_HEADS, GROUP, HEAD_DIM).transpose(0, 2, 1, 3).reshape(D_MODEL, D_MODEL)
    wqkv = jnp.concatenate([wq, w_k[0], w_v[0]], axis=1).astype(BF16)
    wo = w_o[0].reshape(N_KV_HEADS, GROUP, HEAD_DIM, D_MODEL).transpose(1, 0, 2, 3).reshape(D_MODEL, D_MODEL)
    wo = wo.astype(BF16)
    wkvt = jnp.concatenate([w_k[0], w_v[0]], axis=1).T.astype(BF16)
    gm = g_mix.reshape(2, 1, D_MODEL)
    gf = g_ffn.reshape(2, 1, D_MODEL)
    gfin = g_final.reshape(1, D_MODEL)
    cw = conv_w[0]
    sink_vec = sinks[0]

    bucket_p = jnp.asarray(np.tile(_t5_bucket_np(WINDOW - np.arange(KEY_TILE)[None, :]), (SUBLANES, 1)))
    key_s = np.arange(KEY_TILE)[None, :]
    key_pos = np.where(key_s < WINDOW, key_s, key_s - (KEY_TILE - dec_seq) + WINDOW)
    dist_s = np.arange(dec_seq)[:, None] + WINDOW - key_pos
    bucket_s = jnp.asarray(np.repeat(_t5_bucket_np(dist_s), GROUP, axis=0))

    n_s = dec_batch * dec_seq
    prev = jnp.pad(state_conv[0], ((0, 0), (0, dec_seq - CONV_STATE), (0, 0))).reshape(n_s, D_MODEL)
    x, tail_p, u_s, (wg, wu, wd), (bias_p, bias_s, sink_s) = _conv(
        x_prompt.reshape(seq, D_MODEL), x_sample.reshape(n_s, D_MODEL), prev, gm[0], win, cw, wout,
        w_gate, w_up, w_down, dec_seq, bucket_p, bucket_s, rel_table, sink_vec)
    state_conv_prompt = tail_p[SUBLANES - CONV_STATE:].reshape(1, batch, CONV_STATE, D_MODEL)
    state_conv_sample = u_s.reshape(dec_batch, dec_seq, D_MODEL)[:, dec_seq - CONV_STATE:][None]
    x, q_s, kv_new_t = _ffn(x, gf, wg, wu, wd, gfin, 0, gm[1], wqkv, wkvt)

    x_attn, k_last, v_last = _attn_prompt(x, gm[1], wqkv, wo, bias_p, sink_vec, seq)
    cache_k_prompt = k_last.reshape(1, batch, WINDOW, N_KV_HEADS, HEAD_DIM)
    cache_v_prompt = v_last.reshape(1, batch, WINDOW, N_KV_HEADS, HEAD_DIM)
    q_s = q_s.reshape(dec_batch, dec_seq * GROUP, KV_LANES)
    kt = cache_k[0].transpose(0, 2, 3, 1).reshape(dec_batch, KV_LANES, WINDOW)
    vt = cache_v[0].transpose(0, 2, 3, 1).reshape(dec_batch, KV_LANES, WINDOW)
    o_s, kt_out, vt_out = _attn_sample(q_s, kt, vt, kv_new_t, bias_s, sink_s, dec_seq)
    o_s = o_s.reshape(n_s, D_MODEL)
    x = _oproj(x, o_s, wo, x_attn)
    y_prompt, y_sample = _ffn_final(x, gf, wg, wu, wd, gfin, 1)
    y_prompt = y_prompt.reshape(batch, seq, D_MODEL)
    y_sample = y_sample.reshape(dec_batch, dec_seq, D_MODEL)
    cache_k_sample = kt_out.reshape(dec_batch, N_KV_HEADS, HEAD_DIM, WINDOW).transpose(0, 3, 1, 2)[None]
    cache_v_sample = vt_out.reshape(dec_batch, N_KV_HEADS, HEAD_DIM, WINDOW).transpose(0, 3, 1, 2)[None]

    return (y_prompt, y_sample, state_conv_prompt, state_conv_sample,
            cache_k_prompt, cache_k_sample, cache_v_prompt, cache_v_sample)
```

```python
import functools
import math

import numpy as np
import jax
import jax.numpy as jnp
from jax import lax
from jax.experimental import pallas as pl
from jax.experimental.pallas import tpu as pltpu

D_MODEL = 1024
D_FF = 2816
HEAD_DIM = 64
N_HEADS = 16
N_KV_HEADS = 2
GROUP = N_HEADS // N_KV_HEADS
WINDOW = 128
N_BUCKETS = 32
MAX_DISTANCE = 128
CONV_STATE = 2
EPS = 1e-5
NEG_INF = -1e30
ATTN_SCALE = 1.0 / math.sqrt(HEAD_DIM)

KV_LANES = N_KV_HEADS * HEAD_DIM
QKV_COLS = D_MODEL + 2 * KV_LANES
SUBLANES = 8
BF16_SUBLANES = 16
KEY_TILE = 2 * WINDOW

TOKEN_TILE = 512
ATTN_TILE = 512
MXU_WIDTH = 256
FFN_SPLIT = 2
CONV_SPLIT = 2
SAMPLE_BATCH_TILE = 32
VMEM_LIMIT_BYTES = 56 * 1024 * 1024

F32 = jnp.float32
BF16 = jnp.bfloat16


def _params():
    return pltpu.CompilerParams(dimension_semantics=("arbitrary",), vmem_limit_bytes=VMEM_LIMIT_BYTES)


def _resident(shape):
    zeros = (0,) * len(shape)
    return pl.BlockSpec(shape, lambda *_: zeros, pipeline_mode=pl.Buffered(1))


def _layer(shape, layer):
    index = (layer,) + (0,) * len(shape)
    return pl.BlockSpec((None,) + tuple(shape), lambda *_: index, pipeline_mode=pl.Buffered(1))


def _smem():
    return pl.BlockSpec(memory_space=pltpu.SMEM)


def _rms(x, g):
    return x * lax.rsqrt(jnp.mean(x * x, axis=-1, keepdims=True) + EPS) * g


def _pipeline_specs(n, rows):
    tiles = n // rows
    nxt = pl.BlockSpec((rows, D_MODEL), lambda i: (jnp.minimum(i, tiles - 1), 0))
    cur = pl.BlockSpec((rows, D_MODEL), lambda i: (jnp.maximum(i - 1, 0), 0))
    return tiles, nxt, cur


def _two_stage(first, body):
    step = pl.program_id(0)

    @pl.when(step == 0)
    def _():
        first(0)

    @pl.when(step % 2 == 1)
    def _():
        body(0, 1)

    @pl.when((step % 2 == 0) & (step > 0))
    def _():
        body(1, 0)


def _t5_bucket_np(dist):
    n = np.maximum(dist, 0)
    max_exact = N_BUCKETS // 2
    nf = np.maximum(n, 1).astype(np.float32)
    large = max_exact + (np.log(nf / np.float32(max_exact)) / np.float32(math.log(MAX_DISTANCE / max_exact))
                         * np.float32(N_BUCKETS - max_exact)).astype(np.int32)
    large = np.minimum(large, N_BUCKETS - 1)
    return np.where(n < max_exact, n, large).astype(np.int32)


def _ffn_kernel(x_ref, g_ref, wg_ref, wu_ref, wd_ref, gf_ref, *o_refs, final_norm):
    rows = x_ref.shape[0] // FFN_SPLIT
    halves = [pl.ds(i * rows, rows) for i in range(FFN_SPLIT)]
    acts = []
    for r in halves:
        h = _rms(x_ref[r, :], g_ref[...]).astype(BF16)
        gate = jnp.dot(h, wg_ref[...], preferred_element_type=F32)
        up = jnp.dot(h, wu_ref[...], preferred_element_type=F32)
        acts.append((gate * jax.nn.sigmoid(gate) * up).astype(BF16))
    for r, act in zip(halves, acts):
        y = x_ref[r, :] + jnp.dot(act, wd_ref[...], preferred_element_type=F32)
        if final_norm:
            y = _rms(y, gf_ref[...])
        for o_ref in o_refs:
            o_ref[r, :] = y


def _ffn_weight_specs(layer):
    return [_layer((1, D_MODEL), layer), _layer((D_MODEL, D_FF), layer), _layer((D_MODEL, D_FF), layer),
            _layer((D_FF, D_MODEL), layer), _resident((1, D_MODEL))]


def _ffn_proj_kernel(x_ref, g_ref, wg_ref, wu_ref, wd_ref, gf_ref, gq_ref, wq_ref, wkvt_ref,
                     o_ref, q_ref, kvt_ref):
    _ffn_kernel(x_ref, g_ref, wg_ref, wu_ref, wd_ref, gf_ref, o_ref, final_norm=False)

    @pl.when(pl.program_id(0) == pl.num_programs(0) - 1)
    def _():
        h = _rms(o_ref[...], gq_ref[...]).astype(BF16)
        q_ref[...] = jnp.dot(h, wq_ref[...], preferred_element_type=F32)
        kvt_ref[...] = lax.dot_general(wkvt_ref[...], h, (((1,), (1,)), ((), ())), preferred_element_type=F32)


def _ffn(x, g, wg, wu, wd, g_final, layer, g_q, wqkv, wkvt):
    n = x.shape[0]
    tile = pl.BlockSpec((TOKEN_TILE, D_MODEL), lambda i: (i, 0))

    def once(shape):
        return pl.BlockSpec(shape, lambda i: (0, 0))

    return pl.pallas_call(
        _ffn_proj_kernel,
        grid=(n // TOKEN_TILE,),
        in_specs=[tile] + _ffn_weight_specs(layer) + [_resident((1, D_MODEL)), _resident((D_MODEL, D_MODEL)),
                                                      _resident((2 * KV_LANES, D_MODEL))],
        out_specs=[tile, once((TOKEN_TILE, D_MODEL)), once((2 * KV_LANES, TOKEN_TILE))],
        out_shape=[jax.ShapeDtypeStruct((n, D_MODEL), F32),
                   jax.ShapeDtypeStruct((TOKEN_TILE, D_MODEL), F32),
                   jax.ShapeDtypeStruct((2 * KV_LANES, TOKEN_TILE), F32)],
        compiler_params=_params(),
        name="ffn",
    )(x, g, wg, wu, wd, g_final, g_q, wqkv, wkvt)


def _ffn_final(x, g, wg, wu, wd, g_final, layer):
    n = x.shape[0]
    tiles = n // TOKEN_TILE - 1
    shape = (TOKEN_TILE, D_MODEL)
    y_prompt, y_sample = pl.pallas_call(
        functools.partial(_ffn_kernel, final_norm=True),
        grid=(tiles + 1,),
        in_specs=[pl.BlockSpec(shape, lambda i: (jnp.where(i == 0, tiles, i - 1), 0))] + _ffn_weight_specs(layer),
        out_specs=[pl.BlockSpec(shape, lambda i: (jnp.maximum(i - 1, 0), 0)),
                   pl.BlockSpec(shape, lambda i: (jnp.minimum(i, 1), 0))],
        out_shape=[jax.ShapeDtypeStruct((tiles * TOKEN_TILE, D_MODEL), F32),
                   jax.ShapeDtypeStruct((2 * TOKEN_TILE, D_MODEL), F32)],
        compiler_params=_params(),
        name="ffn_final",
    )(x, g, wg, wu, wd, g_final)
    return y_prompt, y_sample[:TOKEN_TILE]


def _conv_gates(h, win_ref):
    cx = jnp.dot(h, win_ref[:, D_MODEL:], preferred_element_type=F32)
    u = cx[:, :D_MODEL] * cx[:, D_MODEL:]
    b = jnp.dot(h, win_ref[:, :D_MODEL], preferred_element_type=F32)
    return b, u


def _conv_kernel(x_ref, xs_ref, prev_ref, g_ref, win_ref, cw_ref, wout_ref, wg_ref, wu_ref, wd_ref,
                 bucket_p_ref, bucket_s_ref, table_ref, sinks_ref,
                 o_ref, tail_out_ref, us_ref, wg_out_ref, wu_out_ref, wd_out_ref,
                 bias_p_ref, bias_s_ref, sink_s_ref, tail_ref, *, steps):
    step = pl.program_id(0)
    last = pl.num_programs(0) - 1
    cw = cw_ref[...]

    @pl.when(step < last)
    def _():
        wg_out_ref[...] = wg_ref[...].astype(BF16)
        wu_out_ref[...] = wu_ref[...].astype(BF16)
        wd_out_ref[...] = wd_ref[...].astype(BF16)
        tail = jnp.where(step > 0, tail_ref[...], 0.0)
        rows = x_ref.shape[0] // CONV_SPLIT
        groups = [pl.ds(i * rows, rows) for i in range(CONV_SPLIT)]
        gates = [_conv_gates(_rms(x_ref[r, :], g_ref[...]), win_ref) for r in groups]
        r8 = lax.broadcasted_iota(jnp.int32, (SUBLANES, D_MODEL), 0)
        for r, (b, u) in zip(groups, gates):
            u1 = pltpu.roll(u, 1, 0)
            u2 = pltpu.roll(u, 2, 0)
            u1 = jnp.concatenate([jnp.where(r8 < 1, pltpu.roll(tail, 1, 0), u1[:SUBLANES]), u1[SUBLANES:]], axis=0)
            u2 = jnp.concatenate([jnp.where(r8 < 2, pltpu.roll(tail, 2, 0), u2[:SUBLANES]), u2[SUBLANES:]], axis=0)
            v = cw[0:1] * u2 + cw[1:2] * u1 + cw[2:3] * u
            o_ref[r, :] = x_ref[r, :] + jnp.dot(b * v, wout_ref[...], preferred_element_type=F32)
            tail = u[rows - SUBLANES:]
        tail_ref[...] = tail
        tail_out_ref[...] = tail

    @pl.when(step == last)
    def _():
        x = xs_ref[...]
        b, u = _conv_gates(_rms(x, g_ref[...]), win_ref)
        rows = u.shape[0]
        prev = prev_ref[...]
        t = lax.broadcasted_iota(jnp.int32, u.shape, 0) % steps
        u1 = jnp.where(t >= 1, pltpu.roll(u, 1, 0), pltpu.roll(prev, rows - 1, 0))
        u2 = jnp.where(t >= 2, pltpu.roll(u, 2, 0), prev)
        v = cw[0:1] * u2 + cw[1:2] * u1 + cw[2:3] * u
        o_ref[...] = x + jnp.dot(b * v, wout_ref[...], preferred_element_type=F32)
        us_ref[...] = u
        _bias_tables(bucket_p_ref, bucket_s_ref, table_ref, sinks_ref, bias_p_ref, bias_s_ref, sink_s_ref)


def _conv(x, xs, prev, g, win, cw, wout, w_gate, w_up, w_down, steps, bucket_p, bucket_s, table, sinks):
    n = x.shape[0]
    tiles = n // TOKEN_TILE
    assert xs.shape == (TOKEN_TILE, D_MODEL)
    tile_in = pl.BlockSpec((TOKEN_TILE, D_MODEL), lambda i: (jnp.minimum(i, tiles - 1), 0))
    tile_out = pl.BlockSpec((TOKEN_TILE, D_MODEL), lambda i: (i, 0))
    sample = _resident((TOKEN_TILE, D_MODEL))

    def slabs(w):
        flat = w.reshape(-1, w.shape[-1])
        rows = flat.shape[0] // tiles
        assert rows * tiles == flat.shape[0] and rows % BF16_SUBLANES == 0
        return flat, pl.BlockSpec((rows, flat.shape[1]), lambda i: (jnp.minimum(i, tiles - 1), 0))

    (wg, wg_spec), (wu, wu_spec), (wd, wd_spec) = slabs(w_gate), slabs(w_up), slabs(w_down)
    q_rows = bucket_s.shape[0]
    table_shapes = [(N_HEADS, WINDOW, KEY_TILE), (N_KV_HEADS, q_rows, KEY_TILE), (N_KV_HEADS, q_rows, KV_LANES)]
    y, tail, u_s, wg_b, wu_b, wd_b, bias_p, bias_s, sink_s = pl.pallas_call(
        functools.partial(_conv_kernel, steps=steps),
        grid=(tiles + 1,),
        in_specs=[tile_in, sample, sample, _resident((1, D_MODEL)), _resident((D_MODEL, 3 * D_MODEL)),
                  _resident((3, D_MODEL)), _resident((D_MODEL, D_MODEL)), wg_spec, wu_spec, wd_spec,
                  _resident((SUBLANES, KEY_TILE)), _resident((q_rows, KEY_TILE)), _smem(), _smem()],
        out_specs=[tile_out, pl.BlockSpec((SUBLANES, D_MODEL), lambda i: (0, 0)),
                   pl.BlockSpec((TOKEN_TILE, D_MODEL), lambda i: (0, 0)), wg_spec, wu_spec, wd_spec]
        + [pl.BlockSpec(shape, lambda i: (0, 0, 0)) for shape in table_shapes],
        out_shape=[jax.ShapeDtypeStruct((n + TOKEN_TILE, D_MODEL), F32),
                   jax.ShapeDtypeStruct((SUBLANES, D_MODEL), F32),
                   jax.ShapeDtypeStruct((TOKEN_TILE, D_MODEL), F32),
                   jax.ShapeDtypeStruct(wg.shape, BF16),
                   jax.ShapeDtypeStruct(wu.shape, BF16),
                   jax.ShapeDtypeStruct(wd.shape, BF16)]
        + [jax.ShapeDtypeStruct(shape, F32) for shape in table_shapes],
        scratch_shapes=[pltpu.VMEM((SUBLANES, D_MODEL), F32)],
        compiler_params=_params(),
        name="conv",
    )(x, xs, prev, g, win, cw, wout, wg, wu, wd, bucket_p, bucket_s, table, sinks)
    weights = (wg_b.reshape(w_gate.shape), wu_b.reshape(w_up.shape), wd_b.reshape(w_down.shape))
    return y, tail, u_s, weights, (bias_p, bias_s, sink_s)


def _lookup_bias(bucket, table_ref, head):
    acc = jnp.zeros(bucket.shape, F32)
    for b in range(N_BUCKETS):
        acc = jnp.where(bucket == b, table_ref[b, head], acc)
    return acc


def _bias_tables(bucket_p_ref, bucket_s_ref, table_ref, sinks_ref, bias_p_ref, bias_s_ref, sink_s_ref):
    bucket_row0 = bucket_p_ref[...]
    for head in range(N_HEADS):
        row0 = _lookup_bias(bucket_row0, table_ref, head)
        rows = jnp.concatenate([row0] * (WINDOW // SUBLANES), axis=0)
        bias_p_ref[head] = pltpu.roll(rows, 0, 1, stride=1, stride_axis=0)
    bucket_s = bucket_s_ref[...]
    grp = lax.broadcasted_iota(jnp.int32, bucket_s.shape, 0) % GROUP
    grp_l = lax.broadcasted_iota(jnp.int32, (bucket_s.shape[0], KV_LANES), 0) % GROUP
    for kv in range(N_KV_HEADS):
        bias = jnp.zeros(bucket_s.shape, F32)
        sink = jnp.zeros((bucket_s.shape[0], KV_LANES), F32)
        for gi in range(GROUP):
            head = kv * GROUP + gi
            bias = jnp.where(grp == gi, _lookup_bias(bucket_s, table_ref, head), bias)
            sink = jnp.where(grp_l == gi, sinks_ref[head], sink)
        bias_s_ref[kv] = bias
        sink_s_ref[kv] = sink


def _attn_prompt_kernel(xn_ref, xc_ref, xs_ref, os_ref, g_ref, wqkv_ref, wo_ref, bias_ref, sinks_ref,
                        o_ref, klast_ref, vlast_ref, qbuf, kbuf, vbuf, kcar, vcar, obuf):
    step = pl.program_id(0)
    rows = xn_ref.shape[0]
    n_blk = rows // WINDOW
    n_chunks = D_MODEL // MXU_WIDTH
    blocks_per_chunk = n_blk // n_chunks

    def norm_next():
        return _rms(xn_ref[...], g_ref[...]).astype(BF16)

    def project_chunk(h, slot, c):
        c0 = c * MXU_WIDTH
        part = jnp.dot(h, wqkv_ref[:, c0:c0 + MXU_WIDTH], preferred_element_type=F32)
        qbuf[slot, :, c0:c0 + MXU_WIDTH] = (part * ATTN_SCALE).astype(BF16)
        if c < n_chunks - 1:
            return
        kv_part = jnp.dot(h, wqkv_ref[:, D_MODEL:], preferred_element_type=F32)
        k = kv_part[:, :KV_LANES]
        v = kv_part[:, KV_LANES:]
        klast_ref[...] = k[rows - WINDOW:]
        vlast_ref[...] = v[rows - WINDOW:]
        lane = lax.broadcasted_iota(jnp.int32, (rows, KV_LANES), 1)
        for kv, half in enumerate((lane < HEAD_DIM, lane >= HEAD_DIM)):
            k_half = jnp.where(half, k, 0.0)
            v_half = jnp.where(half, v, 0.0)
            kbuf[slot, kv] = k_half.astype(BF16)
            vbuf[slot, kv] = v_half.astype(BF16)
            kcar[slot, kv] = k_half[rows - WINDOW:]
            vcar[slot, kv] = v_half[rows - WINDOW:]

    def first(slot):
        o_ref[...] = xs_ref[...] + jnp.dot(os_ref[...].astype(BF16), wo_ref[...], preferred_element_type=F32)
        kcar[...] = jnp.zeros(kcar.shape, F32)
        vcar[...] = jnp.zeros(vcar.shape, F32)
        h = norm_next()
        for c in range(n_chunks):
            project_chunk(h, slot, c)

    def body(cur, nxt):
        k_prev = [jnp.where(step > 1, kcar[nxt, kv], 0.0).astype(BF16) for kv in range(N_KV_HEADS)]
        v_prev = [jnp.where(step > 1, vcar[nxt, kv], 0.0).astype(BF16) for kv in range(N_KV_HEADS)]
        h = norm_next()

        qi = lax.broadcasted_iota(jnp.int32, (WINDOW, KEY_TILE), 0)
        kj = lax.broadcasted_iota(jnp.int32, (WINDOW, KEY_TILE), 1)
        band = (kj >= qi) & (kj <= qi + WINDOW)
        lane_q = lax.broadcasted_iota(jnp.int32, (WINDOW, KV_LANES), 1)
        lane_k = lax.broadcasted_iota(jnp.int32, (KEY_TILE, KV_LANES), 1)
        ones_half = [(lane_k < HEAD_DIM).astype(F32).astype(BF16), (lane_k >= HEAD_DIM).astype(F32).astype(BF16)]

        def block_scores(blk):
            r0 = blk * WINDOW
            qs = jnp.concatenate(
                [qbuf[cur, r0:r0 + WINDOW, gi * KV_LANES:(gi + 1) * KV_LANES] for gi in range(GROUP)], axis=0)
            if blk == 0:
                keys = [jnp.concatenate([k_prev[kv], kbuf[cur, kv, 0:WINDOW, :]], axis=0) for kv in range(N_KV_HEADS)]
            else:
                keys = [kbuf[cur, kv, r0 - WINDOW:r0 + WINDOW, :] for kv in range(N_KV_HEADS)]
            return [lax.dot_general(qs, keys[kv], (((1,), (1,)), ((), ())), preferred_element_type=F32)
                    for kv in range(N_KV_HEADS)]

        scores = block_scores(0)
        for blk in range(n_blk):
            r0 = blk * WINDOW
            if blk == 0:
                mask = band & (kj >= jnp.where(step > 1, 0, WINDOW))
                vals = [jnp.concatenate([v_prev[kv], vbuf[cur, kv, 0:WINDOW, :]], axis=0) for kv in range(N_KV_HEADS)]
            else:
                mask = band
                vals = [vbuf[cur, kv, r0 - WINDOW:r0 + WINDOW, :] for kv in range(N_KV_HEADS)]
            next_scores = block_scores(blk + 1) if blk + 1 < n_blk else None
            if blk % blocks_per_chunk == 0:
                project_chunk(h, nxt, blk // blocks_per_chunk)
            probs, maxes = [], []
            for kv in range(N_KV_HEADS):
                s_all = scores[kv]
                p_kv, m_kv = [], []
                for gi in range(GROUP):
                    head = kv * GROUP + gi
                    s = s_all[gi * WINDOW:(gi + 1) * WINDOW]
                    s = jnp.where(mask, s + bias_ref[head], NEG_INF)
                    m = jnp.maximum(jnp.max(s, axis=-1, keepdims=True), sinks_ref[head])
                    p_kv.append(jnp.exp(s - m).astype(BF16))
                    m_kv.append(m)
                probs.append(jnp.concatenate(p_kv, axis=0))
                maxes.append(m_kv)
            scores = next_scores
            acc = None
            for kv in range(N_KV_HEADS):
                rhs = jnp.concatenate([vals[kv], ones_half[kv]], axis=1)
                part = jnp.dot(probs[kv], rhs, preferred_element_type=F32)
                acc = part if acc is None else acc + part
            outs = []
            for gi in range(GROUP):
                a = acc[gi * WINDOW:(gi + 1) * WINDOW]
                sink_term = jnp.where(lane_q < HEAD_DIM,
                                      jnp.exp(sinks_ref[gi] - maxes[0][gi]),
                                      jnp.exp(sinks_ref[GROUP + gi] - maxes[1][gi]))
                outs.append(a[:, :KV_LANES] / (a[:, KV_LANES:] + sink_term))
            obuf[r0:r0 + WINDOW, :] = jnp.concatenate(outs, axis=1).astype(BF16)

        o_ref[...] = xc_ref[...] + jnp.dot(obuf[...], wo_ref[...], preferred_element_type=F32)

    _two_stage(first, body)


def _attn_prompt(x, o_sample, g, wqkv, wo, bias, sinks, n):
    tiles, nxt, cur = _pipeline_specs(n, ATTN_TILE)
    assert x.shape[0] == n + ATTN_TILE and o_sample.shape == (ATTN_TILE, D_MODEL)
    out = pl.BlockSpec((ATTN_TILE, D_MODEL), lambda i: (jnp.where(i == 0, tiles, i - 1), 0))
    sample = pl.BlockSpec((ATTN_TILE, D_MODEL), lambda i: (tiles, 0), pipeline_mode=pl.Buffered(1))
    last = pl.BlockSpec((WINDOW, KV_LANES), lambda i: (0, 0))
    return pl.pallas_call(
        _attn_prompt_kernel,
        grid=(tiles + 1,),
        in_specs=[nxt, cur, sample, _resident((ATTN_TILE, D_MODEL)), _resident((1, D_MODEL)),
                  _resident((D_MODEL, QKV_COLS)),
                  _resident((D_MODEL, D_MODEL)), _resident((N_HEADS, WINDOW, KEY_TILE)), _smem()],
        out_specs=[out, last, last],
        out_shape=[jax.ShapeDtypeStruct(x.shape, F32),
                   jax.ShapeDtypeStruct((WINDOW, KV_LANES), F32),
                   jax.ShapeDtypeStruct((WINDOW, KV_LANES), F32)],
        scratch_shapes=[pltpu.VMEM((2, ATTN_TILE, D_MODEL), BF16),
                        pltpu.VMEM((2, N_KV_HEADS, ATTN_TILE, KV_LANES), BF16),
                        pltpu.VMEM((2, N_KV_HEADS, ATTN_TILE, KV_LANES), BF16),
                        pltpu.VMEM((2, N_KV_HEADS, WINDOW, KV_LANES), F32),
                        pltpu.VMEM((2, N_KV_HEADS, WINDOW, KV_LANES), F32),
                        pltpu.VMEM((ATTN_TILE, D_MODEL), BF16)],
        compiler_params=_params(),
        name="attn_prompt",
    )(x, x, x, o_sample, g, wqkv, wo, bias, sinks)


def _attn_sample_kernel(q_ref, kt_ref, vt_ref, kvn_ref, bias_ref, sink_ref, o_ref, kout_ref, vout_ref, *, steps):
    q_rows = GROUP * steps
    qi = lax.broadcasted_iota(jnp.int32, (q_rows, KEY_TILE), 0) // GROUP
    kj = lax.broadcasted_iota(jnp.int32, (q_rows, KEY_TILE), 1)
    new_at = KEY_TILE - steps
    mask = ((kj < WINDOW) & (kj >= qi)) | ((kj >= new_at) & (kj - new_at <= qi))
    row = lax.broadcasted_iota(jnp.int32, (KV_LANES, KEY_TILE), 0)
    kv_row = (row < HEAD_DIM, row >= HEAD_DIM)
    lane_q = lax.broadcasted_iota(jnp.int32, (q_rows, KV_LANES), 1)
    lane_c = lax.broadcasted_iota(jnp.int32, (KV_LANES, WINDOW), 1)
    new_k = kvn_ref[0:KV_LANES, :]
    new_v = kvn_ref[KV_LANES:, :]

    def new_cols(new_t, b):
        return pltpu.roll(new_t, (WINDOW - steps - b * steps) % WINDOW, 1)

    def shifted_cache(cache_t, new_last):
        return jnp.where(lane_c < WINDOW - steps, pltpu.roll(cache_t, WINDOW - steps, 1), new_last)

    batches = range(q_ref.shape[0])
    scores = []
    for b in batches:
        kt = kt_ref[b]
        new_last = new_cols(new_k, b)
        kout_ref[b] = shifted_cache(kt, new_last)
        keys = jnp.concatenate([kt, new_last], axis=1)
        q = (q_ref[b] * ATTN_SCALE).astype(BF16)
        scores.append([jnp.dot(q, jnp.where(kv_row[kv], keys, 0.0).astype(BF16), preferred_element_type=F32)
                       for kv in range(N_KV_HEADS)])
    probs, sink_terms = [], []
    for b in batches:
        p_b, t_b = [], []
        for kv in range(N_KV_HEADS):
            s = jnp.where(mask, scores[b][kv] + bias_ref[kv], NEG_INF)
            sink = sink_ref[kv][:, 0:1]
            m = jnp.maximum(jnp.max(s, axis=-1, keepdims=True), sink)
            p_b.append(jnp.exp(s - m).astype(BF16))
            t_b.append(jnp.exp(sink - m))
        probs.append(p_b)
        sink_terms.append(t_b)
    for b in batches:
        vt = vt_ref[b]
        new_last = new_cols(new_v, b)
        vout_ref[b] = shifted_cache(vt, new_last)
        vals = jnp.concatenate([vt, new_last], axis=1)
        acc = None
        for kv in range(N_KV_HEADS):
            rhs = jnp.concatenate([jnp.where(kv_row[kv], vals, 0.0), jnp.where(kv_row[kv], 1.0, 0.0)], axis=0)
            part = lax.dot_general(probs[b][kv], rhs.astype(BF16), (((1,), (1,)), ((), ())),
                                   preferred_element_type=F32)
            acc = part if acc is None else acc + part
        denom = acc[:, KV_LANES:] + jnp.where(lane_q < HEAD_DIM, sink_terms[b][0], sink_terms[b][1])
        o_ref[b] = acc[:, :KV_LANES] / denom


def _attn_sample(q, kt, vt, kvn, bias, sink, steps):
    nb = q.shape[0]
    q_rows = GROUP * steps
    bt = SAMPLE_BATCH_TILE
    assert bt * steps == WINDOW and nb % bt == 0

    def batch_spec(r):
        return pl.BlockSpec((bt, r, KV_LANES), lambda i: (i, 0, 0))

    return pl.pallas_call(
        functools.partial(_attn_sample_kernel, steps=steps),
        grid=(nb // bt,),
        in_specs=[batch_spec(q_rows), batch_spec(KV_LANES), batch_spec(KV_LANES),
                  pl.BlockSpec((2 * KV_LANES, bt * steps), lambda i: (0, i)),
                  _resident((N_KV_HEADS, q_rows, KEY_TILE)), _resident((N_KV_HEADS, q_rows, KV_LANES))],
        out_specs=[batch_spec(q_rows), batch_spec(KV_LANES), batch_spec(KV_LANES)],
        out_shape=[jax.ShapeDtypeStruct((nb, q_rows, KV_LANES), F32),
                   jax.ShapeDtypeStruct((nb, KV_LANES, WINDOW), F32),
                   jax.ShapeDtypeStruct((nb, KV_LANES, WINDOW), F32)],
        compiler_params=_params(),
        name="attn_sample",
    )(q, kt, vt, kvn, bias, sink)


def kernel(x_prompt, x_sample, state_conv, cache_k, cache_v, g_mix, g_ffn, g_final, w_conv_in, conv_w,
           w_conv_out, w_q, w_k, w_v, w_o, sinks, rel_table, w_gate, w_up, w_down):
    batch, seq, _ = x_prompt.shape
    dec_batch, dec_seq, _ = x_sample.shape
    assert batch == 1 and seq % TOKEN_TILE == 0 and seq % ATTN_TILE == 0
    assert ATTN_TILE % (WINDOW * (D_MODEL // MXU_WIDTH)) == 0
    assert dec_batch % SAMPLE_BATCH_TILE == 0 and dec_seq <= SUBLANES
    assert dec_batch * dec_seq == TOKEN_TILE, "the sample rows ride the token-wise kernels as one tile"
    assert g_mix.shape[0] == 2, "layer 0 is the conv mixer, layer 1 the attention mixer"

    win = w_conv_in[0]
    wout = w_conv_out[0]
    wq = w_q[0].reshape(D_MODEL, N_KV_HEADS, GROUP, HEAD_DIM).transpose(0, 2, 1, 3).reshape(D_MODEL, D_MODEL)
    wqkv = jnp.concatenate([wq, w_k[0], w_v[0]], axis=1).astype(BF16)
    wo = w_o[0].reshape(N_KV_HEADS, GROUP, HEAD_DIM, D_MODEL).transpose(1, 0, 2, 3).reshape(D_MODEL, D_MODEL)
    wo = wo.astype(BF16)
    wkvt = jnp.concatenate([w_k[0], w_v[0]], axis=1).T.astype(BF16)
    gm = g_mix.reshape(2, 1, D_MODEL)
    gf = g_ffn.reshape(2, 1, D_MODEL)
    gfin = g_final.reshape(1, D_MODEL)
    cw = conv_w[0]
    sink_vec = sinks[0]

    bucket_p = jnp.asarray(np.tile(_t5_bucket_np(WINDOW - np.arange(KEY_TILE)[None, :]), (SUBLANES, 1)))
    key_s = np.arange(KEY_TILE)[None, :]
    key_pos = np.where(key_s < WINDOW, key_s, key_s - (KEY_TILE - dec_seq) + WINDOW)
    dist_s = np.arange(dec_seq)[:, None] + WINDOW - key_pos
    bucket_s = jnp.asarray(np.repeat(_t5_bucket_np(dist_s), GROUP, axis=0))

    n_s = dec_batch * dec_seq
    prev = jnp.pad(state_conv[0], ((0, 0), (0, dec_seq - CONV_STATE), (0, 0))).reshape(n_s, D_MODEL)
    x, tail_p, u_s, (wg, wu, wd), (bias_p, bias_s, sink_s) = _conv(
        x_prompt.reshape(seq, D_MODEL), x_sample.reshape(n_s, D_MODEL), prev, gm[0], win, cw, wout,
        w_gate, w_up, w_down, dec_seq, bucket_p, bucket_s, rel_table, sink_vec)
    state_conv_prompt = tail_p[SUBLANES - CONV_STATE:].reshape(1, batch, CONV_STATE, D_MODEL)
    state_conv_sample = u_s.reshape(dec_batch, dec_seq, D_MODEL)[:, dec_seq - CONV_STATE:][None]
    x, q_s, kv_new_t = _ffn(x, gf, wg, wu, wd, gfin, 0, gm[1], wqkv, wkvt)

    q_s = q_s.reshape(dec_batch, dec_seq * GROUP, KV_LANES)
    kt = cache_k[0].transpose(0, 2, 3, 1).reshape(dec_batch, KV_LANES, WINDOW)
    vt = cache_v[0].transpose(0, 2, 3, 1).reshape(dec_batch, KV_LANES, WINDOW)
    o_s, kt_out, vt_out = _attn_sample(q_s, kt, vt, kv_new_t, bias_s, sink_s, dec_seq)
    x, k_last, v_last = _attn_prompt(x, o_s.reshape(n_s, D_MODEL), gm[1], wqkv, wo, bias_p, sink_vec, seq)
    cache_k_prompt = k_last.reshape(1, batch, WINDOW, N_KV_HEADS, HEAD_DIM)
    cache_v_prompt = v_last.reshape(1, batch, WINDOW, N_KV_HEADS, HEAD_DIM)
    y_prompt, y_sample = _ffn_final(x, gf, wg, wu, wd, gfin, 1)
    y_prompt = y_prompt.reshape(batch, seq, D_MODEL)
    y_sample = y_sample.reshape(dec_batch, dec_seq, D_MODEL)
    cache_k_sample = kt_out.reshape(dec_batch, N_KV_HEADS, HEAD_DIM, WINDOW).transpose(0, 3, 1, 2)[None]
    cache_v_sample = vt_out.reshape(dec_batch, N_KV_HEADS, HEAD_DIM, WINDOW).transpose(0, 3, 1, 2)[None]

    return (y_prompt, y_sample, state_conv_prompt, state_conv_sample,
            cache_k_prompt, cache_k_sample, cache_v_prompt, cache_v_sample)
```

```python
import functools
import math

import numpy as np
import jax
import jax.numpy as jnp
from jax import lax
from jax.experimental import pallas as pl
from jax.experimental.pallas import tpu as pltpu

D_MODEL = 1024
D_FF = 2816
HEAD_DIM = 64
N_HEADS = 16
N_KV_HEADS = 2
GROUP = N_HEADS // N_KV_HEADS
WINDOW = 128
N_BUCKETS = 32
MAX_DISTANCE = 128
CONV_STATE = 2
EPS = 1e-5
NEG_INF = -1e30
ATTN_SCALE = 1.0 / math.sqrt(HEAD_DIM)

KV_LANES = N_KV_HEADS * HEAD_DIM
QKV_COLS = D_MODEL + 2 * KV_LANES
SUBLANES = 8
BF16_SUBLANES = 16
KEY_TILE = 2 * WINDOW

TOKEN_TILE = 512
ATTN_TILE = 512
MXU_WIDTH = 256
FFN_SPLIT = 2
CONV_SPLIT = 2
SAMPLE_BATCH_TILE = 32
VMEM_LIMIT_BYTES = 56 * 1024 * 1024

F32 = jnp.float32
BF16 = jnp.bfloat16


def _params():
    return pltpu.CompilerParams(dimension_semantics=("arbitrary",), vmem_limit_bytes=VMEM_LIMIT_BYTES)


def _resident(shape):
    zeros = (0,) * len(shape)
    return pl.BlockSpec(shape, lambda *_: zeros, pipeline_mode=pl.Buffered(1))


def _layer(shape, layer):
    index = (layer,) + (0,) * len(shape)
    return pl.BlockSpec((None,) + tuple(shape), lambda *_: index, pipeline_mode=pl.Buffered(1))


def _smem():
    return pl.BlockSpec(memory_space=pltpu.SMEM)


def _rms(x, g):
    return x * lax.rsqrt(jnp.mean(x * x, axis=-1, keepdims=True) + EPS) * g


def _pipeline_specs(n, rows):
    tiles = n // rows
    nxt = pl.BlockSpec((rows, D_MODEL), lambda i: (jnp.minimum(i, tiles - 1), 0))
    cur = pl.BlockSpec((rows, D_MODEL), lambda i: (jnp.maximum(i - 1, 0), 0))
    return tiles, nxt, cur


def _two_stage(first, body):
    step = pl.program_id(0)

    @pl.when(step == 0)
    def _():
        first(0)

    @pl.when(step % 2 == 1)
    def _():
        body(0, 1)

    @pl.when((step % 2 == 0) & (step > 0))
    def _():
        body(1, 0)


def _t5_bucket_np(dist):
    n = np.maximum(dist, 0)
    max_exact = N_BUCKETS // 2
    nf = np.maximum(n, 1).astype(np.float32)
    large = max_exact + (np.log(nf / np.float32(max_exact)) / np.float32(math.log(MAX_DISTANCE / max_exact))
                         * np.float32(N_BUCKETS - max_exact)).astype(np.int32)
    large = np.minimum(large, N_BUCKETS - 1)
    return np.where(n < max_exact, n, large).astype(np.int32)


def _ffn_kernel(x_ref, g_ref, wg_ref, wu_ref, wd_ref, gf_ref, *o_refs, final_norm):
    rows = x_ref.shape[0] // FFN_SPLIT
    halves = [pl.ds(i * rows, rows) for i in range(FFN_SPLIT)]
    acts = []
    for r in halves:
        h = _rms(x_ref[r, :], g_ref[...]).astype(BF16)
        gate = jnp.dot(h, wg_ref[...], preferred_element_type=F32)
        up = jnp.dot(h, wu_ref[...], preferred_element_type=F32)
        acts.append((gate * jax.nn.sigmoid(gate) * up).astype(BF16))
    for r, act in zip(halves, acts):
        y = x_ref[r, :] + jnp.dot(act, wd_ref[...], preferred_element_type=F32)
        if final_norm:
            y = _rms(y, gf_ref[...])
        for o_ref in o_refs:
            o_ref[r, :] = y


def _ffn_weight_specs(layer):
    return [_layer((1, D_MODEL), layer), _layer((D_MODEL, D_FF), layer), _layer((D_MODEL, D_FF), layer),
            _layer((D_FF, D_MODEL), layer), _resident((1, D_MODEL))]


def _ffn_proj_kernel(x_ref, g_ref, wg_ref, wu_ref, wd_ref, gf_ref, gq_ref, wq_ref, wkvt_ref,
                     o_ref, q_ref, kvt_ref):
    _ffn_kernel(x_ref, g_ref, wg_ref, wu_ref, wd_ref, gf_ref, o_ref, final_norm=False)

    @pl.when(pl.program_id(0) == pl.num_programs(0) - 1)
    def _():
        h = _rms(o_ref[...], gq_ref[...]).astype(BF16)
        q_ref[...] = jnp.dot(h, wq_ref[...], preferred_element_type=F32)
        kvt_ref[...] = lax.dot_general(wkvt_ref[...], h, (((1,), (1,)), ((), ())), preferred_element_type=F32)


def _ffn(x, g, wg, wu, wd, g_final, layer, g_q, wqkv, wkvt):
    n = x.shape[0]
    tile = pl.BlockSpec((TOKEN_TILE, D_MODEL), lambda i: (i, 0))

    def once(shape):
        return pl.BlockSpec(shape, lambda i: (0, 0))

    return pl.pallas_call(
        _ffn_proj_kernel,
        grid=(n // TOKEN_TILE,),
        in_specs=[tile] + _ffn_weight_specs(layer) + [_resident((1, D_MODEL)), _resident((D_MODEL, D_MODEL)),
                                                      _resident((2 * KV_LANES, D_MODEL))],
        out_specs=[tile, once((TOKEN_TILE, D_MODEL)), once((2 * KV_LANES, TOKEN_TILE))],
        out_shape=[jax.ShapeDtypeStruct((n, D_MODEL), F32),
                   jax.ShapeDtypeStruct((TOKEN_TILE, D_MODEL), F32),
                   jax.ShapeDtypeStruct((2 * KV_LANES, TOKEN_TILE), F32)],
        compiler_params=_params(),
        name="ffn",
    )(x, g, wg, wu, wd, g_final, g_q, wqkv, wkvt)


def _ffn_final(x, g, wg, wu, wd, g_final, layer):
    n = x.shape[0]
    tiles = n // TOKEN_TILE - 1
    shape = (TOKEN_TILE, D_MODEL)
    y_prompt, y_sample = pl.pallas_call(
        functools.partial(_ffn_kernel, final_norm=True),
        grid=(tiles + 1,),
        in_specs=[pl.BlockSpec(shape, lambda i: (jnp.where(i == 0, tiles, i - 1), 0))] + _ffn_weight_specs(layer),
        out_specs=[pl.BlockSpec(shape, lambda i: (jnp.maximum(i - 1, 0), 0)),
                   pl.BlockSpec(shape, lambda i: (jnp.minimum(i, 1), 0))],
        out_shape=[jax.ShapeDtypeStruct((tiles * TOKEN_TILE, D_MODEL), F32),
                   jax.ShapeDtypeStruct((2 * TOKEN_TILE, D_MODEL), F32)],
        compiler_params=_params(),
        name="ffn_final",
    )(x, g, wg, wu, wd, g_final)
    return y_prompt, y_sample[:TOKEN_TILE]


def _conv_gates(h, win_ref):
    cx = jnp.dot(h, win_ref[:, D_MODEL:], preferred_element_type=F32)
    u = cx[:, :D_MODEL] * cx[:, D_MODEL:]
    b = jnp.dot(h, win_ref[:, :D_MODEL], preferred_element_type=F32)
    return b, u


def _conv_kernel(x_ref, xs_ref, prev_ref, g_ref, win_ref, cw_ref, wout_ref, wg_ref, wu_ref, wd_ref,
                 bucket_p_ref, bucket_s_ref, table_ref, sinks_ref,
                 o_ref, tail_out_ref, us_ref, wg_out_ref, wu_out_ref, wd_out_ref,
                 bias_p_ref, bias_s_ref, sink_s_ref, tail_ref, *, steps):
    step = pl.program_id(0)
    last = pl.num_programs(0) - 1
    cw = cw_ref[...]

    @pl.when(step < last)
    def _():
        wg_out_ref[...] = wg_ref[...].astype(BF16)
        wu_out_ref[...] = wu_ref[...].astype(BF16)
        wd_out_ref[...] = wd_ref[...].astype(BF16)
        tail = jnp.where(step > 0, tail_ref[...], 0.0)
        rows = x_ref.shape[0] // CONV_SPLIT
        groups = [pl.ds(i * rows, rows) for i in range(CONV_SPLIT)]
        gates = [_conv_gates(_rms(x_ref[r, :], g_ref[...]), win_ref) for r in groups]
        r8 = lax.broadcasted_iota(jnp.int32, (SUBLANES, D_MODEL), 0)
        for r, (b, u) in zip(groups, gates):
            u1 = pltpu.roll(u, 1, 0)
            u2 = pltpu.roll(u, 2, 0)
            u1 = jnp.concatenate([jnp.where(r8 < 1, pltpu.roll(tail, 1, 0), u1[:SUBLANES]), u1[SUBLANES:]], axis=0)
            u2 = jnp.concatenate([jnp.where(r8 < 2, pltpu.roll(tail, 2, 0), u2[:SUBLANES]), u2[SUBLANES:]], axis=0)
            v = cw[0:1] * u2 + cw[1:2] * u1 + cw[2:3] * u
            o_ref[r, :] = x_ref[r, :] + jnp.dot(b * v, wout_ref[...], preferred_element_type=F32)
            tail = u[rows - SUBLANES:]
        tail_ref[...] = tail
        tail_out_ref[...] = tail

    @pl.when(step == last)
    def _():
        x = xs_ref[...]
        b, u = _conv_gates(_rms(x, g_ref[...]), win_ref)
        rows = u.shape[0]
        prev = prev_ref[...]
        t = lax.broadcasted_iota(jnp.int32, u.shape, 0) % steps
        u1 = jnp.where(t >= 1, pltpu.roll(u, 1, 0), pltpu.roll(prev, rows - 1, 0))
        u2 = jnp.where(t >= 2, pltpu.roll(u, 2, 0), prev)
        v = cw[0:1] * u2 + cw[1:2] * u1 + cw[2:3] * u
        o_ref[...] = x + jnp.dot(b * v, wout_ref[...], preferred_element_type=F32)
        us_ref[...] = u
        _bias_tables(bucket_p_ref, bucket_s_ref, table_ref, sinks_ref, bias_p_ref, bias_s_ref, sink_s_ref)


def _conv(x, xs, prev, g, win, cw, wout, w_gate, w_up, w_down, steps, bucket_p, bucket_s, table, sinks):
    n = x.shape[0]
    tiles = n // TOKEN_TILE
    assert xs.shape == (TOKEN_TILE, D_MODEL)
    tile_in = pl.BlockSpec((TOKEN_TILE, D_MODEL), lambda i: (jnp.minimum(i, tiles - 1), 0))
    tile_out = pl.BlockSpec((TOKEN_TILE, D_MODEL), lambda i: (i, 0))
    sample = _resident((TOKEN_TILE, D_MODEL))

    def slabs(w):
        flat = w.reshape(-1, w.shape[-1])
        rows = flat.shape[0] // tiles
        assert rows * tiles == flat.shape[0] and rows % BF16_SUBLANES == 0
        return flat, pl.BlockSpec((rows, flat.shape[1]), lambda i: (jnp.minimum(i, tiles - 1), 0))

    (wg, wg_spec), (wu, wu_spec), (wd, wd_spec) = slabs(w_gate), slabs(w_up), slabs(w_down)
    q_rows = bucket_s.shape[0]
    table_shapes = [(N_HEADS, WINDOW, KEY_TILE), (N_KV_HEADS, q_rows, KEY_TILE), (N_KV_HEADS, q_rows, KV_LANES)]
    y, tail, u_s, wg_b, wu_b, wd_b, bias_p, bias_s, sink_s = pl.pallas_call(
        functools.partial(_conv_kernel, steps=steps),
        grid=(tiles + 1,),
        in_specs=[tile_in, sample, sample, _resident((1, D_MODEL)), _resident((D_MODEL, 3 * D_MODEL)),
                  _resident((3, D_MODEL)), _resident((D_MODEL, D_MODEL)), wg_spec, wu_spec, wd_spec,
                  _resident((SUBLANES, KEY_TILE)), _resident((q_rows, KEY_TILE)), _smem(), _smem()],
        out_specs=[tile_out, pl.BlockSpec((SUBLANES, D_MODEL), lambda i: (0, 0)),
                   pl.BlockSpec((TOKEN_TILE, D_MODEL), lambda i: (0, 0)), wg_spec, wu_spec, wd_spec]
        + [pl.BlockSpec(shape, lambda i: (0, 0, 0)) for shape in table_shapes],
        out_shape=[jax.ShapeDtypeStruct((n + TOKEN_TILE, D_MODEL), F32),
                   jax.ShapeDtypeStruct((SUBLANES, D_MODEL), F32),
                   jax.ShapeDtypeStruct((TOKEN_TILE, D_MODEL), F32),
                   jax.ShapeDtypeStruct(wg.shape, BF16),
                   jax.ShapeDtypeStruct(wu.shape, BF16),
                   jax.ShapeDtypeStruct(wd.shape, BF16)]
        + [jax.ShapeDtypeStruct(shape, F32) for shape in table_shapes],
        scratch_shapes=[pltpu.VMEM((SUBLANES, D_MODEL), F32)],
        compiler_params=_params(),
        name="conv",
    )(x, xs, prev, g, win, cw, wout, wg, wu, wd, bucket_p, bucket_s, table, sinks)
    weights = (wg_b.reshape(w_gate.shape), wu_b.reshape(w_up.shape), wd_b.reshape(w_down.shape))
    return y, tail, u_s, weights, (bias_p, bias_s, sink_s)


def _lookup_bias(bucket, table_ref, head):
    acc = jnp.zeros(bucket.shape, F32)
    for b in range(N_BUCKETS):
        acc = jnp.where(bucket == b, table_ref[b, head], acc)
    return acc


def _bias_tables(bucket_p_ref, bucket_s_ref, table_ref, sinks_ref, bias_p_ref, bias_s_ref, sink_s_ref):
    bucket_row0 = bucket_p_ref[...]
    for head in range(N_HEADS):
        row0 = _lookup_bias(bucket_row0, table_ref, head)
        rows = jnp.concatenate([row0] * (WINDOW // SUBLANES), axis=0)
        bias_p_ref[head] = pltpu.roll(rows, 0, 1, stride=1, stride_axis=0)
    bucket_s = bucket_s_ref[...]
    grp = lax.broadcasted_iota(jnp.int32, bucket_s.shape, 0) % GROUP
    grp_l = lax.broadcasted_iota(jnp.int32, (bucket_s.shape[0], KV_LANES), 0) % GROUP
    for kv in range(N_KV_HEADS):
        bias = jnp.zeros(bucket_s.shape, F32)
        sink = jnp.zeros((bucket_s.shape[0], KV_LANES), F32)
        for gi in range(GROUP):
            head = kv * GROUP + gi
            bias = jnp.where(grp == gi, _lookup_bias(bucket_s, table_ref, head), bias)
            sink = jnp.where(grp_l == gi, sinks_ref[head], sink)
        bias_s_ref[kv] = bias
        sink_s_ref[kv] = sink


def _attn_prompt_kernel(xn_ref, xc_ref, xs_ref, os_ref, g_ref, wqkv_ref, wo_ref, bias_ref, sinks_ref,
                        o_ref, klast_ref, vlast_ref, qbuf, kbuf, vbuf, kcar, vcar, obuf):
    step = pl.program_id(0)
    rows = xn_ref.shape[0]
    n_blk = rows // WINDOW
    n_chunks = D_MODEL // MXU_WIDTH
    blocks_per_chunk = n_blk // n_chunks

    def norm_next():
        return _rms(xn_ref[...], g_ref[...]).astype(BF16)

    def project_chunk(h, slot, c):
        c0 = c * MXU_WIDTH
        part = jnp.dot(h, wqkv_ref[:, c0:c0 + MXU_WIDTH], preferred_element_type=F32)
        qbuf[slot, :, c0:c0 + MXU_WIDTH] = (part * ATTN_SCALE).astype(BF16)
        if c < n_chunks - 1:
            return
        kv_part = jnp.dot(h, wqkv_ref[:, D_MODEL:], preferred_element_type=F32)
        k = kv_part[:, :KV_LANES]
        v = kv_part[:, KV_LANES:]
        klast_ref[...] = k[rows - WINDOW:]
        vlast_ref[...] = v[rows - WINDOW:]
        lane = lax.broadcasted_iota(jnp.int32, (rows, KV_LANES), 1)
        for kv, half in enumerate((lane < HEAD_DIM, lane >= HEAD_DIM)):
            k_half = jnp.where(half, k, 0.0)
            v_half = jnp.where(half, v, 0.0)
            kbuf[slot, kv] = k_half.astype(BF16)
            vbuf[slot, kv] = v_half.astype(BF16)
            kcar[slot, kv] = k_half[rows - WINDOW:]
            vcar[slot, kv] = v_half[rows - WINDOW:]

    def first(slot):
        o_ref[...] = xs_ref[...] + jnp.dot(os_ref[...].astype(BF16), wo_ref[...], preferred_element_type=F32)
        kcar[...] = jnp.zeros(kcar.shape, F32)
        vcar[...] = jnp.zeros(vcar.shape, F32)
        h = norm_next()
        for c in range(n_chunks):
            project_chunk(h, slot, c)

    def body(cur, nxt):
        k_prev = [jnp.where(step > 1, kcar[nxt, kv], 0.0).astype(BF16) for kv in range(N_KV_HEADS)]
        v_prev = [jnp.where(step > 1, vcar[nxt, kv], 0.0).astype(BF16) for kv in range(N_KV_HEADS)]
        h = norm_next()

        qi = lax.broadcasted_iota(jnp.int32, (WINDOW, KEY_TILE), 0)
        kj = lax.broadcasted_iota(jnp.int32, (WINDOW, KEY_TILE), 1)
        band = (kj >= qi) & (kj <= qi + WINDOW)
        lane_q = lax.broadcasted_iota(jnp.int32, (WINDOW, KV_LANES), 1)
        lane_k = lax.broadcasted_iota(jnp.int32, (KEY_TILE, KV_LANES), 1)
        ones_half = [(lane_k < HEAD_DIM).astype(F32).astype(BF16), (lane_k >= HEAD_DIM).astype(F32).astype(BF16)]

        def block_scores(blk):
            r0 = blk * WINDOW
            qs = jnp.concatenate(
                [qbuf[cur, r0:r0 + WINDOW, gi * KV_LANES:(gi + 1) * KV_LANES] for gi in range(GROUP)], axis=0)
            if blk == 0:
                keys = [jnp.concatenate([k_prev[kv], kbuf[cur, kv, 0:WINDOW, :]], axis=0) for kv in range(N_KV_HEADS)]
            else:
                keys = [kbuf[cur, kv, r0 - WINDOW:r0 + WINDOW, :] for kv in range(N_KV_HEADS)]
            return [lax.dot_general(qs, keys[kv], (((1,), (1,)), ((), ())), preferred_element_type=F32)
                    for kv in range(N_KV_HEADS)]

        scores = block_scores(0)
        for blk in range(n_blk):
            r0 = blk * WINDOW
            if blk == 0:
                mask = band & (kj >= jnp.where(step > 1, 0, WINDOW))
                vals = [jnp.concatenate([v_prev[kv], vbuf[cur, kv, 0:WINDOW, :]], axis=0) for kv in range(N_KV_HEADS)]
            else:
                mask = band
                vals = [vbuf[cur, kv, r0 - WINDOW:r0 + WINDOW, :] for kv in range(N_KV_HEADS)]
            next_scores = block_scores(blk + 1) if blk + 1 < n_blk else None
            if blk % blocks_per_chunk == 0:
                project_chunk(h, nxt, blk // blocks_per_chunk)
            probs, maxes = [], []
            for kv in range(N_KV_HEADS):
                s_all = scores[kv]
                p_kv, m_kv = [], []
                for gi in range(GROUP):
                    head = kv * GROUP + gi
                    s = s_all[gi * WINDOW:(gi + 1) * WINDOW]
                    s = jnp.where(mask, s + bias_ref[head], NEG_INF)
                    m = jnp.maximum(jnp.max(s, axis=-1, keepdims=True), sinks_ref[head])
                    p_kv.append(jnp.exp(s - m).astype(BF16))
                    m_kv.append(m)
                probs.append(jnp.concatenate(p_kv, axis=0))
                maxes.append(m_kv)
            scores = next_scores
            acc = None
            for kv in range(N_KV_HEADS):
                rhs = jnp.concatenate([vals[kv], ones_half[kv]], axis=1)
                part = jnp.dot(probs[kv], rhs, preferred_element_type=F32)
                acc = part if acc is None else acc + part
            outs = []
            for gi in range(GROUP):
                a = acc[gi * WINDOW:(gi + 1) * WINDOW]
                sink_term = jnp.where(lane_q < HEAD_DIM,
                                      jnp.exp(sinks_ref[gi] - maxes[0][gi]),
                                      jnp.exp(sinks_ref[GROUP + gi] - maxes[1][gi]))
                outs.append(a[:, :KV_LANES] / (a[:, KV_LANES:] + sink_term))
            obuf[r0:r0 + WINDOW, :] = jnp.concatenate(outs, axis=1).astype(BF16)

        o_ref[...] = xc_ref[...] + jnp.dot(obuf[...], wo_ref[...], preferred_element_type=F32)

    _two_stage(first, body)


def _attn_prompt(x, o_sample, g, wqkv, wo, bias, sinks, n):
    tiles, nxt, cur = _pipeline_specs(n, ATTN_TILE)
    assert x.shape[0] == n + ATTN_TILE and o_sample.shape == (ATTN_TILE, D_MODEL)
    out = pl.BlockSpec((ATTN_TILE, D_MODEL), lambda i: (jnp.where(i == 0, tiles, i - 1), 0))
    sample = pl.BlockSpec((ATTN_TILE, D_MODEL), lambda i: (tiles, 0), pipeline_mode=pl.Buffered(1))
    last = pl.BlockSpec((WINDOW, KV_LANES), lambda i: (0, 0))
    return pl.pallas_call(
        _attn_prompt_kernel,
        grid=(tiles + 1,),
        in_specs=[nxt, cur, sample, _resident((ATTN_TILE, D_MODEL)), _resident((1, D_MODEL)),
                  _resident((D_MODEL, QKV_COLS)),
                  _resident((D_MODEL, D_MODEL)), _resident((N_HEADS, WINDOW, KEY_TILE)), _smem()],
        out_specs=[out, last, last],
        out_shape=[jax.ShapeDtypeStruct(x.shape, F32),
                   jax.ShapeDtypeStruct((WINDOW, KV_LANES), F32),
                   jax.ShapeDtypeStruct((WINDOW, KV_LANES), F32)],
        scratch_shapes=[pltpu.VMEM((2, ATTN_TILE, D_MODEL), BF16),
                        pltpu.VMEM((2, N_KV_HEADS, ATTN_TILE, KV_LANES), BF16),
                        pltpu.VMEM((2, N_KV_HEADS, ATTN_TILE, KV_LANES), BF16),
                        pltpu.VMEM((2, N_KV_HEADS, WINDOW, KV_LANES), F32),
                        pltpu.VMEM((2, N_KV_HEADS, WINDOW, KV_LANES), F32),
                        pltpu.VMEM((ATTN_TILE, D_MODEL), BF16)],
        compiler_params=_params(),
        name="attn_prompt",
    )(x, x, x, o_sample, g, wqkv, wo, bias, sinks)


def _attn_sample_kernel(q_ref, kt_ref, vt_ref, kvn_ref, bias_ref, sink_ref, o_ref, kout_ref, vout_ref, *, steps):
    q_rows = GROUP * steps
    qi = lax.broadcasted_iota(jnp.int32, (q_rows, KEY_TILE), 0) // GROUP
    kj = lax.broadcasted_iota(jnp.int32, (q_rows, KEY_TILE), 1)
    new_at = KEY_TILE - steps
    mask = ((kj < WINDOW) & (kj >= qi)) | ((kj >= new_at) & (kj - new_at <= qi))
    row = lax.broadcasted_iota(jnp.int32, (KV_LANES, KEY_TILE), 0)
    kv_row = (row < HEAD_DIM, row >= HEAD_DIM)
    lane_q = lax.broadcasted_iota(jnp.int32, (q_rows, KV_LANES), 1)
    lane_c = lax.broadcasted_iota(jnp.int32, (KV_LANES, WINDOW), 1)
    new_k = kvn_ref[0:KV_LANES, :]
    new_v = kvn_ref[KV_LANES:, :]

    def new_cols(new_t, b):
        return pltpu.roll(new_t, (WINDOW - steps - b * steps) % WINDOW, 1)

    def shifted_cache(cache_t, new_last):
        return jnp.where(lane_c < WINDOW - steps, pltpu.roll(cache_t, WINDOW - steps, 1), new_last)

    batches = range(kt_ref.shape[0])
    q_all = (q_ref[...] * ATTN_SCALE).reshape(q_ref.shape[0], GROUP, KV_LANES)
    scores = []
    for b in batches:
        kt = kt_ref[b]
        new_last = new_cols(new_k, b)
        kout_ref[b] = shifted_cache(kt, new_last)
        keys = jnp.concatenate([kt, new_last], axis=1)
        q = q_all[b * steps:(b + 1) * steps].reshape(q_rows, KV_LANES).astype(BF16)
        scores.append([jnp.dot(q, jnp.where(kv_row[kv], keys, 0.0).astype(BF16), preferred_element_type=F32)
                       for kv in range(N_KV_HEADS)])
    probs, sink_terms = [], []
    for b in batches:
        p_b, t_b = [], []
        for kv in range(N_KV_HEADS):
            s = jnp.where(mask, scores[b][kv] + bias_ref[kv], NEG_INF)
            sink = sink_ref[kv][:, 0:1]
            m = jnp.maximum(jnp.max(s, axis=-1, keepdims=True), sink)
            p_b.append(jnp.exp(s - m).astype(BF16))
            t_b.append(jnp.exp(sink - m))
        probs.append(p_b)
        sink_terms.append(t_b)
    outs = []
    for b in batches:
        vt = vt_ref[b]
        new_last = new_cols(new_v, b)
        vout_ref[b] = shifted_cache(vt, new_last)
        vals = jnp.concatenate([vt, new_last], axis=1)
        acc = None
        for kv in range(N_KV_HEADS):
            rhs = jnp.concatenate([jnp.where(kv_row[kv], vals, 0.0), jnp.where(kv_row[kv], 1.0, 0.0)], axis=0)
            part = lax.dot_general(probs[b][kv], rhs.astype(BF16), (((1,), (1,)), ((), ())),
                                   preferred_element_type=F32)
            acc = part if acc is None else acc + part
        denom = acc[:, KV_LANES:] + jnp.where(lane_q < HEAD_DIM, sink_terms[b][0], sink_terms[b][1])
        outs.append((acc[:, :KV_LANES] / denom).reshape(steps, GROUP, KV_LANES))
    o_ref[...] = jnp.concatenate(outs, axis=0).reshape(o_ref.shape)


def _attn_sample(q, kt, vt, kvn, bias, sink, steps):
    nb = kt.shape[0]
    q_rows = GROUP * steps
    bt = SAMPLE_BATCH_TILE
    assert bt * steps == WINDOW and nb % bt == 0

    def batch_spec(r):
        return pl.BlockSpec((bt, r, KV_LANES), lambda i: (i, 0, 0))

    rows_spec = pl.BlockSpec((bt * steps, D_MODEL), lambda i: (i, 0))
    return pl.pallas_call(
        functools.partial(_attn_sample_kernel, steps=steps),
        grid=(nb // bt,),
        in_specs=[rows_spec, batch_spec(KV_LANES), batch_spec(KV_LANES),
                  pl.BlockSpec((2 * KV_LANES, bt * steps), lambda i: (0, i)),
                  _resident((N_KV_HEADS, q_rows, KEY_TILE)), _resident((N_KV_HEADS, q_rows, KV_LANES))],
        out_specs=[rows_spec, batch_spec(KV_LANES), batch_spec(KV_LANES)],
        out_shape=[jax.ShapeDtypeStruct((nb * steps, D_MODEL), F32),
                   jax.ShapeDtypeStruct((nb, KV_LANES, WINDOW), F32),
                   jax.ShapeDtypeStruct((nb, KV_LANES, WINDOW), F32)],
        compiler_params=_params(),
        name="attn_sample",
    )(q, kt, vt, kvn, bias, sink)


def kernel(x_prompt, x_sample, state_conv, cache_k, cache_v, g_mix, g_ffn, g_final, w_conv_in, conv_w,
           w_conv_out, w_q, w_k, w_v, w_o, sinks, rel_table, w_gate, w_up, w_down):
    batch, seq, _ = x_prompt.shape
    dec_batch, dec_seq, _ = x_sample.shape
    assert batch == 1 and seq % TOKEN_TILE == 0 and seq % ATTN_TILE == 0
    assert ATTN_TILE % (WINDOW * (D_MODEL // MXU_WIDTH)) == 0
    assert dec_batch % SAMPLE_BATCH_TILE == 0 and dec_seq <= SUBLANES
    assert dec_batch * dec_seq == TOKEN_TILE, "the sample rows ride the token-wise kernels as one tile"
    assert g_mix.shape[0] == 2, "layer 0 is the conv mixer, layer 1 the attention mixer"

    win = w_conv_in[0]
    wout = w_conv_out[0]
    wq = w_q[0].reshape(D_MODEL, N_KV_HEADS, GROUP, HEAD_DIM).transpose(0, 2, 1, 3).reshape(D_MODEL, D_MODEL)
    wqkv = jnp.concatenate([wq, w_k[0], w_v[0]], axis=1).astype(BF16)
    wo = w_o[0].reshape(N_KV_HEADS, GROUP, HEAD_DIM, D_MODEL).transpose(1, 0, 2, 3).reshape(D_MODEL, D_MODEL)
    wo = wo.astype(BF16)
    wkvt = jnp.concatenate([w_k[0], w_v[0]], axis=1).T.astype(BF16)
    gm = g_mix.reshape(2, 1, D_MODEL)
    gf = g_ffn.reshape(2, 1, D_MODEL)
    gfin = g_final.reshape(1, D_MODEL)
    cw = conv_w[0]
    sink_vec = sinks[0]

    bucket_p = jnp.asarray(np.tile(_t5_bucket_np(WINDOW - np.arange(KEY_TILE)[None, :]), (SUBLANES, 1)))
    key_s = np.arange(KEY_TILE)[None, :]
    key_pos = np.where(key_s < WINDOW, key_s, key_s - (KEY_TILE - dec_seq) + WINDOW)
    dist_s = np.arange(dec_seq)[:, None] + WINDOW - key_pos
    bucket_s = jnp.asarray(np.repeat(_t5_bucket_np(dist_s), GROUP, axis=0))

    n_s = dec_batch * dec_seq
    prev = jnp.pad(state_conv[0], ((0, 0), (0, dec_seq - CONV_STATE), (0, 0))).reshape(n_s, D_MODEL)
    x, tail_p, u_s, (wg, wu, wd), (bias_p, bias_s, sink_s) = _conv(
        x_prompt.reshape(seq, D_MODEL), x_sample.reshape(n_s, D_MODEL), prev, gm[0], win, cw, wout,
        w_gate, w_up, w_down, dec_seq, bucket_p, bucket_s, rel_table, sink_vec)
    state_conv_prompt = tail_p[SUBLANES - CONV_STATE:].reshape(1, batch, CONV_STATE, D_MODEL)
    state_conv_sample = u_s.reshape(dec_batch, dec_seq, D_MODEL)[:, dec_seq - CONV_STATE:][None]
    x, q_s, kv_new_t = _ffn(x, gf, wg, wu, wd, gfin, 0, gm[1], wqkv, wkvt)

    kt = cache_k[0].transpose(0, 2, 3, 1).reshape(dec_batch, KV_LANES, WINDOW)
    vt = cache_v[0].transpose(0, 2, 3, 1).reshape(dec_batch, KV_LANES, WINDOW)
    o_s, kt_out, vt_out = _attn_sample(q_s, kt, vt, kv_new_t, bias_s, sink_s, dec_seq)
    x, k_last, v_last = _attn_prompt(x, o_s, gm[1], wqkv, wo, bias_p, sink_vec, seq)
    cache_k_prompt = k_last.reshape(1, batch, WINDOW, N_KV_HEADS, HEAD_DIM)
    cache_v_prompt = v_last.reshape(1, batch, WINDOW, N_KV_HEADS, HEAD_DIM)
    y_prompt, y_sample = _ffn_final(x, gf, wg, wu, wd, gfin, 1)
    y_prompt = y_prompt.reshape(batch, seq, D_MODEL)
    y_sample = y_sample.reshape(dec_batch, dec_seq, D_MODEL)
    cache_k_sample = kt_out.reshape(dec_batch, N_KV_HEADS, HEAD_DIM, WINDOW).transpose(0, 3, 1, 2)[None]
    cache_v_sample = vt_out.reshape(dec_batch, N_KV_HEADS, HEAD_DIM, WINDOW).transpose(0, 3, 1, 2)[None]

    return (y_prompt, y_sample, state_conv_prompt, state_conv_sample,
            cache_k_prompt, cache_k_sample, cache_v_prompt, cache_v_sample)
```

```python
import functools
import math

import numpy as np
import jax
import jax.numpy as jnp
from jax import lax
from jax.experimental import pallas as pl
from jax.experimental.pallas import tpu as pltpu

D_MODEL = 1024
D_FF = 2816
HEAD_DIM = 64
N_HEADS = 16
N_KV_HEADS = 2
GROUP = N_HEADS // N_KV_HEADS
WINDOW = 128
N_BUCKETS = 32
MAX_DISTANCE = 128
CONV_STATE = 2
EPS = 1e-5
NEG_INF = -1e30
ATTN_SCALE = 1.0 / math.sqrt(HEAD_DIM)

KV_LANES = N_KV_HEADS * HEAD_DIM
QKV_COLS = D_MODEL + 2 * KV_LANES
SUBLANES = 8
BF16_SUBLANES = 16
KEY_TILE = 2 * WINDOW

TOKEN_TILE = 512
ATTN_TILE = 512
MXU_WIDTH = 256
FFN_SPLIT = 2
CONV_SPLIT = 2
SAMPLE_BATCH_TILE = 32
VMEM_LIMIT_BYTES = 56 * 1024 * 1024

F32 = jnp.float32
BF16 = jnp.bfloat16


def _params():
    return pltpu.CompilerParams(dimension_semantics=("arbitrary",), vmem_limit_bytes=VMEM_LIMIT_BYTES)


def _resident(shape):
    zeros = (0,) * len(shape)
    return pl.BlockSpec(shape, lambda *_: zeros, pipeline_mode=pl.Buffered(1))


def _layer(shape, layer):
    index = (layer,) + (0,) * len(shape)
    return pl.BlockSpec((None,) + tuple(shape), lambda *_: index, pipeline_mode=pl.Buffered(1))


def _smem():
    return pl.BlockSpec(memory_space=pltpu.SMEM)


def _rms(x, g):
    return x * lax.rsqrt(jnp.mean(x * x, axis=-1, keepdims=True) + EPS) * g


def _pipeline_specs(n, rows):
    tiles = n // rows
    nxt = pl.BlockSpec((rows, D_MODEL), lambda i: (jnp.minimum(i, tiles - 1), 0))
    cur = pl.BlockSpec((rows, D_MODEL), lambda i: (jnp.maximum(i - 1, 0), 0))
    return tiles, nxt, cur


def _two_stage(first, body):
    step = pl.program_id(0)

    @pl.when(step == 0)
    def _():
        first(0)

    @pl.when(step % 2 == 1)
    def _():
        body(0, 1)

    @pl.when((step % 2 == 0) & (step > 0))
    def _():
        body(1, 0)


def _t5_bucket_np(dist):
    n = np.maximum(dist, 0)
    max_exact = N_BUCKETS // 2
    nf = np.maximum(n, 1).astype(np.float32)
    large = max_exact + (np.log(nf / np.float32(max_exact)) / np.float32(math.log(MAX_DISTANCE / max_exact))
                         * np.float32(N_BUCKETS - max_exact)).astype(np.int32)
    large = np.minimum(large, N_BUCKETS - 1)
    return np.where(n < max_exact, n, large).astype(np.int32)


def _ffn_kernel(x_ref, g_ref, wg_ref, wu_ref, wd_ref, gf_ref, *o_refs, final_norm):
    rows = x_ref.shape[0] // FFN_SPLIT
    halves = [pl.ds(i * rows, rows) for i in range(FFN_SPLIT)]
    acts = []
    for r in halves:
        h = _rms(x_ref[r, :], g_ref[...]).astype(BF16)
        gate = jnp.dot(h, wg_ref[...], preferred_element_type=F32)
        up = jnp.dot(h, wu_ref[...], preferred_element_type=F32)
        acts.append((gate * jax.nn.sigmoid(gate) * up).astype(BF16))
    for r, act in zip(halves, acts):
        y = x_ref[r, :] + jnp.dot(act, wd_ref[...], preferred_element_type=F32)
        if final_norm:
            y = _rms(y, gf_ref[...])
        for o_ref in o_refs:
            o_ref[r, :] = y


def _ffn_weight_specs(layer):
    return [_layer((1, D_MODEL), layer), _layer((D_MODEL, D_FF), layer), _layer((D_MODEL, D_FF), layer),
            _layer((D_FF, D_MODEL), layer), _resident((1, D_MODEL))]


def _ffn_proj_kernel(x_ref, g_ref, wg_ref, wu_ref, wd_ref, gf_ref, gq_ref, wq_ref, wkvt_ref,
                     o_ref, q_ref, kvt_ref):
    _ffn_kernel(x_ref, g_ref, wg_ref, wu_ref, wd_ref, gf_ref, o_ref, final_norm=False)

    @pl.when(pl.program_id(0) == pl.num_programs(0) - 1)
    def _():
        h = _rms(o_ref[...], gq_ref[...]).astype(BF16)
        q_ref[...] = jnp.dot(h, wq_ref[...], preferred_element_type=F32)
        kvt_ref[...] = lax.dot_general(wkvt_ref[...], h, (((1,), (1,)), ((), ())), preferred_element_type=F32)


def _ffn(x, g, wg, wu, wd, g_final, layer, g_q, wqkv, wkvt):
    n = x.shape[0]
    tile = pl.BlockSpec((TOKEN_TILE, D_MODEL), lambda i: (i, 0))

    def once(shape):
        return pl.BlockSpec(shape, lambda i: (0, 0))

    return pl.pallas_call(
        _ffn_proj_kernel,
        grid=(n // TOKEN_TILE,),
        in_specs=[tile] + _ffn_weight_specs(layer) + [_resident((1, D_MODEL)), _resident((D_MODEL, D_MODEL)),
                                                      _resident((2 * KV_LANES, D_MODEL))],
        out_specs=[tile, once((TOKEN_TILE, D_MODEL)), once((2 * KV_LANES, TOKEN_TILE))],
        out_shape=[jax.ShapeDtypeStruct((n, D_MODEL), F32),
                   jax.ShapeDtypeStruct((TOKEN_TILE, D_MODEL), F32),
                   jax.ShapeDtypeStruct((2 * KV_LANES, TOKEN_TILE), F32)],
        compiler_params=_params(),
        name="ffn",
    )(x, g, wg, wu, wd, g_final, g_q, wqkv, wkvt)


def _ffn_final(x, g, wg, wu, wd, g_final, layer):
    n = x.shape[0]
    tiles = n // TOKEN_TILE - 1
    shape = (TOKEN_TILE, D_MODEL)
    y_prompt, y_sample = pl.pallas_call(
        functools.partial(_ffn_kernel, final_norm=True),
        grid=(tiles + 1,),
        in_specs=[pl.BlockSpec(shape, lambda i: (jnp.where(i == 0, tiles, i - 1), 0))] + _ffn_weight_specs(layer),
        out_specs=[pl.BlockSpec(shape, lambda i: (jnp.maximum(i - 1, 0), 0)),
                   pl.BlockSpec(shape, lambda i: (jnp.minimum(i, 1), 0))],
        out_shape=[jax.ShapeDtypeStruct((tiles * TOKEN_TILE, D_MODEL), F32),
                   jax.ShapeDtypeStruct((2 * TOKEN_TILE, D_MODEL), F32)],
        compiler_params=_params(),
        name="ffn_final",
    )(x, g, wg, wu, wd, g_final)
    return y_prompt, y_sample[:TOKEN_TILE]


def _conv_gates(h, win_ref):
    cx = jnp.dot(h, win_ref[:, D_MODEL:], preferred_element_type=F32)
    u = cx[:, :D_MODEL] * cx[:, D_MODEL:]
    b = jnp.dot(h, win_ref[:, :D_MODEL], preferred_element_type=F32)
    return b, u


def _conv_kernel(x_ref, xs_ref, prev_ref, g_ref, win_ref, cw_ref, wout_ref, wg_ref, wu_ref, wd_ref,
                 bucket_p_ref, bucket_s_ref, table_ref, sinks_ref,
                 o_ref, tail_out_ref, us_ref, wg_out_ref, wu_out_ref, wd_out_ref,
                 bias_p_ref, bias_s_ref, sink_s_ref, tail_ref, *, steps):
    step = pl.program_id(0)
    last = pl.num_programs(0) - 1
    cw = cw_ref[...]

    @pl.when(step < last)
    def _():
        wg_out_ref[...] = wg_ref[...].astype(BF16)
        wu_out_ref[...] = wu_ref[...].astype(BF16)
        wd_out_ref[...] = wd_ref[...].astype(BF16)
        tail = jnp.where(step > 0, tail_ref[...], 0.0)
        rows = x_ref.shape[0] // CONV_SPLIT
        groups = [pl.ds(i * rows, rows) for i in range(CONV_SPLIT)]
        gates = [_conv_gates(_rms(x_ref[r, :], g_ref[...]), win_ref) for r in groups]
        r8 = lax.broadcasted_iota(jnp.int32, (SUBLANES, D_MODEL), 0)
        for r, (b, u) in zip(groups, gates):
            u1 = pltpu.roll(u, 1, 0)
            u2 = pltpu.roll(u, 2, 0)
            u1 = jnp.concatenate([jnp.where(r8 < 1, pltpu.roll(tail, 1, 0), u1[:SUBLANES]), u1[SUBLANES:]], axis=0)
            u2 = jnp.concatenate([jnp.where(r8 < 2, pltpu.roll(tail, 2, 0), u2[:SUBLANES]), u2[SUBLANES:]], axis=0)
            v = cw[0:1] * u2 + cw[1:2] * u1 + cw[2:3] * u
            o_ref[r, :] = x_ref[r, :] + jnp.dot(b * v, wout_ref[...], preferred_element_type=F32)
            tail = u[rows - SUBLANES:]
        tail_ref[...] = tail
        tail_out_ref[...] = tail

    @pl.when(step == last)
    def _():
        x = xs_ref[...].reshape(o_ref.shape)
        b, u = _conv_gates(_rms(x, g_ref[...]), win_ref)
        rows = u.shape[0]
        prev = prev_ref[...]
        t = lax.broadcasted_iota(jnp.int32, u.shape, 0) % steps
        u1 = jnp.where(t >= 1, pltpu.roll(u, 1, 0), pltpu.roll(prev, rows - 1, 0))
        u2 = jnp.where(t >= 2, pltpu.roll(u, 2, 0), prev)
        v = cw[0:1] * u2 + cw[1:2] * u1 + cw[2:3] * u
        o_ref[...] = x + jnp.dot(b * v, wout_ref[...], preferred_element_type=F32)
        us_ref[...] = u.reshape(xs_ref.shape)[:, steps - CONV_STATE:, :]
        _bias_tables(bucket_p_ref, bucket_s_ref, table_ref, sinks_ref, bias_p_ref, bias_s_ref, sink_s_ref)


def _conv(x, xs, prev, g, win, cw, wout, w_gate, w_up, w_down, steps, bucket_p, bucket_s, table, sinks):
    n = x.shape[0]
    tiles = n // TOKEN_TILE
    batches = xs.shape[0]
    assert xs.shape == (batches, steps, D_MODEL) and batches * steps == TOKEN_TILE
    tile_in = pl.BlockSpec((TOKEN_TILE, D_MODEL), lambda i: (jnp.minimum(i, tiles - 1), 0))
    tile_out = pl.BlockSpec((TOKEN_TILE, D_MODEL), lambda i: (i, 0))
    sample = _resident((TOKEN_TILE, D_MODEL))

    def slabs(w):
        flat = w.reshape(-1, w.shape[-1])
        rows = flat.shape[0] // tiles
        assert rows * tiles == flat.shape[0] and rows % BF16_SUBLANES == 0
        return flat, pl.BlockSpec((rows, flat.shape[1]), lambda i: (jnp.minimum(i, tiles - 1), 0))

    (wg, wg_spec), (wu, wu_spec), (wd, wd_spec) = slabs(w_gate), slabs(w_up), slabs(w_down)
    q_rows = bucket_s.shape[0]
    table_shapes = [(N_HEADS, WINDOW, KEY_TILE), (N_KV_HEADS, q_rows, KEY_TILE), (N_KV_HEADS, q_rows, KV_LANES)]
    y, tail, u_s, wg_b, wu_b, wd_b, bias_p, bias_s, sink_s = pl.pallas_call(
        functools.partial(_conv_kernel, steps=steps),
        grid=(tiles + 1,),
        in_specs=[tile_in, _resident(xs.shape), sample, _resident((1, D_MODEL)), _resident((D_MODEL, 3 * D_MODEL)),
                  _resident((3, D_MODEL)), _resident((D_MODEL, D_MODEL)), wg_spec, wu_spec, wd_spec,
                  _resident((SUBLANES, KEY_TILE)), _resident((q_rows, KEY_TILE)), _smem(), _smem()],
        out_specs=[tile_out, pl.BlockSpec((SUBLANES, D_MODEL), lambda i: (0, 0)),
                   pl.BlockSpec((batches, CONV_STATE, D_MODEL), lambda i: (0, 0, 0)), wg_spec, wu_spec, wd_spec]
        + [pl.BlockSpec(shape, lambda i: (0, 0, 0)) for shape in table_shapes],
        out_shape=[jax.ShapeDtypeStruct((n + TOKEN_TILE, D_MODEL), F32),
                   jax.ShapeDtypeStruct((SUBLANES, D_MODEL), F32),
                   jax.ShapeDtypeStruct((batches, CONV_STATE, D_MODEL), F32),
                   jax.ShapeDtypeStruct(wg.shape, BF16),
                   jax.ShapeDtypeStruct(wu.shape, BF16),
                   jax.ShapeDtypeStruct(wd.shape, BF16)]
        + [jax.ShapeDtypeStruct(shape, F32) for shape in table_shapes],
        scratch_shapes=[pltpu.VMEM((SUBLANES, D_MODEL), F32)],
        compiler_params=_params(),
        name="conv",
    )(x, xs, prev, g, win, cw, wout, wg, wu, wd, bucket_p, bucket_s, table, sinks)
    weights = (wg_b.reshape(w_gate.shape), wu_b.reshape(w_up.shape), wd_b.reshape(w_down.shape))
    return y, tail, u_s, weights, (bias_p, bias_s, sink_s)


def _lookup_bias(bucket, table_ref, head):
    acc = jnp.zeros(bucket.shape, F32)
    for b in range(N_BUCKETS):
        acc = jnp.where(bucket == b, table_ref[b, head], acc)
    return acc


def _bias_tables(bucket_p_ref, bucket_s_ref, table_ref, sinks_ref, bias_p_ref, bias_s_ref, sink_s_ref):
    bucket_row0 = bucket_p_ref[...]
    for head in range(N_HEADS):
        row0 = _lookup_bias(bucket_row0, table_ref, head)
        rows = jnp.concatenate([row0] * (WINDOW // SUBLANES), axis=0)
        bias_p_ref[head] = pltpu.roll(rows, 0, 1, stride=1, stride_axis=0)
    bucket_s = bucket_s_ref[...]
    grp = lax.broadcasted_iota(jnp.int32, bucket_s.shape, 0) % GROUP
    grp_l = lax.broadcasted_iota(jnp.int32, (bucket_s.shape[0], KV_LANES), 0) % GROUP
    for kv in range(N_KV_HEADS):
        bias = jnp.zeros(bucket_s.shape, F32)
        sink = jnp.zeros((bucket_s.shape[0], KV_LANES), F32)
        for gi in range(GROUP):
            head = kv * GROUP + gi
            bias = jnp.where(grp == gi, _lookup_bias(bucket_s, table_ref, head), bias)
            sink = jnp.where(grp_l == gi, sinks_ref[head], sink)
        bias_s_ref[kv] = bias
        sink_s_ref[kv] = sink


def _attn_prompt_kernel(xn_ref, xc_ref, xs_ref, os_ref, g_ref, wqkv_ref, wo_ref, bias_ref, sinks_ref,
                        o_ref, klast_ref, vlast_ref, qbuf, kbuf, vbuf, kcar, vcar, obuf):
    step = pl.program_id(0)
    rows = xn_ref.shape[0]
    n_blk = rows // WINDOW
    n_chunks = D_MODEL // MXU_WIDTH
    blocks_per_chunk = n_blk // n_chunks

    def norm_next():
        return _rms(xn_ref[...], g_ref[...]).astype(BF16)

    def project_chunk(h, slot, c):
        c0 = c * MXU_WIDTH
        part = jnp.dot(h, wqkv_ref[:, c0:c0 + MXU_WIDTH], preferred_element_type=F32)
        qbuf[slot, :, c0:c0 + MXU_WIDTH] = (part * ATTN_SCALE).astype(BF16)
        if c < n_chunks - 1:
            return
        kv_part = jnp.dot(h, wqkv_ref[:, D_MODEL:], preferred_element_type=F32)
        k = kv_part[:, :KV_LANES]
        v = kv_part[:, KV_LANES:]
        klast_ref[...] = k[rows - WINDOW:]
        vlast_ref[...] = v[rows - WINDOW:]
        lane = lax.broadcasted_iota(jnp.int32, (rows, KV_LANES), 1)
        for kv, half in enumerate((lane < HEAD_DIM, lane >= HEAD_DIM)):
            k_half = jnp.where(half, k, 0.0)
            v_half = jnp.where(half, v, 0.0)
            kbuf[slot, kv] = k_half.astype(BF16)
            vbuf[slot, kv] = v_half.astype(BF16)
            kcar[slot, kv] = k_half[rows - WINDOW:]
            vcar[slot, kv] = v_half[rows - WINDOW:]

    def first(slot):
        o_ref[...] = xs_ref[...] + jnp.dot(os_ref[...].astype(BF16), wo_ref[...], preferred_element_type=F32)
        kcar[...] = jnp.zeros(kcar.shape, F32)
        vcar[...] = jnp.zeros(vcar.shape, F32)
        h = norm_next()
        for c in range(n_chunks):
            project_chunk(h, slot, c)

    def body(cur, nxt):
        k_prev = [jnp.where(step > 1, kcar[nxt, kv], 0.0).astype(BF16) for kv in range(N_KV_HEADS)]
        v_prev = [jnp.where(step > 1, vcar[nxt, kv], 0.0).astype(BF16) for kv in range(N_KV_HEADS)]
        h = norm_next()

        qi = lax.broadcasted_iota(jnp.int32, (WINDOW, KEY_TILE), 0)
        kj = lax.broadcasted_iota(jnp.int32, (WINDOW, KEY_TILE), 1)
        band = (kj >= qi) & (kj <= qi + WINDOW)
        lane_q = lax.broadcasted_iota(jnp.int32, (WINDOW, KV_LANES), 1)
        lane_k = lax.broadcasted_iota(jnp.int32, (KEY_TILE, KV_LANES), 1)
        ones_half = [(lane_k < HEAD_DIM).astype(F32).astype(BF16), (lane_k >= HEAD_DIM).astype(F32).astype(BF16)]

        def block_scores(blk):
            r0 = blk * WINDOW
            qs = jnp.concatenate(
                [qbuf[cur, r0:r0 + WINDOW, gi * KV_LANES:(gi + 1) * KV_LANES] for gi in range(GROUP)], axis=0)
            if blk == 0:
                keys = [jnp.concatenate([k_prev[kv], kbuf[cur, kv, 0:WINDOW, :]], axis=0) for kv in range(N_KV_HEADS)]
            else:
                keys = [kbuf[cur, kv, r0 - WINDOW:r0 + WINDOW, :] for kv in range(N_KV_HEADS)]
            return [lax.dot_general(qs, keys[kv], (((1,), (1,)), ((), ())), preferred_element_type=F32)
                    for kv in range(N_KV_HEADS)]

        scores = block_scores(0)
        for blk in range(n_blk):
            r0 = blk * WINDOW
            if blk == 0:
                mask = band & (kj >= jnp.where(step > 1, 0, WINDOW))
                vals = [jnp.concatenate([v_prev[kv], vbuf[cur, kv, 0:WINDOW, :]], axis=0) for kv in range(N_KV_HEADS)]
            else:
                mask = band
                vals = [vbuf[cur, kv, r0 - WINDOW:r0 + WINDOW, :] for kv in range(N_KV_HEADS)]
            next_scores = block_scores(blk + 1) if blk + 1 < n_blk else None
            if blk % blocks_per_chunk == 0:
                project_chunk(h, nxt, blk // blocks_per_chunk)
            probs, maxes = [], []
            for kv in range(N_KV_HEADS):
                s_all = scores[kv]
                p_kv, m_kv = [], []
                for gi in range(GROUP):
                    head = kv * GROUP + gi
                    s = s_all[gi * WINDOW:(gi + 1) * WINDOW]
                    s = jnp.where(mask, s + bias_ref[head], NEG_INF)
                    m = jnp.maximum(jnp.max(s, axis=-1, keepdims=True), sinks_ref[head])
                    p_kv.append(jnp.exp(s - m).astype(BF16))
                    m_kv.append(m)
                probs.append(jnp.concatenate(p_kv, axis=0))
                maxes.append(m_kv)
            scores = next_scores
            acc = None
            for kv in range(N_KV_HEADS):
                rhs = jnp.concatenate([vals[kv], ones_half[kv]], axis=1)
                part = jnp.dot(probs[kv], rhs, preferred_element_type=F32)
                acc = part if acc is None else acc + part
            outs = []
            for gi in range(GROUP):
                a = acc[gi * WINDOW:(gi + 1) * WINDOW]
                sink_term = jnp.where(lane_q < HEAD_DIM,
                                      jnp.exp(sinks_ref[gi] - maxes[0][gi]),
                                      jnp.exp(sinks_ref[GROUP + gi] - maxes[1][gi]))
                outs.append(a[:, :KV_LANES] / (a[:, KV_LANES:] + sink_term))
            obuf[r0:r0 + WINDOW, :] = jnp.concatenate(outs, axis=1).astype(BF16)

        o_ref[...] = xc_ref[...] + jnp.dot(obuf[...], wo_ref[...], preferred_element_type=F32)

    _two_stage(first, body)


def _attn_prompt(x, o_sample, g, wqkv, wo, bias, sinks, n):
    tiles, nxt, cur = _pipeline_specs(n, ATTN_TILE)
    assert x.shape[0] == n + ATTN_TILE and o_sample.shape == (ATTN_TILE, D_MODEL)
    out = pl.BlockSpec((ATTN_TILE, D_MODEL), lambda i: (jnp.where(i == 0, tiles, i - 1), 0))
    sample = pl.BlockSpec((ATTN_TILE, D_MODEL), lambda i: (tiles, 0), pipeline_mode=pl.Buffered(1))
    last = pl.BlockSpec((WINDOW, KV_LANES), lambda i: (0, 0))
    return pl.pallas_call(
        _attn_prompt_kernel,
        grid=(tiles + 1,),
        in_specs=[nxt, cur, sample, _resident((ATTN_TILE, D_MODEL)), _resident((1, D_MODEL)),
                  _resident((D_MODEL, QKV_COLS)),
                  _resident((D_MODEL, D_MODEL)), _resident((N_HEADS, WINDOW, KEY_TILE)), _smem()],
        out_specs=[out, last, last],
        out_shape=[jax.ShapeDtypeStruct(x.shape, F32),
                   jax.ShapeDtypeStruct((WINDOW, KV_LANES), F32),
                   jax.ShapeDtypeStruct((WINDOW, KV_LANES), F32)],
        scratch_shapes=[pltpu.VMEM((2, ATTN_TILE, D_MODEL), BF16),
                        pltpu.VMEM((2, N_KV_HEADS, ATTN_TILE, KV_LANES), BF16),
                        pltpu.VMEM((2, N_KV_HEADS, ATTN_TILE, KV_LANES), BF16),
                        pltpu.VMEM((2, N_KV_HEADS, WINDOW, KV_LANES), F32),
                        pltpu.VMEM((2, N_KV_HEADS, WINDOW, KV_LANES), F32),
                        pltpu.VMEM((ATTN_TILE, D_MODEL), BF16)],
        compiler_params=_params(),
        name="attn_prompt",
    )(x, x, x, o_sample, g, wqkv, wo, bias, sinks)


def _attn_sample_kernel(q_ref, kt_ref, vt_ref, kvn_ref, bias_ref, sink_ref, o_ref, kout_ref, vout_ref, *, steps):
    q_rows = GROUP * steps
    qi = lax.broadcasted_iota(jnp.int32, (q_rows, KEY_TILE), 0) // GROUP
    kj = lax.broadcasted_iota(jnp.int32, (q_rows, KEY_TILE), 1)
    new_at = KEY_TILE - steps
    mask = ((kj < WINDOW) & (kj >= qi)) | ((kj >= new_at) & (kj - new_at <= qi))
    row = lax.broadcasted_iota(jnp.int32, (KV_LANES, KEY_TILE), 0)
    kv_row = (row < HEAD_DIM, row >= HEAD_DIM)
    lane_q = lax.broadcasted_iota(jnp.int32, (q_rows, KV_LANES), 1)
    lane_c = lax.broadcasted_iota(jnp.int32, (KV_LANES, WINDOW), 1)
    new_k = kvn_ref[0:KV_LANES, :]
    new_v = kvn_ref[KV_LANES:, :]

    def new_cols(new_t, b):
        return pltpu.roll(new_t, (WINDOW - steps - b * steps) % WINDOW, 1)

    def shifted_cache(cache_t, new_last):
        return jnp.where(lane_c < WINDOW - steps, pltpu.roll(cache_t, WINDOW - steps, 1), new_last)

    batches = range(kt_ref.shape[0])
    q_all = (q_ref[...] * ATTN_SCALE).reshape(q_ref.shape[0], GROUP, KV_LANES)
    scores = []
    for b in batches:
        kt = kt_ref[b]
        new_last = new_cols(new_k, b)
        kout_ref[b] = shifted_cache(kt, new_last)
        keys = jnp.concatenate([kt, new_last], axis=1)
        q = q_all[b * steps:(b + 1) * steps].reshape(q_rows, KV_LANES).astype(BF16)
        scores.append([jnp.dot(q, jnp.where(kv_row[kv], keys, 0.0).astype(BF16), preferred_element_type=F32)
                       for kv in range(N_KV_HEADS)])
    probs, sink_terms = [], []
    for b in batches:
        p_b, t_b = [], []
        for kv in range(N_KV_HEADS):
            s = jnp.where(mask, scores[b][kv] + bias_ref[kv], NEG_INF)
            sink = sink_ref[kv][:, 0:1]
            m = jnp.maximum(jnp.max(s, axis=-1, keepdims=True), sink)
            p_b.append(jnp.exp(s - m).astype(BF16))
            t_b.append(jnp.exp(sink - m))
        probs.append(p_b)
        sink_terms.append(t_b)
    outs = []
    for b in batches:
        vt = vt_ref[b]
        new_last = new_cols(new_v, b)
        vout_ref[b] = shifted_cache(vt, new_last)
        vals = jnp.concatenate([vt, new_last], axis=1)
        acc = None
        for kv in range(N_KV_HEADS):
            rhs = jnp.concatenate([jnp.where(kv_row[kv], vals, 0.0), jnp.where(kv_row[kv], 1.0, 0.0)], axis=0)
            part = lax.dot_general(probs[b][kv], rhs.astype(BF16), (((1,), (1,)), ((), ())),
                                   preferred_element_type=F32)
            acc = part if acc is None else acc + part
        denom = acc[:, KV_LANES:] + jnp.where(lane_q < HEAD_DIM, sink_terms[b][0], sink_terms[b][1])
        outs.append((acc[:, :KV_LANES] / denom).reshape(steps, GROUP, KV_LANES))
    o_ref[...] = jnp.concatenate(outs, axis=0).reshape(o_ref.shape)


def _attn_sample(q, kt, vt, kvn, bias, sink, steps):
    nb = kt.shape[0]
    q_rows = GROUP * steps
    bt = SAMPLE_BATCH_TILE
    assert bt * steps == WINDOW and nb % bt == 0

    def batch_spec(r):
        return pl.BlockSpec((bt, r, KV_LANES), lambda i: (i, 0, 0))

    rows_spec = pl.BlockSpec((bt * steps, D_MODEL), lambda i: (i, 0))
    return pl.pallas_call(
        functools.partial(_attn_sample_kernel, steps=steps),
        grid=(nb // bt,),
        in_specs=[rows_spec, batch_spec(KV_LANES), batch_spec(KV_LANES),
                  pl.BlockSpec((2 * KV_LANES, bt * steps), lambda i: (0, i)),
                  _resident((N_KV_HEADS, q_rows, KEY_TILE)), _resident((N_KV_HEADS, q_rows, KV_LANES))],
        out_specs=[rows_spec, batch_spec(KV_LANES), batch_spec(KV_LANES)],
        out_shape=[jax.ShapeDtypeStruct((nb * steps, D_MODEL), F32),
                   jax.ShapeDtypeStruct((nb, KV_LANES, WINDOW), F32),
                   jax.ShapeDtypeStruct((nb, KV_LANES, WINDOW), F32)],
        compiler_params=_params(),
        name="attn_sample",
    )(q, kt, vt, kvn, bias, sink)


def kernel(x_prompt, x_sample, state_conv, cache_k, cache_v, g_mix, g_ffn, g_final, w_conv_in, conv_w,
           w_conv_out, w_q, w_k, w_v, w_o, sinks, rel_table, w_gate, w_up, w_down):
    batch, seq, _ = x_prompt.shape
    dec_batch, dec_seq, _ = x_sample.shape
    assert batch == 1 and seq % TOKEN_TILE == 0 and seq % ATTN_TILE == 0
    assert ATTN_TILE % (WINDOW * (D_MODEL // MXU_WIDTH)) == 0
    assert dec_batch % SAMPLE_BATCH_TILE == 0 and dec_seq <= SUBLANES
    assert dec_batch * dec_seq == TOKEN_TILE, "the sample rows ride the token-wise kernels as one tile"
    assert g_mix.shape[0] == 2, "layer 0 is the conv mixer, layer 1 the attention mixer"

    win = w_conv_in[0]
    wout = w_conv_out[0]
    wq = w_q[0].reshape(D_MODEL, N_KV_HEADS, GROUP, HEAD_DIM).transpose(0, 2, 1, 3).reshape(D_MODEL, D_MODEL)
    wqkv = jnp.concatenate([wq, w_k[0], w_v[0]], axis=1).astype(BF16)
    wo = w_o[0].reshape(N_KV_HEADS, GROUP, HEAD_DIM, D_MODEL).transpose(1, 0, 2, 3).reshape(D_MODEL, D_MODEL)
    wo = wo.astype(BF16)
    wkvt = jnp.concatenate([w_k[0], w_v[0]], axis=1).T.astype(BF16)
    gm = g_mix.reshape(2, 1, D_MODEL)
    gf = g_ffn.reshape(2, 1, D_MODEL)
    gfin = g_final.reshape(1, D_MODEL)
    cw = conv_w[0]
    sink_vec = sinks[0]

    bucket_p = jnp.asarray(np.tile(_t5_bucket_np(WINDOW - np.arange(KEY_TILE)[None, :]), (SUBLANES, 1)))
    key_s = np.arange(KEY_TILE)[None, :]
    key_pos = np.where(key_s < WINDOW, key_s, key_s - (KEY_TILE - dec_seq) + WINDOW)
    dist_s = np.arange(dec_seq)[:, None] + WINDOW - key_pos
    bucket_s = jnp.asarray(np.repeat(_t5_bucket_np(dist_s), GROUP, axis=0))

    n_s = dec_batch * dec_seq
    prev = jnp.pad(state_conv[0], ((0, 0), (0, dec_seq - CONV_STATE), (0, 0))).reshape(n_s, D_MODEL)
    x, tail_p, u_s, (wg, wu, wd), (bias_p, bias_s, sink_s) = _conv(
        x_prompt.reshape(seq, D_MODEL), x_sample, prev, gm[0], win, cw, wout,
        w_gate, w_up, w_down, dec_seq, bucket_p, bucket_s, rel_table, sink_vec)
    state_conv_prompt = tail_p[SUBLANES - CONV_STATE:].reshape(1, batch, CONV_STATE, D_MODEL)
    state_conv_sample = u_s[None]
    x, q_s, kv_new_t = _ffn(x, gf, wg, wu, wd, gfin, 0, gm[1], wqkv, wkvt)

    kt = cache_k[0].transpose(0, 2, 3, 1).reshape(dec_batch, KV_LANES, WINDOW)
    vt = cache_v[0].transpose(0, 2, 3, 1).reshape(dec_batch, KV_LANES, WINDOW)
    o_s, kt_out, vt_out = _attn_sample(q_s, kt, vt, kv_new_t, bias_s, sink_s, dec_seq)
    x, k_last, v_last = _attn_prompt(x, o_s, gm[1], wqkv, wo, bias_p, sink_vec, seq)
    cache_k_prompt = k_last.reshape(1, batch, WINDOW, N_KV_HEADS, HEAD_DIM)
    cache_v_prompt = v_last.reshape(1, batch, WINDOW, N_KV_HEADS, HEAD_DIM)
    y_prompt, y_sample = _ffn_final(x, gf, wg, wu, wd, gfin, 1)
    y_prompt = y_prompt.reshape(batch, seq, D_MODEL)
    y_sample = y_sample.reshape(dec_batch, dec_seq, D_MODEL)
    cache_k_sample = kt_out.reshape(dec_batch, N_KV_HEADS, HEAD_DIM, WINDOW).transpose(0, 3, 1, 2)[None]
    cache_v_sample = vt_out.reshape(dec_batch, N_KV_HEADS, HEAD_DIM, WINDOW).transpose(0, 3, 1, 2)[None]

    return (y_prompt, y_sample, state_conv_prompt, state_conv_sample,
            cache_k_prompt, cache_k_sample, cache_v_prompt, cache_v_sample)
```

```python
import functools
import math

import numpy as np
import jax
import jax.numpy as jnp
from jax import lax
from jax.experimental import pallas as pl
from jax.experimental.pallas import tpu as pltpu

D_MODEL = 1024
D_FF = 2816
HEAD_DIM = 64
N_HEADS = 16
N_KV_HEADS = 2
GROUP = N_HEADS // N_KV_HEADS
WINDOW = 128
N_BUCKETS = 32
MAX_DISTANCE = 128
CONV_STATE = 2
EPS = 1e-5
NEG_INF = -1e30
ATTN_SCALE = 1.0 / math.sqrt(HEAD_DIM)

KV_LANES = N_KV_HEADS * HEAD_DIM
QKV_COLS = D_MODEL + 2 * KV_LANES
SUBLANES = 8
BF16_SUBLANES = 16
KEY_TILE = 2 * WINDOW

TOKEN_TILE = 512
ATTN_TILE = 512
MXU_WIDTH = 256
FFN_SPLIT = 2
CONV_SPLIT = 2
SAMPLE_BATCH_TILE = 32
VMEM_LIMIT_BYTES = 56 * 1024 * 1024

F32 = jnp.float32
BF16 = jnp.bfloat16


def _params():
    return pltpu.CompilerParams(dimension_semantics=("arbitrary",), vmem_limit_bytes=VMEM_LIMIT_BYTES)


def _resident(shape):
    zeros = (0,) * len(shape)
    return pl.BlockSpec(shape, lambda *_: zeros, pipeline_mode=pl.Buffered(1))


def _layer(shape, layer):
    index = (layer,) + (0,) * len(shape)
    return pl.BlockSpec((None,) + tuple(shape), lambda *_: index, pipeline_mode=pl.Buffered(1))


def _smem():
    return pl.BlockSpec(memory_space=pltpu.SMEM)


def _rms(x, g):
    return x * lax.rsqrt(jnp.mean(x * x, axis=-1, keepdims=True) + EPS) * g


def _pipeline_specs(n, rows):
    tiles = n // rows
    nxt = pl.BlockSpec((rows, D_MODEL), lambda i: (jnp.minimum(i, tiles - 1), 0))
    cur = pl.BlockSpec((rows, D_MODEL), lambda i: (jnp.maximum(i - 1, 0), 0))
    return tiles, nxt, cur


def _two_stage(first, body):
    step = pl.program_id(0)

    @pl.when(step == 0)
    def _():
        first(0)

    @pl.when(step % 2 == 1)
    def _():
        body(0, 1)

    @pl.when((step % 2 == 0) & (step > 0))
    def _():
        body(1, 0)


def _t5_bucket_np(dist):
    n = np.maximum(dist, 0)
    max_exact = N_BUCKETS // 2
    nf = np.maximum(n, 1).astype(np.float32)
    large = max_exact + (np.log(nf / np.float32(max_exact)) / np.float32(math.log(MAX_DISTANCE / max_exact))
                         * np.float32(N_BUCKETS - max_exact)).astype(np.int32)
    large = np.minimum(large, N_BUCKETS - 1)
    return np.where(n < max_exact, n, large).astype(np.int32)


def _ffn_kernel(x_ref, g_ref, wg_ref, wu_ref, wd_ref, gf_ref, *o_refs, final_norm):
    rows = x_ref.shape[0] // FFN_SPLIT
    halves = [pl.ds(i * rows, rows) for i in range(FFN_SPLIT)]
    acts = []
    for r in halves:
        h = _rms(x_ref[r, :], g_ref[...]).astype(BF16)
        gate = jnp.dot(h, wg_ref[...], preferred_element_type=F32)
        up = jnp.dot(h, wu_ref[...], preferred_element_type=F32)
        acts.append((gate * jax.nn.sigmoid(gate) * up).astype(BF16))
    for r, act in zip(halves, acts):
        y = x_ref[r, :] + jnp.dot(act, wd_ref[...], preferred_element_type=F32)
        if final_norm:
            y = _rms(y, gf_ref[...])
        for o_ref in o_refs:
            o_ref[r, :] = y


def _ffn_weight_specs(layer):
    return [_layer((1, D_MODEL), layer), _layer((D_MODEL, D_FF), layer), _layer((D_MODEL, D_FF), layer),
            _layer((D_FF, D_MODEL), layer), _resident((1, D_MODEL))]


def _ffn_proj_kernel(x_ref, g_ref, wg_ref, wu_ref, wd_ref, gf_ref, gq_ref, wq_ref, wkvt_ref,
                     o_ref, q_ref, kvt_ref):
    _ffn_kernel(x_ref, g_ref, wg_ref, wu_ref, wd_ref, gf_ref, o_ref, final_norm=False)

    @pl.when(pl.program_id(0) == pl.num_programs(0) - 1)
    def _():
        h = _rms(o_ref[...], gq_ref[...]).astype(BF16)
        q_ref[...] = jnp.dot(h, wq_ref[...], preferred_element_type=F32)
        kvt_ref[...] = lax.dot_general(wkvt_ref[...], h, (((1,), (1,)), ((), ())), preferred_element_type=F32)


def _ffn(x, g, wg, wu, wd, g_final, layer, g_q, wqkv, wkvt):
    n = x.shape[0]
    tile = pl.BlockSpec((TOKEN_TILE, D_MODEL), lambda i: (i, 0))

    def once(shape):
        return pl.BlockSpec(shape, lambda i: (0, 0))

    return pl.pallas_call(
        _ffn_proj_kernel,
        grid=(n // TOKEN_TILE,),
        in_specs=[tile] + _ffn_weight_specs(layer) + [_resident((1, D_MODEL)), _resident((D_MODEL, D_MODEL)),
                                                      _resident((2 * KV_LANES, D_MODEL))],
        out_specs=[tile, once((TOKEN_TILE, D_MODEL)), once((2 * KV_LANES, TOKEN_TILE))],
        out_shape=[jax.ShapeDtypeStruct((n, D_MODEL), F32),
                   jax.ShapeDtypeStruct((TOKEN_TILE, D_MODEL), F32),
                   jax.ShapeDtypeStruct((2 * KV_LANES, TOKEN_TILE), F32)],
        compiler_params=_params(),
        name="ffn",
    )(x, g, wg, wu, wd, g_final, g_q, wqkv, wkvt)


def _ffn_final(x, g, wg, wu, wd, g_final, layer):
    n = x.shape[0]
    tiles = n // TOKEN_TILE - 1
    shape = (TOKEN_TILE, D_MODEL)
    y_prompt, y_sample = pl.pallas_call(
        functools.partial(_ffn_kernel, final_norm=True),
        grid=(tiles + 1,),
        in_specs=[pl.BlockSpec(shape, lambda i: (jnp.where(i == 0, tiles, i - 1), 0))] + _ffn_weight_specs(layer),
        out_specs=[pl.BlockSpec(shape, lambda i: (jnp.maximum(i - 1, 0), 0)),
                   pl.BlockSpec(shape, lambda i: (jnp.minimum(i, 1), 0))],
        out_shape=[jax.ShapeDtypeStruct((tiles * TOKEN_TILE, D_MODEL), F32),
                   jax.ShapeDtypeStruct((2 * TOKEN_TILE, D_MODEL), F32)],
        compiler_params=_params(),
        name="ffn_final",
    )(x, g, wg, wu, wd, g_final)
    return y_prompt, y_sample[:TOKEN_TILE]


def _conv_gates(h, win_ref):
    cx = jnp.dot(h, win_ref[:, D_MODEL:], preferred_element_type=F32)
    u = cx[:, :D_MODEL] * cx[:, D_MODEL:]
    b = jnp.dot(h, win_ref[:, :D_MODEL], preferred_element_type=F32)
    return b, u


def _conv_kernel(x_ref, xs_ref, prev_ref, g_ref, win_ref, cw_ref, wout_ref, wg_ref, wu_ref, wd_ref,
                 bucket_p_ref, bucket_s_ref, table_ref, sinks_ref,
                 o_ref, tail_out_ref, us_ref, wg_out_ref, wu_out_ref, wd_out_ref,
                 bias_p_ref, bias_s_ref, sink_s_ref, tail_ref, *, steps):
    step = pl.program_id(0)
    last = pl.num_programs(0) - 1
    cw = cw_ref[...]

    @pl.when(step < last)
    def _():
        wg_out_ref[...] = wg_ref[...].astype(BF16)
        wu_out_ref[...] = wu_ref[...].astype(BF16)
        wd_out_ref[...] = wd_ref[...].astype(BF16)
        tail = jnp.where(step > 0, tail_ref[...], 0.0)
        rows = x_ref.shape[0] // CONV_SPLIT
        groups = [pl.ds(i * rows, rows) for i in range(CONV_SPLIT)]
        gates = [_conv_gates(_rms(x_ref[r, :], g_ref[...]), win_ref) for r in groups]
        r8 = lax.broadcasted_iota(jnp.int32, (SUBLANES, D_MODEL), 0)
        for r, (b, u) in zip(groups, gates):
            u1 = pltpu.roll(u, 1, 0)
            u2 = pltpu.roll(u, 2, 0)
            u1 = jnp.concatenate([jnp.where(r8 < 1, pltpu.roll(tail, 1, 0), u1[:SUBLANES]), u1[SUBLANES:]], axis=0)
            u2 = jnp.concatenate([jnp.where(r8 < 2, pltpu.roll(tail, 2, 0), u2[:SUBLANES]), u2[SUBLANES:]], axis=0)
            v = cw[0:1] * u2 + cw[1:2] * u1 + cw[2:3] * u
            o_ref[r, :] = x_ref[r, :] + jnp.dot(b * v, wout_ref[...], preferred_element_type=F32)
            tail = u[rows - SUBLANES:]
        tail_ref[...] = tail
        tail_out_ref[...] = tail

    @pl.when(step == last)
    def _():
        x = xs_ref[...].reshape(o_ref.shape)
        b, u = _conv_gates(_rms(x, g_ref[...]), win_ref)
        u3 = u.reshape(xs_ref.shape)
        prev = prev_ref[...]
        u1 = jnp.concatenate([prev[:, CONV_STATE - 1:, :], u3[:, :steps - 1, :]], axis=1)
        u2 = jnp.concatenate([prev, u3[:, :steps - CONV_STATE, :]], axis=1)
        v = (cw[0:1] * u2 + cw[1:2] * u1 + cw[2:3] * u3).reshape(u.shape)
        o_ref[...] = x + jnp.dot(b * v, wout_ref[...], preferred_element_type=F32)
        us_ref[...] = u3[:, steps - CONV_STATE:, :]
        _bias_tables(bucket_p_ref, bucket_s_ref, table_ref, sinks_ref, bias_p_ref, bias_s_ref, sink_s_ref)


def _conv(x, xs, prev, g, win, cw, wout, w_gate, w_up, w_down, steps, bucket_p, bucket_s, table, sinks):
    n = x.shape[0]
    tiles = n // TOKEN_TILE
    batches = xs.shape[0]
    assert xs.shape == (batches, steps, D_MODEL) and batches * steps == TOKEN_TILE
    tile_in = pl.BlockSpec((TOKEN_TILE, D_MODEL), lambda i: (jnp.minimum(i, tiles - 1), 0))
    tile_out = pl.BlockSpec((TOKEN_TILE, D_MODEL), lambda i: (i, 0))

    def slabs(w):
        flat = w.reshape(-1, w.shape[-1])
        rows = flat.shape[0] // tiles
        assert rows * tiles == flat.shape[0] and rows % BF16_SUBLANES == 0
        return flat, pl.BlockSpec((rows, flat.shape[1]), lambda i: (jnp.minimum(i, tiles - 1), 0))

    (wg, wg_spec), (wu, wu_spec), (wd, wd_spec) = slabs(w_gate), slabs(w_up), slabs(w_down)
    q_rows = bucket_s.shape[0]
    table_shapes = [(N_HEADS, WINDOW, KEY_TILE), (N_KV_HEADS, q_rows, KEY_TILE), (N_KV_HEADS, q_rows, KV_LANES)]
    y, tail, u_s, wg_b, wu_b, wd_b, bias_p, bias_s, sink_s = pl.pallas_call(
        functools.partial(_conv_kernel, steps=steps),
        grid=(tiles + 1,),
        in_specs=[tile_in, _resident(xs.shape), _resident(prev.shape), _resident((1, D_MODEL)),
                  _resident((D_MODEL, 3 * D_MODEL)),
                  _resident((3, D_MODEL)), _resident((D_MODEL, D_MODEL)), wg_spec, wu_spec, wd_spec,
                  _resident((SUBLANES, KEY_TILE)), _resident((q_rows, KEY_TILE)), _smem(), _smem()],
        out_specs=[tile_out, pl.BlockSpec((SUBLANES, D_MODEL), lambda i: (0, 0)),
                   pl.BlockSpec((batches, CONV_STATE, D_MODEL), lambda i: (0, 0, 0)), wg_spec, wu_spec, wd_spec]
        + [pl.BlockSpec(shape, lambda i: (0, 0, 0)) for shape in table_shapes],
        out_shape=[jax.ShapeDtypeStruct((n + TOKEN_TILE, D_MODEL), F32),
                   jax.ShapeDtypeStruct((SUBLANES, D_MODEL), F32),
                   jax.ShapeDtypeStruct((batches, CONV_STATE, D_MODEL), F32),
                   jax.ShapeDtypeStruct(wg.shape, BF16),
                   jax.ShapeDtypeStruct(wu.shape, BF16),
                   jax.ShapeDtypeStruct(wd.shape, BF16)]
        + [jax.ShapeDtypeStruct(shape, F32) for shape in table_shapes],
        scratch_shapes=[pltpu.VMEM((SUBLANES, D_MODEL), F32)],
        compiler_params=_params(),
        name="conv",
    )(x, xs, prev, g, win, cw, wout, wg, wu, wd, bucket_p, bucket_s, table, sinks)
    weights = (wg_b.reshape(w_gate.shape), wu_b.reshape(w_up.shape), wd_b.reshape(w_down.shape))
    return y, tail, u_s, weights, (bias_p, bias_s, sink_s)


def _lookup_bias(bucket, table_ref, head):
    acc = jnp.zeros(bucket.shape, F32)
    for b in range(N_BUCKETS):
        acc = jnp.where(bucket == b, table_ref[b, head], acc)
    return acc


def _bias_tables(bucket_p_ref, bucket_s_ref, table_ref, sinks_ref, bias_p_ref, bias_s_ref, sink_s_ref):
    bucket_row0 = bucket_p_ref[...]
    for head in range(N_HEADS):
        row0 = _lookup_bias(bucket_row0, table_ref, head)
        rows = jnp.concatenate([row0] * (WINDOW // SUBLANES), axis=0)
        bias_p_ref[head] = pltpu.roll(rows, 0, 1, stride=1, stride_axis=0)
    bucket_s = bucket_s_ref[...]
    grp = lax.broadcasted_iota(jnp.int32, bucket_s.shape, 0) % GROUP
    grp_l = lax.broadcasted_iota(jnp.int32, (bucket_s.shape[0], KV_LANES), 0) % GROUP
    for kv in range(N_KV_HEADS):
        bias = jnp.zeros(bucket_s.shape, F32)
        sink = jnp.zeros((bucket_s.shape[0], KV_LANES), F32)
        for gi in range(GROUP):
            head = kv * GROUP + gi
            bias = jnp.where(grp == gi, _lookup_bias(bucket_s, table_ref, head), bias)
            sink = jnp.where(grp_l == gi, sinks_ref[head], sink)
        bias_s_ref[kv] = bias
        sink_s_ref[kv] = sink


def _attn_prompt_kernel(xn_ref, xc_ref, xs_ref, os_ref, g_ref, wqkv_ref, wo_ref, bias_ref, sinks_ref,
                        o_ref, klast_ref, vlast_ref, qbuf, kbuf, vbuf, kcar, vcar, obuf):
    step = pl.program_id(0)
    rows = xn_ref.shape[0]
    n_blk = rows // WINDOW
    n_chunks = D_MODEL // MXU_WIDTH
    blocks_per_chunk = n_blk // n_chunks

    def norm_next():
        return _rms(xn_ref[...], g_ref[...]).astype(BF16)

    def project_chunk(h, slot, c):
        c0 = c * MXU_WIDTH
        part = jnp.dot(h, wqkv_ref[:, c0:c0 + MXU_WIDTH], preferred_element_type=F32)
        qbuf[slot, :, c0:c0 + MXU_WIDTH] = (part * ATTN_SCALE).astype(BF16)
        if c < n_chunks - 1:
            return
        kv_part = jnp.dot(h, wqkv_ref[:, D_MODEL:], preferred_element_type=F32)
        k = kv_part[:, :KV_LANES]
        v = kv_part[:, KV_LANES:]
        klast_ref[...] = k[rows - WINDOW:]
        vlast_ref[...] = v[rows - WINDOW:]
        lane = lax.broadcasted_iota(jnp.int32, (rows, KV_LANES), 1)
        for kv, half in enumerate((lane < HEAD_DIM, lane >= HEAD_DIM)):
            k_half = jnp.where(half, k, 0.0)
            v_half = jnp.where(half, v, 0.0)
            kbuf[slot, kv] = k_half.astype(BF16)
            vbuf[slot, kv] = v_half.astype(BF16)
            kcar[slot, kv] = k_half[rows - WINDOW:]
            vcar[slot, kv] = v_half[rows - WINDOW:]

    def first(slot):
        o_ref[...] = xs_ref[...] + jnp.dot(os_ref[...].astype(BF16), wo_ref[...], preferred_element_type=F32)
        kcar[...] = jnp.zeros(kcar.shape, F32)
        vcar[...] = jnp.zeros(vcar.shape, F32)
        h = norm_next()
        for c in range(n_chunks):
            project_chunk(h, slot, c)

    def body(cur, nxt):
        k_prev = [jnp.where(step > 1, kcar[nxt, kv], 0.0).astype(BF16) for kv in range(N_KV_HEADS)]
        v_prev = [jnp.where(step > 1, vcar[nxt, kv], 0.0).astype(BF16) for kv in range(N_KV_HEADS)]
        h = norm_next()

        qi = lax.broadcasted_iota(jnp.int32, (WINDOW, KEY_TILE), 0)
        kj = lax.broadcasted_iota(jnp.int32, (WINDOW, KEY_TILE), 1)
        band = (kj >= qi) & (kj <= qi + WINDOW)
        lane_q = lax.broadcasted_iota(jnp.int32, (WINDOW, KV_LANES), 1)
        lane_k = lax.broadcasted_iota(jnp.int32, (KEY_TILE, KV_LANES), 1)
        ones_half = [(lane_k < HEAD_DIM).astype(F32).astype(BF16), (lane_k >= HEAD_DIM).astype(F32).astype(BF16)]

        def block_scores(blk):
            r0 = blk * WINDOW
            qs = jnp.concatenate(
                [qbuf[cur, r0:r0 + WINDOW, gi * KV_LANES:(gi + 1) * KV_LANES] for gi in range(GROUP)], axis=0)
            if blk == 0:
                keys = [jnp.concatenate([k_prev[kv], kbuf[cur, kv, 0:WINDOW, :]], axis=0) for kv in range(N_KV_HEADS)]
            else:
                keys = [kbuf[cur, kv, r0 - WINDOW:r0 + WINDOW, :] for kv in range(N_KV_HEADS)]
            return [lax.dot_general(qs, keys[kv], (((1,), (1,)), ((), ())), preferred_element_type=F32)
                    for kv in range(N_KV_HEADS)]

        scores = block_scores(0)
        for blk in range(n_blk):
            r0 = blk * WINDOW
            if blk == 0:
                mask = band & (kj >= jnp.where(step > 1, 0, WINDOW))
                vals = [jnp.concatenate([v_prev[kv], vbuf[cur, kv, 0:WINDOW, :]], axis=0) for kv in range(N_KV_HEADS)]
            else:
                mask = band
                vals = [vbuf[cur, kv, r0 - WINDOW:r0 + WINDOW, :] for kv in range(N_KV_HEADS)]
            next_scores = block_scores(blk + 1) if blk + 1 < n_blk else None
            if blk % blocks_per_chunk == 0:
                project_chunk(h, nxt, blk // blocks_per_chunk)
            probs, maxes = [], []
            for kv in range(N_KV_HEADS):
                s_all = scores[kv]
                p_kv, m_kv = [], []
                for gi in range(GROUP):
                    head = kv * GROUP + gi
                    s = s_all[gi * WINDOW:(gi + 1) * WINDOW]
                    s = jnp.where(mask, s + bias_ref[head], NEG_INF)
                    m = jnp.maximum(jnp.max(s, axis=-1, keepdims=True), sinks_ref[head])
                    p_kv.append(jnp.exp(s - m).astype(BF16))
                    m_kv.append(m)
                probs.append(jnp.concatenate(p_kv, axis=0))
                maxes.append(m_kv)
            scores = next_scores
            acc = None
            for kv in range(N_KV_HEADS):
                rhs = jnp.concatenate([vals[kv], ones_half[kv]], axis=1)
                part = jnp.dot(probs[kv], rhs, preferred_element_type=F32)
                acc = part if acc is None else acc + part
            outs = []
            for gi in range(GROUP):
                a = acc[gi * WINDOW:(gi + 1) * WINDOW]
                sink_term = jnp.where(lane_q < HEAD_DIM,
                                      jnp.exp(sinks_ref[gi] - maxes[0][gi]),
                                      jnp.exp(sinks_ref[GROUP + gi] - maxes[1][gi]))
                outs.append(a[:, :KV_LANES] / (a[:, KV_LANES:] + sink_term))
            obuf[r0:r0 + WINDOW, :] = jnp.concatenate(outs, axis=1).astype(BF16)

        o_ref[...] = xc_ref[...] + jnp.dot(obuf[...], wo_ref[...], preferred_element_type=F32)

    _two_stage(first, body)


def _attn_prompt(x, o_sample, g, wqkv, wo, bias, sinks, n):
    tiles, nxt, cur = _pipeline_specs(n, ATTN_TILE)
    assert x.shape[0] == n + ATTN_TILE and o_sample.shape == (ATTN_TILE, D_MODEL)
    out = pl.BlockSpec((ATTN_TILE, D_MODEL), lambda i: (jnp.where(i == 0, tiles, i - 1), 0))
    sample = pl.BlockSpec((ATTN_TILE, D_MODEL), lambda i: (tiles, 0), pipeline_mode=pl.Buffered(1))
    last = pl.BlockSpec((WINDOW, KV_LANES), lambda i: (0, 0))
    return pl.pallas_call(
        _attn_prompt_kernel,
        grid=(tiles + 1,),
        in_specs=[nxt, cur, sample, _resident((ATTN_TILE, D_MODEL)), _resident((1, D_MODEL)),
                  _resident((D_MODEL, QKV_COLS)),
                  _resident((D_MODEL, D_MODEL)), _resident((N_HEADS, WINDOW, KEY_TILE)), _smem()],
        out_specs=[out, last, last],
        out_shape=[jax.ShapeDtypeStruct(x.shape, F32),
                   jax.ShapeDtypeStruct((WINDOW, KV_LANES), F32),
                   jax.ShapeDtypeStruct((WINDOW, KV_LANES), F32)],
        scratch_shapes=[pltpu.VMEM((2, ATTN_TILE, D_MODEL), BF16),
                        pltpu.VMEM((2, N_KV_HEADS, ATTN_TILE, KV_LANES), BF16),
                        pltpu.VMEM((2, N_KV_HEADS, ATTN_TILE, KV_LANES), BF16),
                        pltpu.VMEM((2, N_KV_HEADS, WINDOW, KV_LANES), F32),
                        pltpu.VMEM((2, N_KV_HEADS, WINDOW, KV_LANES), F32),
                        pltpu.VMEM((ATTN_TILE, D_MODEL), BF16)],
        compiler_params=_params(),
        name="attn_prompt",
    )(x, x, x, o_sample, g, wqkv, wo, bias, sinks)


def _attn_sample_kernel(q_ref, kt_ref, vt_ref, kvn_ref, bias_ref, sink_ref, o_ref, kout_ref, vout_ref, *, steps):
    q_rows = GROUP * steps
    qi = lax.broadcasted_iota(jnp.int32, (q_rows, KEY_TILE), 0) // GROUP
    kj = lax.broadcasted_iota(jnp.int32, (q_rows, KEY_TILE), 1)
    new_at = KEY_TILE - steps
    mask = ((kj < WINDOW) & (kj >= qi)) | ((kj >= new_at) & (kj - new_at <= qi))
    row = lax.broadcasted_iota(jnp.int32, (KV_LANES, KEY_TILE), 0)
    kv_row = (row < HEAD_DIM, row >= HEAD_DIM)
    lane_q = lax.broadcasted_iota(jnp.int32, (q_rows, KV_LANES), 1)
    lane_c = lax.broadcasted_iota(jnp.int32, (KV_LANES, WINDOW), 1)
    new_k = kvn_ref[0:KV_LANES, :]
    new_v = kvn_ref[KV_LANES:, :]

    def new_cols(new_t, b):
        return pltpu.roll(new_t, (WINDOW - steps - b * steps) % WINDOW, 1)

    def shifted_cache(cache_t, new_last):
        return jnp.where(lane_c < WINDOW - steps, pltpu.roll(cache_t, WINDOW - steps, 1), new_last)

    batches = range(kt_ref.shape[0])
    q_all = (q_ref[...] * ATTN_SCALE).reshape(q_ref.shape[0], GROUP, KV_LANES)
    scores = []
    for b in batches:
        kt = kt_ref[b]
        new_last = new_cols(new_k, b)
        kout_ref[b] = shifted_cache(kt, new_last)
        keys = jnp.concatenate([kt, new_last], axis=1)
        q = q_all[b * steps:(b + 1) * steps].reshape(q_rows, KV_LANES).astype(BF16)
        scores.append([jnp.dot(q, jnp.where(kv_row[kv], keys, 0.0).astype(BF16), preferred_element_type=F32)
                       for kv in range(N_KV_HEADS)])
    probs, sink_terms = [], []
    for b in batches:
        p_b, t_b = [], []
        for kv in range(N_KV_HEADS):
            s = jnp.where(mask, scores[b][kv] + bias_ref[kv], NEG_INF)
            sink = sink_ref[kv][:, 0:1]
            m = jnp.maximum(jnp.max(s, axis=-1, keepdims=True), sink)
            p_b.append(jnp.exp(s - m).astype(BF16))
            t_b.append(jnp.exp(sink - m))
        probs.append(p_b)
        sink_terms.append(t_b)
    outs = []
    for b in batches:
        vt = vt_ref[b]
        new_last = new_cols(new_v, b)
        vout_ref[b] = shifted_cache(vt, new_last)
        vals = jnp.concatenate([vt, new_last], axis=1)
        acc = None
        for kv in range(N_KV_HEADS):
            rhs = jnp.concatenate([jnp.where(kv_row[kv], vals, 0.0), jnp.where(kv_row[kv], 1.0, 0.0)], axis=0)
            part = lax.dot_general(probs[b][kv], rhs.astype(BF16), (((1,), (1,)), ((), ())),
                                   preferred_element_type=F32)
            acc = part if acc is None else acc + part
        denom = acc[:, KV_LANES:] + jnp.where(lane_q < HEAD_DIM, sink_terms[b][0], sink_terms[b][1])
        outs.append((acc[:, :KV_LANES] / denom).reshape(steps, GROUP, KV_LANES))
    o_ref[...] = jnp.concatenate(outs, axis=0).reshape(o_ref.shape)


def _attn_sample(q, kt, vt, kvn, bias, sink, steps):
    nb = kt.shape[0]
    q_rows = GROUP * steps
    bt = SAMPLE_BATCH_TILE
    assert bt * steps == WINDOW and nb % bt == 0

    def batch_spec(r):
        return pl.BlockSpec((bt, r, KV_LANES), lambda i: (i, 0, 0))

    rows_spec = pl.BlockSpec((bt * steps, D_MODEL), lambda i: (i, 0))
    return pl.pallas_call(
        functools.partial(_attn_sample_kernel, steps=steps),
        grid=(nb // bt,),
        in_specs=[rows_spec, batch_spec(KV_LANES), batch_spec(KV_LANES),
                  pl.BlockSpec((2 * KV_LANES, bt * steps), lambda i: (0, i)),
                  _resident((N_KV_HEADS, q_rows, KEY_TILE)), _resident((N_KV_HEADS, q_rows, KV_LANES))],
        out_specs=[rows_spec, batch_spec(KV_LANES), batch_spec(KV_LANES)],
        out_shape=[jax.ShapeDtypeStruct((nb * steps, D_MODEL), F32),
                   jax.ShapeDtypeStruct((nb, KV_LANES, WINDOW), F32),
                   jax.ShapeDtypeStruct((nb, KV_LANES, WINDOW), F32)],
        compiler_params=_params(),
        name="attn_sample",
    )(q, kt, vt, kvn, bias, sink)


def kernel(x_prompt, x_sample, state_conv, cache_k, cache_v, g_mix, g_ffn, g_final, w_conv_in, conv_w,
           w_conv_out, w_q, w_k, w_v, w_o, sinks, rel_table, w_gate, w_up, w_down):
    batch, seq, _ = x_prompt.shape
    dec_batch, dec_seq, _ = x_sample.shape
    assert batch == 1 and seq % TOKEN_TILE == 0 and seq % ATTN_TILE == 0
    assert ATTN_TILE % (WINDOW * (D_MODEL // MXU_WIDTH)) == 0
    assert dec_batch % SAMPLE_BATCH_TILE == 0 and dec_seq <= SUBLANES
    assert dec_batch * dec_seq == TOKEN_TILE, "the sample rows ride the token-wise kernels as one tile"
    assert g_mix.shape[0] == 2, "layer 0 is the conv mixer, layer 1 the attention mixer"

    win = w_conv_in[0]
    wout = w_conv_out[0]
    wq = w_q[0].reshape(D_MODEL, N_KV_HEADS, GROUP, HEAD_DIM).transpose(0, 2, 1, 3).reshape(D_MODEL, D_MODEL)
    wqkv = jnp.concatenate([wq, w_k[0], w_v[0]], axis=1).astype(BF16)
    wo = w_o[0].reshape(N_KV_HEADS, GROUP, HEAD_DIM, D_MODEL).transpose(1, 0, 2, 3).reshape(D_MODEL, D_MODEL)
    wo = wo.astype(BF16)
    wkvt = jnp.concatenate([w_k[0], w_v[0]], axis=1).T.astype(BF16)
    gm = g_mix.reshape(2, 1, D_MODEL)
    gf = g_ffn.reshape(2, 1, D_MODEL)
    gfin = g_final.reshape(1, D_MODEL)
    cw = conv_w[0]
    sink_vec = sinks[0]

    bucket_p = jnp.asarray(np.tile(_t5_bucket_np(WINDOW - np.arange(KEY_TILE)[None, :]), (SUBLANES, 1)))
    key_s = np.arange(KEY_TILE)[None, :]
    key_pos = np.where(key_s < WINDOW, key_s, key_s - (KEY_TILE - dec_seq) + WINDOW)
    dist_s = np.arange(dec_seq)[:, None] + WINDOW - key_pos
    bucket_s = jnp.asarray(np.repeat(_t5_bucket_np(dist_s), GROUP, axis=0))

    x, tail_p, u_s, (wg, wu, wd), (bias_p, bias_s, sink_s) = _conv(
        x_prompt.reshape(seq, D_MODEL), x_sample, state_conv[0], gm[0], win, cw, wout,
        w_gate, w_up, w_down, dec_seq, bucket_p, bucket_s, rel_table, sink_vec)
    state_conv_prompt = tail_p[SUBLANES - CONV_STATE:].reshape(1, batch, CONV_STATE, D_MODEL)
    state_conv_sample = u_s[None]
    x, q_s, kv_new_t = _ffn(x, gf, wg, wu, wd, gfin, 0, gm[1], wqkv, wkvt)

    kt = cache_k[0].transpose(0, 2, 3, 1).reshape(dec_batch, KV_LANES, WINDOW)
    vt = cache_v[0].transpose(0, 2, 3, 1).reshape(dec_batch, KV_LANES, WINDOW)
    o_s, kt_out, vt_out = _attn_sample(q_s, kt, vt, kv_new_t, bias_s, sink_s, dec_seq)
    x, k_last, v_last = _attn_prompt(x, o_s, gm[1], wqkv, wo, bias_p, sink_vec, seq)
    cache_k_prompt = k_last.reshape(1, batch, WINDOW, N_KV_HEADS, HEAD_DIM)
    cache_v_prompt = v_last.reshape(1, batch, WINDOW, N_KV_HEADS, HEAD_DIM)
    y_prompt, y_sample = _ffn_final(x, gf, wg, wu, wd, gfin, 1)
    y_prompt = y_prompt.reshape(batch, seq, D_MODEL)
    y_sample = y_sample.reshape(dec_batch, dec_seq, D_MODEL)
    cache_k_sample = kt_out.reshape(dec_batch, N_KV_HEADS, HEAD_DIM, WINDOW).transpose(0, 3, 1, 2)[None]
    cache_v_sample = vt_out.reshape(dec_batch, N_KV_HEADS, HEAD_DIM, WINDOW).transpose(0, 3, 1, 2)[None]

    return (y_prompt, y_sample, state_conv_prompt, state_conv_sample,
            cache_k_prompt, cache_k_sample, cache_v_prompt, cache_v_sample)
```

```python
import functools
import math

import numpy as np
import jax
import jax.numpy as jnp
from jax import lax
from jax.experimental import pallas as pl
from jax.experimental.pallas import tpu as pltpu

D_MODEL = 1024
D_FF = 2816
HEAD_DIM = 64
N_HEADS = 16
N_KV_HEADS = 2
GROUP = N_HEADS // N_KV_HEADS
WINDOW = 128
N_BUCKETS = 32
MAX_DISTANCE = 128
CONV_STATE = 2
EPS = 1e-5
NEG_INF = -1e30
ATTN_SCALE = 1.0 / math.sqrt(HEAD_DIM)

KV_LANES = N_KV_HEADS * HEAD_DIM
QKV_COLS = D_MODEL + 2 * KV_LANES
SUBLANES = 8
BF16_SUBLANES = 16
KEY_TILE = 2 * WINDOW

TOKEN_TILE = 512
ATTN_TILE = 512
MXU_WIDTH = 256
FFN_SPLIT = 2
CONV_SPLIT = 2
SAMPLE_BATCH_TILE = 32
VMEM_LIMIT_BYTES = 56 * 1024 * 1024

F32 = jnp.float32
BF16 = jnp.bfloat16


def _params():
    return pltpu.CompilerParams(dimension_semantics=("arbitrary",), vmem_limit_bytes=VMEM_LIMIT_BYTES)


def _resident(shape):
    zeros = (0,) * len(shape)
    return pl.BlockSpec(shape, lambda *_: zeros, pipeline_mode=pl.Buffered(1))


def _layer(shape, layer):
    index = (layer,) + (0,) * len(shape)
    return pl.BlockSpec((None,) + tuple(shape), lambda *_: index, pipeline_mode=pl.Buffered(1))


def _smem():
    return pl.BlockSpec(memory_space=pltpu.SMEM)


def _rms(x, g):
    return x * lax.rsqrt(jnp.mean(x * x, axis=-1, keepdims=True) + EPS) * g


def _pipeline_specs(n, rows):
    tiles = n // rows
    nxt = pl.BlockSpec((rows, D_MODEL), lambda i: (jnp.minimum(i, tiles - 1), 0))
    cur = pl.BlockSpec((rows, D_MODEL), lambda i: (jnp.maximum(i - 1, 0), 0))
    return tiles, nxt, cur


def _two_stage(first, body):
    step = pl.program_id(0)

    @pl.when(step == 0)
    def _():
        first(0)

    @pl.when(step % 2 == 1)
    def _():
        body(0, 1)

    @pl.when((step % 2 == 0) & (step > 0))
    def _():
        body(1, 0)


def _t5_bucket_np(dist):
    n = np.maximum(dist, 0)
    max_exact = N_BUCKETS // 2
    nf = np.maximum(n, 1).astype(np.float32)
    large = max_exact + (np.log(nf / np.float32(max_exact)) / np.float32(math.log(MAX_DISTANCE / max_exact))
                         * np.float32(N_BUCKETS - max_exact)).astype(np.int32)
    large = np.minimum(large, N_BUCKETS - 1)
    return np.where(n < max_exact, n, large).astype(np.int32)


def _ffn_kernel(x_ref, g_ref, wg_ref, wu_ref, wd_ref, gf_ref, o_ref, *, final_norm):
    rows = x_ref.shape[0] // FFN_SPLIT
    halves = [pl.ds(i * rows, rows) for i in range(FFN_SPLIT)]
    acts = []
    for r in halves:
        h = _rms(x_ref[r, :], g_ref[...]).astype(BF16)
        gate = jnp.dot(h, wg_ref[...], preferred_element_type=F32)
        up = jnp.dot(h, wu_ref[...], preferred_element_type=F32)
        acts.append((gate * jax.nn.sigmoid(gate) * up).astype(BF16))
    for r, act in zip(halves, acts):
        y = x_ref[r, :] + jnp.dot(act, wd_ref[...], preferred_element_type=F32)
        if final_norm:
            y = _rms(y, gf_ref[...])
        o_ref[r, :] = y


def _ffn_weight_specs(layer):
    return [_layer((1, D_MODEL), layer), _layer((D_MODEL, D_FF), layer), _layer((D_MODEL, D_FF), layer),
            _layer((D_FF, D_MODEL), layer), _resident((1, D_MODEL))]


def _ffn_proj_kernel(x_ref, g_ref, wg_ref, wu_ref, wd_ref, gf_ref, gq_ref, wq_ref, wkvt_ref,
                     o_ref, q_ref, kvt_ref):
    _ffn_kernel(x_ref, g_ref, wg_ref, wu_ref, wd_ref, gf_ref, o_ref, final_norm=False)

    @pl.when(pl.program_id(0) == pl.num_programs(0) - 1)
    def _():
        h = _rms(o_ref[...], gq_ref[...]).astype(BF16)
        q_ref[...] = jnp.dot(h, wq_ref[...], preferred_element_type=F32)
        kvt_ref[...] = lax.dot_general(wkvt_ref[...], h, (((1,), (1,)), ((), ())), preferred_element_type=F32)


def _ffn(x, g, wg, wu, wd, g_final, layer, g_q, wqkv, wkvt):
    n = x.shape[0]
    tile = pl.BlockSpec((TOKEN_TILE, D_MODEL), lambda i: (i, 0))

    def once(shape):
        return pl.BlockSpec(shape, lambda i: (0, 0))

    return pl.pallas_call(
        _ffn_proj_kernel,
        grid=(n // TOKEN_TILE,),
        in_specs=[tile] + _ffn_weight_specs(layer) + [_resident((1, D_MODEL)), _resident((D_MODEL, D_MODEL)),
                                                      _resident((2 * KV_LANES, D_MODEL))],
        out_specs=[tile, once((TOKEN_TILE, D_MODEL)), once((2 * KV_LANES, TOKEN_TILE))],
        out_shape=[jax.ShapeDtypeStruct((n, D_MODEL), F32),
                   jax.ShapeDtypeStruct((TOKEN_TILE, D_MODEL), F32),
                   jax.ShapeDtypeStruct((2 * KV_LANES, TOKEN_TILE), F32)],
        compiler_params=_params(),
        name="ffn",
    )(x, g, wg, wu, wd, g_final, g_q, wqkv, wkvt)


def _ffn_final_kernel(x_ref, g_ref, wg_ref, wu_ref, wd_ref, gf_ref, o_ref, os_ref):
    _ffn_kernel(x_ref, g_ref, wg_ref, wu_ref, wd_ref, gf_ref, o_ref, final_norm=True)

    @pl.when(pl.program_id(0) == 0)
    def _():
        os_ref[...] = o_ref[...].reshape(os_ref.shape)


def _ffn_final(x, g, wg, wu, wd, g_final, layer, sample_shape):
    n = x.shape[0]
    tiles = n // TOKEN_TILE - 1
    shape = (TOKEN_TILE, D_MODEL)
    assert math.prod(sample_shape) == TOKEN_TILE * D_MODEL
    return pl.pallas_call(
        _ffn_final_kernel,
        grid=(tiles + 1,),
        in_specs=[pl.BlockSpec(shape, lambda i: (jnp.where(i == 0, tiles, i - 1), 0))] + _ffn_weight_specs(layer),
        out_specs=[pl.BlockSpec(shape, lambda i: (jnp.maximum(i - 1, 0), 0)),
                   pl.BlockSpec(sample_shape, lambda i: (0,) * len(sample_shape))],
        out_shape=[jax.ShapeDtypeStruct((tiles * TOKEN_TILE, D_MODEL), F32),
                   jax.ShapeDtypeStruct(sample_shape, F32)],
        compiler_params=_params(),
        name="ffn_final",
    )(x, g, wg, wu, wd, g_final)


def _conv_gates(h, win_ref):
    cx = jnp.dot(h, win_ref[:, D_MODEL:], preferred_element_type=F32)
    u = cx[:, :D_MODEL] * cx[:, D_MODEL:]
    b = jnp.dot(h, win_ref[:, :D_MODEL], preferred_element_type=F32)
    return b, u


def _conv_kernel(x_ref, xs_ref, prev_ref, g_ref, win_ref, cw_ref, wout_ref, wg_ref, wu_ref, wd_ref,
                 bucket_p_ref, bucket_s_ref, table_ref, sinks_ref,
                 o_ref, tail_out_ref, us_ref, wg_out_ref, wu_out_ref, wd_out_ref,
                 bias_p_ref, bias_s_ref, sink_s_ref, tail_ref, *, steps):
    step = pl.program_id(0)
    last = pl.num_programs(0) - 1
    cw = cw_ref[...]

    @pl.when(step < last)
    def _():
        wg_out_ref[...] = wg_ref[...].astype(BF16)
        wu_out_ref[...] = wu_ref[...].astype(BF16)
        wd_out_ref[...] = wd_ref[...].astype(BF16)
        tail = jnp.where(step > 0, tail_ref[...], 0.0)
        rows = x_ref.shape[0] // CONV_SPLIT
        groups = [pl.ds(i * rows, rows) for i in range(CONV_SPLIT)]
        gates = [_conv_gates(_rms(x_ref[r, :], g_ref[...]), win_ref) for r in groups]
        r8 = lax.broadcasted_iota(jnp.int32, (SUBLANES, D_MODEL), 0)
        for r, (b, u) in zip(groups, gates):
            u1 = pltpu.roll(u, 1, 0)
            u2 = pltpu.roll(u, 2, 0)
            u1 = jnp.concatenate([jnp.where(r8 < 1, pltpu.roll(tail, 1, 0), u1[:SUBLANES]), u1[SUBLANES:]], axis=0)
            u2 = jnp.concatenate([jnp.where(r8 < 2, pltpu.roll(tail, 2, 0), u2[:SUBLANES]), u2[SUBLANES:]], axis=0)
            v = cw[0:1] * u2 + cw[1:2] * u1 + cw[2:3] * u
            o_ref[r, :] = x_ref[r, :] + jnp.dot(b * v, wout_ref[...], preferred_element_type=F32)
            tail = u[rows - SUBLANES:]
        tail_ref[...] = tail
        tail_out_ref[...] = tail

    @pl.when(step == last)
    def _():
        x = xs_ref[...].reshape(o_ref.shape)
        b, u = _conv_gates(_rms(x, g_ref[...]), win_ref)
        u3 = u.reshape(xs_ref.shape)
        prev = prev_ref[...]
        u1 = jnp.concatenate([prev[:, CONV_STATE - 1:, :], u3[:, :steps - 1, :]], axis=1)
        u2 = jnp.concatenate([prev, u3[:, :steps - CONV_STATE, :]], axis=1)
        v = (cw[0:1] * u2 + cw[1:2] * u1 + cw[2:3] * u3).reshape(u.shape)
        o_ref[...] = x + jnp.dot(b * v, wout_ref[...], preferred_element_type=F32)
        us_ref[...] = u3[:, steps - CONV_STATE:, :]
        _bias_tables(bucket_p_ref, bucket_s_ref, table_ref, sinks_ref, bias_p_ref, bias_s_ref, sink_s_ref)


def _conv(x, xs, prev, g, win, cw, wout, w_gate, w_up, w_down, steps, bucket_p, bucket_s, table, sinks):
    n = x.shape[0]
    tiles = n // TOKEN_TILE
    batches = xs.shape[0]
    assert xs.shape == (batches, steps, D_MODEL) and batches * steps == TOKEN_TILE
    tile_in = pl.BlockSpec((TOKEN_TILE, D_MODEL), lambda i: (jnp.minimum(i, tiles - 1), 0))
    tile_out = pl.BlockSpec((TOKEN_TILE, D_MODEL), lambda i: (i, 0))

    def slabs(w):
        flat = w.reshape(-1, w.shape[-1])
        rows = flat.shape[0] // tiles
        assert rows * tiles == flat.shape[0] and rows % BF16_SUBLANES == 0
        return flat, pl.BlockSpec((rows, flat.shape[1]), lambda i: (jnp.minimum(i, tiles - 1), 0))

    (wg, wg_spec), (wu, wu_spec), (wd, wd_spec) = slabs(w_gate), slabs(w_up), slabs(w_down)
    q_rows = bucket_s.shape[0]
    table_shapes = [(N_HEADS, WINDOW, KEY_TILE), (N_KV_HEADS, q_rows, KEY_TILE), (N_KV_HEADS, q_rows, KV_LANES)]
    y, tail, u_s, wg_b, wu_b, wd_b, bias_p, bias_s, sink_s = pl.pallas_call(
        functools.partial(_conv_kernel, steps=steps),
        grid=(tiles + 1,),
        in_specs=[tile_in, _resident(xs.shape), _resident(prev.shape), _resident((1, D_MODEL)),
                  _resident((D_MODEL, 3 * D_MODEL)),
                  _resident((3, D_MODEL)), _resident((D_MODEL, D_MODEL)), wg_spec, wu_spec, wd_spec,
                  _resident((SUBLANES, KEY_TILE)), _resident((q_rows, KEY_TILE)), _smem(), _smem()],
        out_specs=[tile_out, pl.BlockSpec((SUBLANES, D_MODEL), lambda i: (0, 0)),
                   pl.BlockSpec((batches, CONV_STATE, D_MODEL), lambda i: (0, 0, 0)), wg_spec, wu_spec, wd_spec]
        + [pl.BlockSpec(shape, lambda i: (0, 0, 0)) for shape in table_shapes],
        out_shape=[jax.ShapeDtypeStruct((n + TOKEN_TILE, D_MODEL), F32),
                   jax.ShapeDtypeStruct((SUBLANES, D_MODEL), F32),
                   jax.ShapeDtypeStruct((batches, CONV_STATE, D_MODEL), F32),
                   jax.ShapeDtypeStruct(wg.shape, BF16),
                   jax.ShapeDtypeStruct(wu.shape, BF16),
                   jax.ShapeDtypeStruct(wd.shape, BF16)]
        + [jax.ShapeDtypeStruct(shape, F32) for shape in table_shapes],
        scratch_shapes=[pltpu.VMEM((SUBLANES, D_MODEL), F32)],
        compiler_params=_params(),
        name="conv",
    )(x, xs, prev, g, win, cw, wout, wg, wu, wd, bucket_p, bucket_s, table, sinks)
    weights = (wg_b.reshape(w_gate.shape), wu_b.reshape(w_up.shape), wd_b.reshape(w_down.shape))
    return y, tail, u_s, weights, (bias_p, bias_s, sink_s)


def _lookup_bias(bucket, table_ref, head):
    acc = jnp.zeros(bucket.shape, F32)
    for b in range(N_BUCKETS):
        acc = jnp.where(bucket == b, table_ref[b, head], acc)
    return acc


def _bias_tables(bucket_p_ref, bucket_s_ref, table_ref, sinks_ref, bias_p_ref, bias_s_ref, sink_s_ref):
    bucket_row0 = bucket_p_ref[...]
    for head in range(N_HEADS):
        row0 = _lookup_bias(bucket_row0, table_ref, head)
        rows = jnp.concatenate([row0] * (WINDOW // SUBLANES), axis=0)
        bias_p_ref[head] = pltpu.roll(rows, 0, 1, stride=1, stride_axis=0)
    bucket_s = bucket_s_ref[...]
    grp = lax.broadcasted_iota(jnp.int32, bucket_s.shape, 0) % GROUP
    grp_l = lax.broadcasted_iota(jnp.int32, (bucket_s.shape[0], KV_LANES), 0) % GROUP
    for kv in range(N_KV_HEADS):
        bias = jnp.zeros(bucket_s.shape, F32)
        sink = jnp.zeros((bucket_s.shape[0], KV_LANES), F32)
        for gi in range(GROUP):
            head = kv * GROUP + gi
            bias = jnp.where(grp == gi, _lookup_bias(bucket_s, table_ref, head), bias)
            sink = jnp.where(grp_l == gi, sinks_ref[head], sink)
        bias_s_ref[kv] = bias
        sink_s_ref[kv] = sink


def _attn_prompt_kernel(xn_ref, xc_ref, xs_ref, os_ref, g_ref, wqkv_ref, wo_ref, bias_ref, sinks_ref,
                        o_ref, klast_ref, vlast_ref, qbuf, kbuf, vbuf, kcar, vcar, obuf):
    step = pl.program_id(0)
    rows = xn_ref.shape[0]
    n_blk = rows // WINDOW
    n_chunks = D_MODEL // MXU_WIDTH
    blocks_per_chunk = n_blk // n_chunks

    def norm_next():
        return _rms(xn_ref[...], g_ref[...]).astype(BF16)

    def project_chunk(h, slot, c):
        c0 = c * MXU_WIDTH
        part = jnp.dot(h, wqkv_ref[:, c0:c0 + MXU_WIDTH], preferred_element_type=F32)
        qbuf[slot, :, c0:c0 + MXU_WIDTH] = (part * ATTN_SCALE).astype(BF16)
        if c < n_chunks - 1:
            return
        kv_part = jnp.dot(h, wqkv_ref[:, D_MODEL:], preferred_element_type=F32)
        k = kv_part[:, :KV_LANES]
        v = kv_part[:, KV_LANES:]
        klast_ref[...] = k[rows - WINDOW:]
        vlast_ref[...] = v[rows - WINDOW:]
        lane = lax.broadcasted_iota(jnp.int32, (rows, KV_LANES), 1)
        for kv, half in enumerate((lane < HEAD_DIM, lane >= HEAD_DIM)):
            k_half = jnp.where(half, k, 0.0)
            v_half = jnp.where(half, v, 0.0)
            kbuf[slot, kv] = k_half.astype(BF16)
            vbuf[slot, kv] = v_half.astype(BF16)
            kcar[slot, kv] = k_half[rows - WINDOW:]
            vcar[slot, kv] = v_half[rows - WINDOW:]

    def first(slot):
        o_ref[...] = xs_ref[...] + jnp.dot(os_ref[...].astype(BF16), wo_ref[...], preferred_element_type=F32)
        kcar[...] = jnp.zeros(kcar.shape, F32)
        vcar[...] = jnp.zeros(vcar.shape, F32)
        h = norm_next()
        for c in range(n_chunks):
            project_chunk(h, slot, c)

    def body(cur, nxt):
        k_prev = [jnp.where(step > 1, kcar[nxt, kv], 0.0).astype(BF16) for kv in range(N_KV_HEADS)]
        v_prev = [jnp.where(step > 1, vcar[nxt, kv], 0.0).astype(BF16) for kv in range(N_KV_HEADS)]
        h = norm_next()

        qi = lax.broadcasted_iota(jnp.int32, (WINDOW, KEY_TILE), 0)
        kj = lax.broadcasted_iota(jnp.int32, (WINDOW, KEY_TILE), 1)
        band = (kj >= qi) & (kj <= qi + WINDOW)
        lane_q = lax.broadcasted_iota(jnp.int32, (WINDOW, KV_LANES), 1)
        lane_k = lax.broadcasted_iota(jnp.int32, (KEY_TILE, KV_LANES), 1)
        ones_half = [(lane_k < HEAD_DIM).astype(F32).astype(BF16), (lane_k >= HEAD_DIM).astype(F32).astype(BF16)]

        def block_scores(blk):
            r0 = blk * WINDOW
            qs = jnp.concatenate(
                [qbuf[cur, r0:r0 + WINDOW, gi * KV_LANES:(gi + 1) * KV_LANES] for gi in range(GROUP)], axis=0)
            if blk == 0:
                keys = [jnp.concatenate([k_prev[kv], kbuf[cur, kv, 0:WINDOW, :]], axis=0) for kv in range(N_KV_HEADS)]
            else:
                keys = [kbuf[cur, kv, r0 - WINDOW:r0 + WINDOW, :] for kv in range(N_KV_HEADS)]
            return [lax.dot_general(qs, keys[kv], (((1,), (1,)), ((), ())), preferred_element_type=F32)
                    for kv in range(N_KV_HEADS)]

        scores = block_scores(0)
        for blk in range(n_blk):
            r0 = blk * WINDOW
            if blk == 0:
                mask = band & (kj >= jnp.where(step > 1, 0, WINDOW))
                vals = [jnp.concatenate([v_prev[kv], vbuf[cur, kv, 0:WINDOW, :]], axis=0) for kv in range(N_KV_HEADS)]
            else:
                mask = band
                vals = [vbuf[cur, kv, r0 - WINDOW:r0 + WINDOW, :] for kv in range(N_KV_HEADS)]
            next_scores = block_scores(blk + 1) if blk + 1 < n_blk else None
            if blk % blocks_per_chunk == 0:
                project_chunk(h, nxt, blk // blocks_per_chunk)
            probs, maxes = [], []
            for kv in range(N_KV_HEADS):
                s_all = scores[kv]
                p_kv, m_kv = [], []
                for gi in range(GROUP):
                    head = kv * GROUP + gi
                    s = s_all[gi * WINDOW:(gi + 1) * WINDOW]
                    s = jnp.where(mask, s + bias_ref[head], NEG_INF)
                    m = jnp.maximum(jnp.max(s, axis=-1, keepdims=True), sinks_ref[head])
                    p_kv.append(jnp.exp(s - m).astype(BF16))
                    m_kv.append(m)
                probs.append(jnp.concatenate(p_kv, axis=0))
                maxes.append(m_kv)
            scores = next_scores
            acc = None
            for kv in range(N_KV_HEADS):
                rhs = jnp.concatenate([vals[kv], ones_half[kv]], axis=1)
                part = jnp.dot(probs[kv], rhs, preferred_element_type=F32)
                acc = part if acc is None else acc + part
            outs = []
            for gi in range(GROUP):
                a = acc[gi * WINDOW:(gi + 1) * WINDOW]
                sink_term = jnp.where(lane_q < HEAD_DIM,
                                      jnp.exp(sinks_ref[gi] - maxes[0][gi]),
                                      jnp.exp(sinks_ref[GROUP + gi] - maxes[1][gi]))
                outs.append(a[:, :KV_LANES] / (a[:, KV_LANES:] + sink_term))
            obuf[r0:r0 + WINDOW, :] = jnp.concatenate(outs, axis=1).astype(BF16)

        o_ref[...] = xc_ref[...] + jnp.dot(obuf[...], wo_ref[...], preferred_element_type=F32)

    _two_stage(first, body)


def _attn_prompt(x, o_sample, g, wqkv, wo, bias, sinks, n):
    tiles, nxt, cur = _pipeline_specs(n, ATTN_TILE)
    assert x.shape[0] == n + ATTN_TILE and o_sample.shape == (ATTN_TILE, D_MODEL)
    out = pl.BlockSpec((ATTN_TILE, D_MODEL), lambda i: (jnp.where(i == 0, tiles, i - 1), 0))
    sample = pl.BlockSpec((ATTN_TILE, D_MODEL), lambda i: (tiles, 0), pipeline_mode=pl.Buffered(1))
    last = pl.BlockSpec((WINDOW, KV_LANES), lambda i: (0, 0))
    return pl.pallas_call(
        _attn_prompt_kernel,
        grid=(tiles + 1,),
        in_specs=[nxt, cur, sample, _resident((ATTN_TILE, D_MODEL)), _resident((1, D_MODEL)),
                  _resident((D_MODEL, QKV_COLS)),
                  _resident((D_MODEL, D_MODEL)), _resident((N_HEADS, WINDOW, KEY_TILE)), _smem()],
        out_specs=[out, last, last],
        out_shape=[jax.ShapeDtypeStruct(x.shape, F32),
                   jax.ShapeDtypeStruct((WINDOW, KV_LANES), F32),
                   jax.ShapeDtypeStruct((WINDOW, KV_LANES), F32)],
        scratch_shapes=[pltpu.VMEM((2, ATTN_TILE, D_MODEL), BF16),
                        pltpu.VMEM((2, N_KV_HEADS, ATTN_TILE, KV_LANES), BF16),
                        pltpu.VMEM((2, N_KV_HEADS, ATTN_TILE, KV_LANES), BF16),
                        pltpu.VMEM((2, N_KV_HEADS, WINDOW, KV_LANES), F32),
                        pltpu.VMEM((2, N_KV_HEADS, WINDOW, KV_LANES), F32),
                        pltpu.VMEM((ATTN_TILE, D_MODEL), BF16)],
        compiler_params=_params(),
        name="attn_prompt",
    )(x, x, x, o_sample, g, wqkv, wo, bias, sinks)


def _attn_sample_kernel(q_ref, kt_ref, vt_ref, kvn_ref, bias_ref, sink_ref, o_ref, kout_ref, vout_ref, *, steps):
    q_rows = GROUP * steps
    qi = lax.broadcasted_iota(jnp.int32, (q_rows, KEY_TILE), 0) // GROUP
    kj = lax.broadcasted_iota(jnp.int32, (q_rows, KEY_TILE), 1)
    new_at = KEY_TILE - steps
    mask = ((kj < WINDOW) & (kj >= qi)) | ((kj >= new_at) & (kj - new_at <= qi))
    row = lax.broadcasted_iota(jnp.int32, (KV_LANES, KEY_TILE), 0)
    kv_row = (row < HEAD_DIM, row >= HEAD_DIM)
    lane_q = lax.broadcasted_iota(jnp.int32, (q_rows, KV_LANES), 1)
    lane_c = lax.broadcasted_iota(jnp.int32, (KV_LANES, WINDOW), 1)
    new_k = kvn_ref[0:KV_LANES, :]
    new_v = kvn_ref[KV_LANES:, :]

    def new_cols(new_t, b):
        return pltpu.roll(new_t, (WINDOW - steps - b * steps) % WINDOW, 1)

    def shifted_cache(cache_t, new_last):
        return jnp.where(lane_c < WINDOW - steps, pltpu.roll(cache_t, WINDOW - steps, 1), new_last)

    batches = range(kt_ref.shape[0])
    q_all = (q_ref[...] * ATTN_SCALE).reshape(q_ref.shape[0], GROUP, KV_LANES)
    scores = []
    for b in batches:
        kt = kt_ref[b]
        new_last = new_cols(new_k, b)
        kout_ref[b] = shifted_cache(kt, new_last)
        keys = jnp.concatenate([kt, new_last], axis=1)
        q = q_all[b * steps:(b + 1) * steps].reshape(q_rows, KV_LANES).astype(BF16)
        scores.append([jnp.dot(q, jnp.where(kv_row[kv], keys, 0.0).astype(BF16), preferred_element_type=F32)
                       for kv in range(N_KV_HEADS)])
    probs, sink_terms = [], []
    for b in batches:
        p_b, t_b = [], []
        for kv in range(N_KV_HEADS):
            s = jnp.where(mask, scores[b][kv] + bias_ref[kv], NEG_INF)
            sink = sink_ref[kv][:, 0:1]
            m = jnp.maximum(jnp.max(s, axis=-1, keepdims=True), sink)
            p_b.append(jnp.exp(s - m).astype(BF16))
            t_b.append(jnp.exp(sink - m))
        probs.append(p_b)
        sink_terms.append(t_b)
    outs = []
    for b in batches:
        vt = vt_ref[b]
        new_last = new_cols(new_v, b)
        vout_ref[b] = shifted_cache(vt, new_last)
        vals = jnp.concatenate([vt, new_last], axis=1)
        acc = None
        for kv in range(N_KV_HEADS):
            rhs = jnp.concatenate([jnp.where(kv_row[kv], vals, 0.0), jnp.where(kv_row[kv], 1.0, 0.0)], axis=0)
            part = lax.dot_general(probs[b][kv], rhs.astype(BF16), (((1,), (1,)), ((), ())),
                                   preferred_element_type=F32)
            acc = part if acc is None else acc + part
        denom = acc[:, KV_LANES:] + jnp.where(lane_q < HEAD_DIM, sink_terms[b][0], sink_terms[b][1])
        outs.append((acc[:, :KV_LANES] / denom).reshape(steps, GROUP, KV_LANES))
    o_ref[...] = jnp.concatenate(outs, axis=0).reshape(o_ref.shape)


def _attn_sample(q, kt, vt, kvn, bias, sink, steps):
    nb = kt.shape[0]
    q_rows = GROUP * steps
    bt = SAMPLE_BATCH_TILE
    assert bt * steps == WINDOW and nb % bt == 0

    def batch_spec(r):
        return pl.BlockSpec((bt, r, KV_LANES), lambda i: (i, 0, 0))

    rows_spec = pl.BlockSpec((bt * steps, D_MODEL), lambda i: (i, 0))
    return pl.pallas_call(
        functools.partial(_attn_sample_kernel, steps=steps),
        grid=(nb // bt,),
        in_specs=[rows_spec, batch_spec(KV_LANES), batch_spec(KV_LANES),
                  pl.BlockSpec((2 * KV_LANES, bt * steps), lambda i: (0, i)),
                  _resident((N_KV_HEADS, q_rows, KEY_TILE)), _resident((N_KV_HEADS, q_rows, KV_LANES))],
        out_specs=[rows_spec, batch_spec(KV_LANES), batch_spec(KV_LANES)],
        out_shape=[jax.ShapeDtypeStruct((nb * steps, D_MODEL), F32),
                   jax.ShapeDtypeStruct((nb, KV_LANES, WINDOW), F32),
                   jax.ShapeDtypeStruct((nb, KV_LANES, WINDOW), F32)],
        compiler_params=_params(),
        name="attn_sample",
    )(q, kt, vt, kvn, bias, sink)


def kernel(x_prompt, x_sample, state_conv, cache_k, cache_v, g_mix, g_ffn, g_final, w_conv_in, conv_w,
           w_conv_out, w_q, w_k, w_v, w_o, sinks, rel_table, w_gate, w_up, w_down):
    batch, seq, _ = x_prompt.shape
    dec_batch, dec_seq, _ = x_sample.shape
    assert batch == 1 and seq % TOKEN_TILE == 0 and seq % ATTN_TILE == 0
    assert ATTN_TILE % (WINDOW * (D_MODEL // MXU_WIDTH)) == 0
    assert dec_batch % SAMPLE_BATCH_TILE == 0 and dec_seq <= SUBLANES
    assert dec_batch * dec_seq == TOKEN_TILE, "the sample rows ride the token-wise kernels as one tile"
    assert g_mix.shape[0] == 2, "layer 0 is the conv mixer, layer 1 the attention mixer"

    win = w_conv_in[0]
    wout = w_conv_out[0]
    wq = w_q[0].reshape(D_MODEL, N_KV_HEADS, GROUP, HEAD_DIM).transpose(0, 2, 1, 3).reshape(D_MODEL, D_MODEL)
    wqkv = jnp.concatenate([wq, w_k[0], w_v[0]], axis=1).astype(BF16)
    wo = w_o[0].reshape(N_KV_HEADS, GROUP, HEAD_DIM, D_MODEL).transpose(1, 0, 2, 3).reshape(D_MODEL, D_MODEL)
    wo = wo.astype(BF16)
    wkvt = jnp.concatenate([w_k[0], w_v[0]], axis=1).T.astype(BF16)
    gm = g_mix.reshape(2, 1, D_MODEL)
    gf = g_ffn.reshape(2, 1, D_MODEL)
    gfin = g_final.reshape(1, D_MODEL)
    cw = conv_w[0]
    sink_vec = sinks[0]

    bucket_p = jnp.asarray(np.tile(_t5_bucket_np(WINDOW - np.arange(KEY_TILE)[None, :]), (SUBLANES, 1)))
    key_s = np.arange(KEY_TILE)[None, :]
    key_pos = np.where(key_s < WINDOW, key_s, key_s - (KEY_TILE - dec_seq) + WINDOW)
    dist_s = np.arange(dec_seq)[:, None] + WINDOW - key_pos
    bucket_s = jnp.asarray(np.repeat(_t5_bucket_np(dist_s), GROUP, axis=0))

    x, tail_p, u_s, (wg, wu, wd), (bias_p, bias_s, sink_s) = _conv(
        x_prompt.reshape(seq, D_MODEL), x_sample, state_conv[0], gm[0], win, cw, wout,
        w_gate, w_up, w_down, dec_seq, bucket_p, bucket_s, rel_table, sink_vec)
    state_conv_prompt = tail_p[SUBLANES - CONV_STATE:].reshape(1, batch, CONV_STATE, D_MODEL)
    state_conv_sample = u_s[None]
    x, q_s, kv_new_t = _ffn(x, gf, wg, wu, wd, gfin, 0, gm[1], wqkv, wkvt)

    kt = cache_k[0].transpose(0, 2, 3, 1).reshape(dec_batch, KV_LANES, WINDOW)
    vt = cache_v[0].transpose(0, 2, 3, 1).reshape(dec_batch, KV_LANES, WINDOW)
    o_s, kt_out, vt_out = _attn_sample(q_s, kt, vt, kv_new_t, bias_s, sink_s, dec_seq)
    x, k_last, v_last = _attn_prompt(x, o_s, gm[1], wqkv, wo, bias_p, sink_vec, seq)
    cache_k_prompt = k_last.reshape(1, batch, WINDOW, N_KV_HEADS, HEAD_DIM)
    cache_v_prompt = v_last.reshape(1, batch, WINDOW, N_KV_HEADS, HEAD_DIM)
    y_prompt, y_sample = _ffn_final(x, gf, wg, wu, wd, gfin, 1, x_sample.shape)
    y_prompt = y_prompt.reshape(batch, seq, D_MODEL)
    cache_k_sample = kt_out.reshape(dec_batch, N_KV_HEADS, HEAD_DIM, WINDOW).transpose(0, 3, 1, 2)[None]
    cache_v_sample = vt_out.reshape(dec_batch, N_KV_HEADS, HEAD_DIM, WINDOW).transpose(0, 3, 1, 2)[None]

    return (y_prompt, y_sample, state_conv_prompt, state_conv_sample,
            cache_k_prompt, cache_k_sample, cache_v_prompt, cache_v_sample)
```

```python
import functools
import math

import numpy as np
import jax
import jax.numpy as jnp
from jax import lax
from jax.experimental import pallas as pl
from jax.experimental.pallas import tpu as pltpu

D_MODEL = 1024
D_FF = 2816
HEAD_DIM = 64
N_HEADS = 16
N_KV_HEADS = 2
GROUP = N_HEADS // N_KV_HEADS
WINDOW = 128
N_BUCKETS = 32
MAX_DISTANCE = 128
CONV_STATE = 2
EPS = 1e-5
NEG_INF = -1e30
ATTN_SCALE = 1.0 / math.sqrt(HEAD_DIM)

KV_LANES = N_KV_HEADS * HEAD_DIM
QKV_COLS = D_MODEL + 2 * KV_LANES
SUBLANES = 8
BF16_SUBLANES = 16
KEY_TILE = 2 * WINDOW

TOKEN_TILE = 512
ATTN_TILE = 512
MXU_WIDTH = 256
FFN_SPLIT = 2
CONV_SPLIT = 2
SAMPLE_BATCH_TILE = 32
VMEM_LIMIT_BYTES = 56 * 1024 * 1024

F32 = jnp.float32
BF16 = jnp.bfloat16


def _params():
    return pltpu.CompilerParams(dimension_semantics=("arbitrary",), vmem_limit_bytes=VMEM_LIMIT_BYTES)


def _resident(shape):
    zeros = (0,) * len(shape)
    return pl.BlockSpec(shape, lambda *_: zeros, pipeline_mode=pl.Buffered(1))


def _layer(shape, layer):
    index = (layer,) + (0,) * len(shape)
    return pl.BlockSpec((None,) + tuple(shape), lambda *_: index, pipeline_mode=pl.Buffered(1))


def _smem():
    return pl.BlockSpec(memory_space=pltpu.SMEM)


def _rms(x, g):
    return x * lax.rsqrt(jnp.mean(x * x, axis=-1, keepdims=True) + EPS) * g


def _pipeline_specs(n, rows):
    tiles = n // rows
    nxt = pl.BlockSpec((rows, D_MODEL), lambda i: (jnp.minimum(i, tiles - 1), 0))
    cur = pl.BlockSpec((rows, D_MODEL), lambda i: (jnp.maximum(i - 1, 0), 0))
    return tiles, nxt, cur


def _two_stage(first, body):
    step = pl.program_id(0)

    @pl.when(step == 0)
    def _():
        first(0)

    @pl.when(step % 2 == 1)
    def _():
        body(0, 1)

    @pl.when((step % 2 == 0) & (step > 0))
    def _():
        body(1, 0)


def _t5_bucket_np(dist):
    n = np.maximum(dist, 0)
    max_exact = N_BUCKETS // 2
    nf = np.maximum(n, 1).astype(np.float32)
    large = max_exact + (np.log(nf / np.float32(max_exact)) / np.float32(math.log(MAX_DISTANCE / max_exact))
                         * np.float32(N_BUCKETS - max_exact)).astype(np.int32)
    large = np.minimum(large, N_BUCKETS - 1)
    return np.where(n < max_exact, n, large).astype(np.int32)


def _ffn_kernel(x_ref, g_ref, wg_ref, wu_ref, wd_ref, gf_ref, o_ref, *, final_norm):
    rows = x_ref.shape[0] // FFN_SPLIT
    halves = [pl.ds(i * rows, rows) for i in range(FFN_SPLIT)]
    acts = []
    for r in halves:
        h = _rms(x_ref[r, :], g_ref[...]).astype(BF16)
        gate = jnp.dot(h, wg_ref[...], preferred_element_type=F32)
        up = jnp.dot(h, wu_ref[...], preferred_element_type=F32)
        acts.append((gate * jax.nn.sigmoid(gate) * up).astype(BF16))
    for r, act in zip(halves, acts):
        y = x_ref[r, :] + jnp.dot(act, wd_ref[...], preferred_element_type=F32)
        if final_norm:
            y = _rms(y, gf_ref[...])
        o_ref[r, :] = y


def _ffn_weight_specs(layer):
    return [_layer((1, D_MODEL), layer), _layer((D_MODEL, D_FF), layer), _layer((D_MODEL, D_FF), layer),
            _layer((D_FF, D_MODEL), layer), _resident((1, D_MODEL))]


def _ffn_proj_kernel(x_ref, g_ref, wg_ref, wu_ref, wd_ref, gf_ref, gq_ref, wq_ref, wkvt_ref,
                     o_ref, q_ref, kvt_ref):
    _ffn_kernel(x_ref, g_ref, wg_ref, wu_ref, wd_ref, gf_ref, o_ref, final_norm=False)

    @pl.when(pl.program_id(0) == pl.num_programs(0) - 1)
    def _():
        h = _rms(o_ref[...], gq_ref[...]).astype(BF16)
        q_ref[...] = jnp.dot(h, wq_ref[...], preferred_element_type=F32)
        kvt_ref[...] = lax.dot_general(wkvt_ref[...], h, (((1,), (1,)), ((), ())), preferred_element_type=F32)


def _ffn(x, g, wg, wu, wd, g_final, layer, g_q, wqkv, wkvt):
    n = x.shape[0]
    tile = pl.BlockSpec((TOKEN_TILE, D_MODEL), lambda i: (i, 0))

    def once(shape):
        return pl.BlockSpec(shape, lambda i: (0, 0))

    return pl.pallas_call(
        _ffn_proj_kernel,
        grid=(n // TOKEN_TILE,),
        in_specs=[tile] + _ffn_weight_specs(layer) + [_layer((1, D_MODEL), layer + 1), _resident((D_MODEL, D_MODEL)),
                                                      _resident((2 * KV_LANES, D_MODEL))],
        out_specs=[tile, once((TOKEN_TILE, D_MODEL)), once((2 * KV_LANES, TOKEN_TILE))],
        out_shape=[jax.ShapeDtypeStruct((n, D_MODEL), F32),
                   jax.ShapeDtypeStruct((TOKEN_TILE, D_MODEL), F32),
                   jax.ShapeDtypeStruct((2 * KV_LANES, TOKEN_TILE), F32)],
        compiler_params=_params(),
        name="ffn",
    )(x, g, wg, wu, wd, g_final, g_q, wqkv, wkvt)


def _ffn_final_kernel(x_ref, g_ref, wg_ref, wu_ref, wd_ref, gf_ref, o_ref, os_ref):
    _ffn_kernel(x_ref, g_ref, wg_ref, wu_ref, wd_ref, gf_ref, o_ref, final_norm=True)

    @pl.when(pl.program_id(0) == 0)
    def _():
        os_ref[...] = o_ref[...].reshape(os_ref.shape)


def _ffn_final(x, g, wg, wu, wd, g_final, layer, sample_shape):
    n = x.shape[0]
    tiles = n // TOKEN_TILE - 1
    shape = (TOKEN_TILE, D_MODEL)
    assert math.prod(sample_shape) == TOKEN_TILE * D_MODEL
    return pl.pallas_call(
        _ffn_final_kernel,
        grid=(tiles + 1,),
        in_specs=[pl.BlockSpec(shape, lambda i: (jnp.where(i == 0, tiles, i - 1), 0))] + _ffn_weight_specs(layer),
        out_specs=[pl.BlockSpec(shape, lambda i: (jnp.maximum(i - 1, 0), 0)),
                   pl.BlockSpec(sample_shape, lambda i: (0,) * len(sample_shape))],
        out_shape=[jax.ShapeDtypeStruct((tiles * TOKEN_TILE, D_MODEL), F32),
                   jax.ShapeDtypeStruct(sample_shape, F32)],
        compiler_params=_params(),
        name="ffn_final",
    )(x, g, wg, wu, wd, g_final)


def _conv_gates(h, win_ref):
    cx = jnp.dot(h, win_ref[:, D_MODEL:], preferred_element_type=F32)
    u = cx[:, :D_MODEL] * cx[:, D_MODEL:]
    b = jnp.dot(h, win_ref[:, :D_MODEL], preferred_element_type=F32)
    return b, u


def _conv_kernel(x_ref, xs_ref, prev_ref, g_ref, win_ref, cw_ref, wout_ref, wg_ref, wu_ref, wd_ref,
                 bucket_p_ref, bucket_s_ref, table_ref, sinks_ref,
                 o_ref, tail_out_ref, us_ref, wg_out_ref, wu_out_ref, wd_out_ref,
                 bias_p_ref, bias_s_ref, sink_s_ref, tail_ref, *, steps):
    step = pl.program_id(0)
    last = pl.num_programs(0) - 1
    cw = cw_ref[...]

    @pl.when(step < last)
    def _():
        wg_out_ref[...] = wg_ref[...].astype(BF16)
        wu_out_ref[...] = wu_ref[...].astype(BF16)
        wd_out_ref[...] = wd_ref[...].astype(BF16)
        tail = jnp.where(step > 0, tail_ref[...], 0.0)
        rows = x_ref.shape[0] // CONV_SPLIT
        groups = [pl.ds(i * rows, rows) for i in range(CONV_SPLIT)]
        gates = [_conv_gates(_rms(x_ref[r, :], g_ref[...]), win_ref) for r in groups]
        r8 = lax.broadcasted_iota(jnp.int32, (SUBLANES, D_MODEL), 0)
        for r, (b, u) in zip(groups, gates):
            u1 = pltpu.roll(u, 1, 0)
            u2 = pltpu.roll(u, 2, 0)
            u1 = jnp.concatenate([jnp.where(r8 < 1, pltpu.roll(tail, 1, 0), u1[:SUBLANES]), u1[SUBLANES:]], axis=0)
            u2 = jnp.concatenate([jnp.where(r8 < 2, pltpu.roll(tail, 2, 0), u2[:SUBLANES]), u2[SUBLANES:]], axis=0)
            v = cw[0:1] * u2 + cw[1:2] * u1 + cw[2:3] * u
            o_ref[r, :] = x_ref[r, :] + jnp.dot(b * v, wout_ref[...], preferred_element_type=F32)
            tail = u[rows - SUBLANES:]
        tail_ref[...] = tail
        tail_out_ref[...] = tail[SUBLANES - CONV_STATE:]

    @pl.when(step == last)
    def _():
        x = xs_ref[...].reshape(o_ref.shape)
        b, u = _conv_gates(_rms(x, g_ref[...]), win_ref)
        u3 = u.reshape(xs_ref.shape)
        prev = prev_ref[...]
        u1 = jnp.concatenate([prev[:, CONV_STATE - 1:, :], u3[:, :steps - 1, :]], axis=1)
        u2 = jnp.concatenate([prev, u3[:, :steps - CONV_STATE, :]], axis=1)
        v = (cw[0:1] * u2 + cw[1:2] * u1 + cw[2:3] * u3).reshape(u.shape)
        o_ref[...] = x + jnp.dot(b * v, wout_ref[...], preferred_element_type=F32)
        us_ref[...] = u3[:, steps - CONV_STATE:, :]
        _bias_tables(bucket_p_ref, bucket_s_ref, table_ref, sinks_ref, bias_p_ref, bias_s_ref, sink_s_ref)


def _conv(x, xs, prev, g, win, cw, wout, w_gate, w_up, w_down, steps, bucket_p, bucket_s, table, sinks):
    n = x.shape[0]
    tiles = n // TOKEN_TILE
    batches = xs.shape[0]
    assert xs.shape == (batches, steps, D_MODEL) and batches * steps == TOKEN_TILE
    tile_in = pl.BlockSpec((TOKEN_TILE, D_MODEL), lambda i: (jnp.minimum(i, tiles - 1), 0))
    tile_out = pl.BlockSpec((TOKEN_TILE, D_MODEL), lambda i: (i, 0))

    def slabs(w):
        flat = w.reshape(-1, w.shape[-1])
        rows = flat.shape[0] // tiles
        assert rows * tiles == flat.shape[0] and rows % BF16_SUBLANES == 0
        return flat, pl.BlockSpec((rows, flat.shape[1]), lambda i: (jnp.minimum(i, tiles - 1), 0))

    (wg, wg_spec), (wu, wu_spec), (wd, wd_spec) = slabs(w_gate), slabs(w_up), slabs(w_down)
    q_rows = bucket_s.shape[0]
    table_shapes = [(N_HEADS, WINDOW, KEY_TILE), (N_KV_HEADS, q_rows, KEY_TILE), (N_KV_HEADS, q_rows, KV_LANES)]
    y, tail, u_s, wg_b, wu_b, wd_b, bias_p, bias_s, sink_s = pl.pallas_call(
        functools.partial(_conv_kernel, steps=steps),
        grid=(tiles + 1,),
        in_specs=[tile_in, _resident(xs.shape), _resident(prev.shape), _layer((1, D_MODEL), 0),
                  _resident((D_MODEL, 3 * D_MODEL)),
                  _resident((3, D_MODEL)), _resident((D_MODEL, D_MODEL)), wg_spec, wu_spec, wd_spec,
                  _resident((SUBLANES, KEY_TILE)), _resident((q_rows, KEY_TILE)), _smem(), _smem()],
        out_specs=[tile_out, pl.BlockSpec((CONV_STATE, D_MODEL), lambda i: (0, 0)),
                   pl.BlockSpec((batches, CONV_STATE, D_MODEL), lambda i: (0, 0, 0)), wg_spec, wu_spec, wd_spec]
        + [pl.BlockSpec(shape, lambda i: (0, 0, 0)) for shape in table_shapes],
        out_shape=[jax.ShapeDtypeStruct((n + TOKEN_TILE, D_MODEL), F32),
                   jax.ShapeDtypeStruct((CONV_STATE, D_MODEL), F32),
                   jax.ShapeDtypeStruct((batches, CONV_STATE, D_MODEL), F32),
                   jax.ShapeDtypeStruct(wg.shape, BF16),
                   jax.ShapeDtypeStruct(wu.shape, BF16),
                   jax.ShapeDtypeStruct(wd.shape, BF16)]
        + [jax.ShapeDtypeStruct(shape, F32) for shape in table_shapes],
        scratch_shapes=[pltpu.VMEM((SUBLANES, D_MODEL), F32)],
        compiler_params=_params(),
        name="conv",
    )(x, xs, prev, g, win, cw, wout, wg, wu, wd, bucket_p, bucket_s, table, sinks)
    weights = (wg_b.reshape(w_gate.shape), wu_b.reshape(w_up.shape), wd_b.reshape(w_down.shape))
    return y, tail, u_s, weights, (bias_p, bias_s, sink_s)


def _lookup_bias(bucket, table_ref, head):
    acc = jnp.zeros(bucket.shape, F32)
    for b in range(N_BUCKETS):
        acc = jnp.where(bucket == b, table_ref[b, head], acc)
    return acc


def _bias_tables(bucket_p_ref, bucket_s_ref, table_ref, sinks_ref, bias_p_ref, bias_s_ref, sink_s_ref):
    bucket_row0 = bucket_p_ref[...]
    for head in range(N_HEADS):
        row0 = _lookup_bias(bucket_row0, table_ref, head)
        rows = jnp.concatenate([row0] * (WINDOW // SUBLANES), axis=0)
        bias_p_ref[head] = pltpu.roll(rows, 0, 1, stride=1, stride_axis=0)
    bucket_s = bucket_s_ref[...]
    grp = lax.broadcasted_iota(jnp.int32, bucket_s.shape, 0) % GROUP
    grp_l = lax.broadcasted_iota(jnp.int32, (bucket_s.shape[0], KV_LANES), 0) % GROUP
    for kv in range(N_KV_HEADS):
        bias = jnp.zeros(bucket_s.shape, F32)
        sink = jnp.zeros((bucket_s.shape[0], KV_LANES), F32)
        for gi in range(GROUP):
            head = kv * GROUP + gi
            bias = jnp.where(grp == gi, _lookup_bias(bucket_s, table_ref, head), bias)
            sink = jnp.where(grp_l == gi, sinks_ref[head], sink)
        bias_s_ref[kv] = bias
        sink_s_ref[kv] = sink


def _attn_prompt_kernel(xn_ref, xc_ref, xs_ref, os_ref, g_ref, wqkv_ref, wo_ref, bias_ref, sinks_ref,
                        o_ref, klast_ref, vlast_ref, qbuf, kbuf, vbuf, kcar, vcar, obuf):
    step = pl.program_id(0)
    rows = xn_ref.shape[0]
    n_blk = rows // WINDOW
    n_chunks = D_MODEL // MXU_WIDTH
    blocks_per_chunk = n_blk // n_chunks

    def norm_next():
        return _rms(xn_ref[...], g_ref[...]).astype(BF16)

    def project_chunk(h, slot, c):
        c0 = c * MXU_WIDTH
        part = jnp.dot(h, wqkv_ref[:, c0:c0 + MXU_WIDTH], preferred_element_type=F32)
        qbuf[slot, :, c0:c0 + MXU_WIDTH] = (part * ATTN_SCALE).astype(BF16)
        if c < n_chunks - 1:
            return
        kv_part = jnp.dot(h, wqkv_ref[:, D_MODEL:], preferred_element_type=F32)
        k = kv_part[:, :KV_LANES]
        v = kv_part[:, KV_LANES:]
        klast_ref[...] = k[rows - WINDOW:]
        vlast_ref[...] = v[rows - WINDOW:]
        lane = lax.broadcasted_iota(jnp.int32, (rows, KV_LANES), 1)
        for kv, half in enumerate((lane < HEAD_DIM, lane >= HEAD_DIM)):
            k_half = jnp.where(half, k, 0.0)
            v_half = jnp.where(half, v, 0.0)
            kbuf[slot, kv] = k_half.astype(BF16)
            vbuf[slot, kv] = v_half.astype(BF16)
            kcar[slot, kv] = k_half[rows - WINDOW:]
            vcar[slot, kv] = v_half[rows - WINDOW:]

    def first(slot):
        o_ref[...] = xs_ref[...] + jnp.dot(os_ref[...].astype(BF16), wo_ref[...], preferred_element_type=F32)
        kcar[...] = jnp.zeros(kcar.shape, F32)
        vcar[...] = jnp.zeros(vcar.shape, F32)
        h = norm_next()
        for c in range(n_chunks):
            project_chunk(h, slot, c)

    def body(cur, nxt):
        k_prev = [jnp.where(step > 1, kcar[nxt, kv], 0.0).astype(BF16) for kv in range(N_KV_HEADS)]
        v_prev = [jnp.where(step > 1, vcar[nxt, kv], 0.0).astype(BF16) for kv in range(N_KV_HEADS)]
        h = norm_next()

        qi = lax.broadcasted_iota(jnp.int32, (WINDOW, KEY_TILE), 0)
        kj = lax.broadcasted_iota(jnp.int32, (WINDOW, KEY_TILE), 1)
        band = (kj >= qi) & (kj <= qi + WINDOW)
        lane_q = lax.broadcasted_iota(jnp.int32, (WINDOW, KV_LANES), 1)
        lane_k = lax.broadcasted_iota(jnp.int32, (KEY_TILE, KV_LANES), 1)
        ones_half = [(lane_k < HEAD_DIM).astype(F32).astype(BF16), (lane_k >= HEAD_DIM).astype(F32).astype(BF16)]

        def block_scores(blk):
            r0 = blk * WINDOW
            qs = jnp.concatenate(
                [qbuf[cur, r0:r0 + WINDOW, gi * KV_LANES:(gi + 1) * KV_LANES] for gi in range(GROUP)], axis=0)
            if blk == 0:
                keys = [jnp.concatenate([k_prev[kv], kbuf[cur, kv, 0:WINDOW, :]], axis=0) for kv in range(N_KV_HEADS)]
            else:
                keys = [kbuf[cur, kv, r0 - WINDOW:r0 + WINDOW, :] for kv in range(N_KV_HEADS)]
            return [lax.dot_general(qs, keys[kv], (((1,), (1,)), ((), ())), preferred_element_type=F32)
                    for kv in range(N_KV_HEADS)]

        scores = block_scores(0)
        for blk in range(n_blk):
            r0 = blk * WINDOW
            if blk == 0:
                mask = band & (kj >= jnp.where(step > 1, 0, WINDOW))
                vals = [jnp.concatenate([v_prev[kv], vbuf[cur, kv, 0:WINDOW, :]], axis=0) for kv in range(N_KV_HEADS)]
            else:
                mask = band
                vals = [vbuf[cur, kv, r0 - WINDOW:r0 + WINDOW, :] for kv in range(N_KV_HEADS)]
            next_scores = block_scores(blk + 1) if blk + 1 < n_blk else None
            if blk % blocks_per_chunk == 0:
                project_chunk(h, nxt, blk // blocks_per_chunk)
            probs, maxes = [], []
            for kv in range(N_KV_HEADS):
                s_all = scores[kv]
                p_kv, m_kv = [], []
                for gi in range(GROUP):
                    head = kv * GROUP + gi
                    s = s_all[gi * WINDOW:(gi + 1) * WINDOW]
                    s = jnp.where(mask, s + bias_ref[head], NEG_INF)
                    m = jnp.maximum(jnp.max(s, axis=-1, keepdims=True), sinks_ref[head])
                    p_kv.append(jnp.exp(s - m).astype(BF16))
                    m_kv.append(m)
                probs.append(jnp.concatenate(p_kv, axis=0))
                maxes.append(m_kv)
            scores = next_scores
            acc = None
            for kv in range(N_KV_HEADS):
                rhs = jnp.concatenate([vals[kv], ones_half[kv]], axis=1)
                part = jnp.dot(probs[kv], rhs, preferred_element_type=F32)
                acc = part if acc is None else acc + part
            outs = []
            for gi in range(GROUP):
                a = acc[gi * WINDOW:(gi + 1) * WINDOW]
                sink_term = jnp.where(lane_q < HEAD_DIM,
                                      jnp.exp(sinks_ref[gi] - maxes[0][gi]),
                                      jnp.exp(sinks_ref[GROUP + gi] - maxes[1][gi]))
                outs.append(a[:, :KV_LANES] / (a[:, KV_LANES:] + sink_term))
            obuf[r0:r0 + WINDOW, :] = jnp.concatenate(outs, axis=1).astype(BF16)

        o_ref[...] = xc_ref[...] + jnp.dot(obuf[...], wo_ref[...], preferred_element_type=F32)

    _two_stage(first, body)


def _attn_prompt(x, o_sample, g, wqkv, wo, bias, sinks, n):
    tiles, nxt, cur = _pipeline_specs(n, ATTN_TILE)
    assert x.shape[0] == n + ATTN_TILE and o_sample.shape == (ATTN_TILE, D_MODEL)
    out = pl.BlockSpec((ATTN_TILE, D_MODEL), lambda i: (jnp.where(i == 0, tiles, i - 1), 0))
    sample = pl.BlockSpec((ATTN_TILE, D_MODEL), lambda i: (tiles, 0), pipeline_mode=pl.Buffered(1))
    last = pl.BlockSpec((WINDOW, KV_LANES), lambda i: (0, 0))
    return pl.pallas_call(
        _attn_prompt_kernel,
        grid=(tiles + 1,),
        in_specs=[nxt, cur, sample, _resident((ATTN_TILE, D_MODEL)), _layer((1, D_MODEL), 1),
                  _resident((D_MODEL, QKV_COLS)),
                  _resident((D_MODEL, D_MODEL)), _resident((N_HEADS, WINDOW, KEY_TILE)), _smem()],
        out_specs=[out, last, last],
        out_shape=[jax.ShapeDtypeStruct(x.shape, F32),
                   jax.ShapeDtypeStruct((WINDOW, KV_LANES), F32),
                   jax.ShapeDtypeStruct((WINDOW, KV_LANES), F32)],
        scratch_shapes=[pltpu.VMEM((2, ATTN_TILE, D_MODEL), BF16),
                        pltpu.VMEM((2, N_KV_HEADS, ATTN_TILE, KV_LANES), BF16),
                        pltpu.VMEM((2, N_KV_HEADS, ATTN_TILE, KV_LANES), BF16),
                        pltpu.VMEM((2, N_KV_HEADS, WINDOW, KV_LANES), F32),
                        pltpu.VMEM((2, N_KV_HEADS, WINDOW, KV_LANES), F32),
                        pltpu.VMEM((ATTN_TILE, D_MODEL), BF16)],
        compiler_params=_params(),
        name="attn_prompt",
    )(x, x, x, o_sample, g, wqkv, wo, bias, sinks)


def _attn_sample_kernel(q_ref, kt_ref, vt_ref, kvn_ref, bias_ref, sink_ref, o_ref, kout_ref, vout_ref, *, steps):
    q_rows = GROUP * steps
    qi = lax.broadcasted_iota(jnp.int32, (q_rows, KEY_TILE), 0) // GROUP
    kj = lax.broadcasted_iota(jnp.int32, (q_rows, KEY_TILE), 1)
    new_at = KEY_TILE - steps
    mask = ((kj < WINDOW) & (kj >= qi)) | ((kj >= new_at) & (kj - new_at <= qi))
    row = lax.broadcasted_iota(jnp.int32, (KV_LANES, KEY_TILE), 0)
    kv_row = (row < HEAD_DIM, row >= HEAD_DIM)
    lane_q = lax.broadcasted_iota(jnp.int32, (q_rows, KV_LANES), 1)
    lane_c = lax.broadcasted_iota(jnp.int32, (KV_LANES, WINDOW), 1)
    new_k = kvn_ref[0:KV_LANES, :]
    new_v = kvn_ref[KV_LANES:, :]

    def new_cols(new_t, b):
        return pltpu.roll(new_t, (WINDOW - steps - b * steps) % WINDOW, 1)

    def shifted_cache(cache_t, new_last):
        return jnp.where(lane_c < WINDOW - steps, pltpu.roll(cache_t, WINDOW - steps, 1), new_last)

    batches = range(kt_ref.shape[0])
    q_all = (q_ref[...] * ATTN_SCALE).reshape(q_ref.shape[0], GROUP, KV_LANES)
    scores = []
    for b in batches:
        kt = kt_ref[b]
        new_last = new_cols(new_k, b)
        kout_ref[b] = shifted_cache(kt, new_last)
        keys = jnp.concatenate([kt, new_last], axis=1)
        q = q_all[b * steps:(b + 1) * steps].reshape(q_rows, KV_LANES).astype(BF16)
        scores.append([jnp.dot(q, jnp.where(kv_row[kv], keys, 0.0).astype(BF16), preferred_element_type=F32)
                       for kv in range(N_KV_HEADS)])
    probs, sink_terms = [], []
    for b in batches:
        p_b, t_b = [], []
        for kv in range(N_KV_HEADS):
            s = jnp.where(mask, scores[b][kv] + bias_ref[kv], NEG_INF)
            sink = sink_ref[kv][:, 0:1]
            m = jnp.maximum(jnp.max(s, axis=-1, keepdims=True), sink)
            p_b.append(jnp.exp(s - m).astype(BF16))
            t_b.append(jnp.exp(sink - m))
        probs.append(p_b)
        sink_terms.append(t_b)
    outs = []
    for b in batches:
        vt = vt_ref[b]
        new_last = new_cols(new_v, b)
        vout_ref[b] = shifted_cache(vt, new_last)
        vals = jnp.concatenate([vt, new_last], axis=1)
        acc = None
        for kv in range(N_KV_HEADS):
            rhs = jnp.concatenate([jnp.where(kv_row[kv], vals, 0.0), jnp.where(kv_row[kv], 1.0, 0.0)], axis=0)
            part = lax.dot_general(probs[b][kv], rhs.astype(BF16), (((1,), (1,)), ((), ())),
                                   preferred_element_type=F32)
            acc = part if acc is None else acc + part
        denom = acc[:, KV_LANES:] + jnp.where(lane_q < HEAD_DIM, sink_terms[b][0], sink_terms[b][1])
        outs.append((acc[:, :KV_LANES] / denom).reshape(steps, GROUP, KV_LANES))
    o_ref[...] = jnp.concatenate(outs, axis=0).reshape(o_ref.shape)


def _attn_sample(q, kt, vt, kvn, bias, sink, steps):
    nb = kt.shape[0]
    q_rows = GROUP * steps
    bt = SAMPLE_BATCH_TILE
    assert bt * steps == WINDOW and nb % bt == 0

    def batch_spec(r):
        return pl.BlockSpec((bt, r, KV_LANES), lambda i: (i, 0, 0))

    rows_spec = pl.BlockSpec((bt * steps, D_MODEL), lambda i: (i, 0))
    return pl.pallas_call(
        functools.partial(_attn_sample_kernel, steps=steps),
        grid=(nb // bt,),
        in_specs=[rows_spec, batch_spec(KV_LANES), batch_spec(KV_LANES),
                  pl.BlockSpec((2 * KV_LANES, bt * steps), lambda i: (0, i)),
                  _resident((N_KV_HEADS, q_rows, KEY_TILE)), _resident((N_KV_HEADS, q_rows, KV_LANES))],
        out_specs=[rows_spec, batch_spec(KV_LANES), batch_spec(KV_LANES)],
        out_shape=[jax.ShapeDtypeStruct((nb * steps, D_MODEL), F32),
                   jax.ShapeDtypeStruct((nb, KV_LANES, WINDOW), F32),
                   jax.ShapeDtypeStruct((nb, KV_LANES, WINDOW), F32)],
        compiler_params=_params(),
        name="attn_sample",
    )(q, kt, vt, kvn, bias, sink)


def kernel(x_prompt, x_sample, state_conv, cache_k, cache_v, g_mix, g_ffn, g_final, w_conv_in, conv_w,
           w_conv_out, w_q, w_k, w_v, w_o, sinks, rel_table, w_gate, w_up, w_down):
    batch, seq, _ = x_prompt.shape
    dec_batch, dec_seq, _ = x_sample.shape
    assert batch == 1 and seq % TOKEN_TILE == 0 and seq % ATTN_TILE == 0
    assert ATTN_TILE % (WINDOW * (D_MODEL // MXU_WIDTH)) == 0
    assert dec_batch % SAMPLE_BATCH_TILE == 0 and dec_seq <= SUBLANES
    assert dec_batch * dec_seq == TOKEN_TILE, "the sample rows ride the token-wise kernels as one tile"
    assert g_mix.shape[0] == 2, "layer 0 is the conv mixer, layer 1 the attention mixer"

    win = w_conv_in[0]
    wout = w_conv_out[0]
    wq = w_q[0].reshape(D_MODEL, N_KV_HEADS, GROUP, HEAD_DIM).transpose(0, 2, 1, 3).reshape(D_MODEL, D_MODEL)
    wqkv = jnp.concatenate([wq, w_k[0], w_v[0]], axis=1).astype(BF16)
    wo = w_o[0].reshape(N_KV_HEADS, GROUP, HEAD_DIM, D_MODEL).transpose(1, 0, 2, 3).reshape(D_MODEL, D_MODEL)
    wo = wo.astype(BF16)
    wkvt = jnp.concatenate([w_k[0], w_v[0]], axis=1).T.astype(BF16)
    gm = g_mix.reshape(2, 1, D_MODEL)
    gf = g_ffn.reshape(2, 1, D_MODEL)
    gfin = g_final.reshape(1, D_MODEL)
    cw = conv_w[0]
    sink_vec = sinks[0]

    bucket_p = jnp.asarray(np.tile(_t5_bucket_np(WINDOW - np.arange(KEY_TILE)[None, :]), (SUBLANES, 1)))
    key_s = np.arange(KEY_TILE)[None, :]
    key_pos = np.where(key_s < WINDOW, key_s, key_s - (KEY_TILE - dec_seq) + WINDOW)
    dist_s = np.arange(dec_seq)[:, None] + WINDOW - key_pos
    bucket_s = jnp.asarray(np.repeat(_t5_bucket_np(dist_s), GROUP, axis=0))

    x, tail_p, u_s, (wg, wu, wd), (bias_p, bias_s, sink_s) = _conv(
        x_prompt.reshape(seq, D_MODEL), x_sample, state_conv[0], gm, win, cw, wout,
        w_gate, w_up, w_down, dec_seq, bucket_p, bucket_s, rel_table, sink_vec)
    state_conv_prompt = tail_p.reshape(1, batch, CONV_STATE, D_MODEL)
    state_conv_sample = u_s[None]
    x, q_s, kv_new_t = _ffn(x, gf, wg, wu, wd, gfin, 0, gm, wqkv, wkvt)

    kt = cache_k[0].transpose(0, 2, 3, 1).reshape(dec_batch, KV_LANES, WINDOW)
    vt = cache_v[0].transpose(0, 2, 3, 1).reshape(dec_batch, KV_LANES, WINDOW)
    o_s, kt_out, vt_out = _attn_sample(q_s, kt, vt, kv_new_t, bias_s, sink_s, dec_seq)
    x, k_last, v_last = _attn_prompt(x, o_s, gm, wqkv, wo, bias_p, sink_vec, seq)
    cache_k_prompt = k_last.reshape(1, batch, WINDOW, N_KV_HEADS, HEAD_DIM)
    cache_v_prompt = v_last.reshape(1, batch, WINDOW, N_KV_HEADS, HEAD_DIM)
    y_prompt, y_sample = _ffn_final(x, gf, wg, wu, wd, gfin, 1, x_sample.shape)
    y_prompt = y_prompt.reshape(batch, seq, D_MODEL)
    cache_k_sample = kt_out.reshape(dec_batch, N_KV_HEADS, HEAD_DIM, WINDOW).transpose(0, 3, 1, 2)[None]
    cache_v_sample = vt_out.reshape(dec_batch, N_KV_HEADS, HEAD_DIM, WINDOW).transpose(0, 3, 1, 2)[None]

    return (y_prompt, y_sample, state_conv_prompt, state_conv_sample,
            cache_k_prompt, cache_k_sample, cache_v_prompt, cache_v_sample)
```
